```python
import math
import jax
import jax.numpy as jnp
from jax import lax
import numpy as np

D_MODEL = 1024
BATCH = 2
SEQ = 8192
DEPTH = 1
DEC_BATCH = 128
DEC_SEQ = 4
PAST_LEN = 8192
PAGE_SIZE = 128

D_LRU = 1280
LRU_HEADS = 16
LRU_BLOCK = D_LRU // LRU_HEADS
CONV_W = 4
LRU_C = 8.0
N_HEADS = 16
HEAD_DIM = 64
KV_HEADS = 4
GROUP = N_HEADS // KV_HEADS
L_CMP = 32
STRIDE = 16
CMP_R = L_CMP // STRIDE
CMP_HIDDEN = 128
L_SEL = 64
N_SEL = 16
N_LOCAL = 2
WINDOW = 512
Q_BLOCK = 128
N_BUCKETS = 32
MAX_DISTANCE = 128
N_GROUPS = 4
EXPERTS_PER_GROUP = 8
N_EXPERTS = N_GROUPS * EXPERTS_PER_GROUP
TOP_K = 2
D_EXPERT = 256
MOE_BLOCK = 128
EPS = 1e-6
NEG = -1e30
FORCED = 1e9
Q_DIM = N_HEADS * HEAD_DIM
KV_DIM = KV_HEADS * HEAD_DIM
SPLIT_POINTS = (D_LRU, 2 * D_LRU, 2 * D_LRU + Q_DIM, 2 * D_LRU + Q_DIM + 4 * KV_DIM,
                2 * D_LRU + Q_DIM + 6 * KV_DIM, 2 * D_LRU + Q_DIM + 6 * KV_DIM + 3 * N_HEADS)
D_IN = SPLIT_POINTS[-1] + 2 * D_MODEL

kernel_name = 'hawk_nsa_hmoe_decode_step'


def rms_norm(x, g):
    x = x.astype(jnp.float32)
    return x * lax.rsqrt(jnp.mean(x * x, axis=-1, keepdims=True) + EPS) * g.astype(jnp.float32)


def split_projection(z):
    lx, lg, q, kv, kvw, ng, mg = jnp.split(z, SPLIT_POINTS, axis=-1)
    lead = z.shape[:-1]
    return (lx, lg, q.reshape(*lead, N_HEADS, HEAD_DIM), kv.reshape(*lead, 4, KV_HEADS, HEAD_DIM),
            kvw.reshape(*lead, 2, KV_HEADS, HEAD_DIM), ng.reshape(*lead, N_HEADS, 3),
            mg.reshape(*lead, 2, D_MODEL))


def t5_bucket(rel):
    n = jnp.maximum(rel, 0)
    exact = N_BUCKETS // 2
    nf = jnp.maximum(n, exact).astype(jnp.float32)
    large = exact + (jnp.log(nf / exact) / math.log(MAX_DISTANCE / exact)
                     * (N_BUCKETS - exact)).astype(jnp.int32)
    return jnp.where(n < exact, n, jnp.minimum(large, N_BUCKETS - 1))


def masked_softmax(logits, mask):
    p = jax.nn.softmax(jnp.where(mask, logits, NEG), axis=-1)
    return jnp.where(mask, p, 0.0)


def chunk_project(x, w1):
    *lead, length, kvh, hd = x.shape
    ch = jnp.swapaxes(x.reshape(*lead, length // STRIDE, STRIDE, kvh, hd), -3, -2)
    ch = ch.reshape(*lead, length // STRIDE, kvh, STRIDE * hd)
    return jnp.einsum('...ckf,rfh->...ckrh', ch, w1.reshape(CMP_R, STRIDE * hd, CMP_HIDDEN))


def compress_blocks(p, b1, w2):
    nc = p.shape[-4] - CMP_R + 1
    pre = b1
    for j in range(CMP_R):
        pre = pre + p[..., j:j + nc, :, j, :]
    return jax.nn.gelu(pre) @ w2


def nsa_core(q, q_pos, kc, vc, nsb, gather_sel, kw, vw, kw_pos, gates, rel_bias):
    nq = q.shape[0]
    nc = kc.shape[0]
    qg = q.astype(jnp.float32).reshape(nq, KV_HEADS, GROUP, HEAD_DIM) * (HEAD_DIM ** -0.5)
    tbl = rel_bias.astype(jnp.float32).reshape(N_BUCKETS, KV_HEADS, GROUP).transpose(1, 0, 2)
    kvh = jnp.arange(KV_HEADS)[None, :, None]

    kc_end = jnp.arange(nc) * STRIDE + (L_CMP - 1)
    rel_c = q_pos[:, None] - kc_end[None, :]
    logit_c = jnp.einsum('qkgd,nkd->qkgn', qg, kc) + tbl[:, t5_bucket(rel_c)].transpose(1, 0, 3, 2)
    p_c = masked_softmax(logit_c, (rel_c >= 0)[:, None, None, :])
    o_c = jnp.einsum('qkgn,nkd->qkgd', p_c, vc)

    ci = jnp.arange(nc)[:, None] * STRIDE
    sj = jnp.arange(nsb)[None, :] * L_SEL
    overlap = ((ci < sj + L_SEL) & (ci + L_CMP > sj)).astype(jnp.float32)
    imp = jnp.einsum('qkn,nj->qkj', p_c.sum(axis=2), overlap)
    blk = jnp.arange(nsb)[None, None, :]
    qb = (q_pos // L_SEL)[:, None, None]
    forced = (blk == 0) | (blk > qb - N_LOCAL)
    score = jnp.where(blk > qb, -FORCED, jnp.where(forced, FORCED, imp))
    _, sel = lax.top_k(score, min(N_SEL, nsb))
    n = sel.shape[-1]
    kv_s = gather_sel(sel)
    ks = kv_s[..., 0, :].reshape(nq, KV_HEADS, n * L_SEL, HEAD_DIM)
    vs = kv_s[..., 1, :].reshape(nq, KV_HEADS, n * L_SEL, HEAD_DIM)
    ks_pos = (sel[..., None] * L_SEL + jnp.arange(L_SEL)).reshape(nq, KV_HEADS, n * L_SEL)
    rel_s = q_pos[:, None, None] - ks_pos
    logit_s = jnp.einsum('qkgd,qksd->qkgs', qg, ks) + jnp.swapaxes(tbl[kvh, t5_bucket(rel_s)], -1, -2)
    p_s = masked_softmax(logit_s, (rel_s >= 0)[:, :, None, :])
    o_s = jnp.einsum('qkgs,qksd->qkgd', p_s, vs)

    rel_w = q_pos[:, None] - kw_pos[None, :]
    m_w = (rel_w >= 0) & (rel_w < WINDOW) & (kw_pos >= 0)[None, :]
    logit_w = jnp.einsum('qkgd,nkd->qkgn', qg, kw) + tbl[:, t5_bucket(rel_w)].transpose(1, 0, 3, 2)
    p_w = masked_softmax(logit_w, m_w[:, None, None, :])
    o_w = jnp.einsum('qkgn,nkd->qkgd', p_w, vw)

    g = jax.nn.sigmoid(gates.astype(jnp.float32)).reshape(nq, KV_HEADS, GROUP, 3)
    o = g[..., 0:1] * o_c + g[..., 1:2] * o_s + g[..., 2:3] * o_w
    return o.reshape(nq, Q_DIM)


def nsa_prompt_seq(q, gates, kc, vc, kv_sel, kv_win, rel_bias):
    s = q.shape[0]
    nsb = s // L_SEL
    sel_blocks = kv_sel.reshape(nsb, L_SEL, 2, KV_HEADS, HEAD_DIM)

    def gather_sel(sel):
        kvh = jnp.arange(KV_HEADS)[None, :, None]
        return sel_blocks[sel, :, :, kvh]

    win = jnp.pad(kv_win, ((WINDOW, 0), (0, 0), (0, 0), (0, 0)))

    def query_block(i):
        s0 = i * Q_BLOCK
        qb = lax.dynamic_slice_in_dim(q, s0, Q_BLOCK)
        gb = lax.dynamic_slice_in_dim(gates, s0, Q_BLOCK)
        wb = lax.dynamic_slice_in_dim(win, s0, WINDOW + Q_BLOCK)
        q_pos = s0 + jnp.arange(Q_BLOCK)
        w_pos = s0 - WINDOW + jnp.arange(WINDOW + Q_BLOCK)
        return nsa_core(qb, q_pos, kc, vc, nsb, gather_sel, wb[:, 0], wb[:, 1], w_pos, gb, rel_bias)

    return lax.map(query_block, jnp.arange(s // Q_BLOCK)).reshape(s, Q_DIM)


def nsa_sample_seq(q, gates, kv_new, kvw_new, win_buf, pt, cache_kv, layer, cmp_w1, cmp_b1, cmp_w2, rel_bias):
    ds = q.shape[0]
    past = pt.shape[0] * PAGE_SIZE
    q_pos = past + jnp.arange(ds)

    new_c = jnp.pad(kv_new[:, :2], ((0, (-ds) % STRIDE), (0, 0), (0, 0), (0, 0)))

    def compressed(slot):
        past_rows = cache_kv[layer, pt, :, slot].reshape(past, KV_HEADS, HEAD_DIM)
        p = jnp.concatenate([chunk_project(past_rows, cmp_w1[slot]),
                             chunk_project(new_c[:, slot], cmp_w1[slot])], axis=0)
        return compress_blocks(p, cmp_b1[slot], cmp_w2[slot])

    kc = compressed(0)
    vc = compressed(1)

    ppb = PAGE_SIZE // L_SEL
    past_b = past // L_SEL
    nb_new = -(-ds // L_SEL)
    new_s = jnp.pad(kv_new[:, 2:], ((0, nb_new * L_SEL - ds), (0, 0), (0, 0), (0, 0)))
    new_s = new_s.reshape(nb_new, L_SEL, 2, KV_HEADS, HEAD_DIM)

    def gather_sel(sel):
        kvh = jnp.arange(KV_HEADS)[None, :, None]
        jp = jnp.minimum(sel, past_b - 1)
        phys = pt[jp // ppb][..., None]
        row = (jp % ppb)[..., None] * L_SEL + jnp.arange(L_SEL)
        from_cache = cache_kv[layer, phys, row, 2:, kvh[..., None]]
        from_new = new_s[jnp.clip(sel - past_b, 0, nb_new - 1), :, :, kvh]
        return jnp.where((sel < past_b)[..., None, None, None], from_cache, from_new)

    wb = win_buf.shape[0]
    win = jnp.concatenate([win_buf, kvw_new], axis=0)
    w_pos = past - wb + jnp.arange(wb + ds)
    return nsa_core(q, q_pos, kc, vc, past_b + nb_new, gather_sel, win[:, 0], win[:, 1], w_pos, gates, rel_bias)


def _linear_combine(left, right):
    a_l, b_l = left
    a_r, b_r = right
    return a_l * a_r, a_r * b_l + b_r


def rglru_branch(x_in, gate_in, conv_buf, h0, conv_w, conv_b, w_gate_a, b_gate_a, w_gate_x, b_gate_x, lru_lambda):
    n, t, d = x_in.shape
    xp = jnp.concatenate([conv_buf.astype(jnp.float32), x_in.astype(jnp.float32)], axis=1)
    xc = lax.conv_general_dilated(xp, conv_w.astype(jnp.float32)[:, None, :], window_strides=(1,), padding='VALID',
                                  dimension_numbers=('NWC', 'WIO', 'NWC'), feature_group_count=d) + conv_b
    xh = xc.reshape(n, t, LRU_HEADS, LRU_BLOCK)
    gate_x = jax.nn.sigmoid(jnp.einsum('nthi,hij->nthj', xh, w_gate_x).reshape(n, t, d) + b_gate_x)
    gate_a = jax.nn.sigmoid(jnp.einsum('nthi,hij->nthj', xh, w_gate_a).reshape(n, t, d) + b_gate_a)
    log_a = -LRU_C * gate_a * jax.nn.softplus(-lru_lambda.astype(jnp.float32))
    a = jnp.exp(log_a)
    u = jnp.sqrt(-jnp.expm1(2.0 * log_a)) * (gate_x * xc)
    u = u.at[:, 0].add(a[:, 0] * h0.astype(jnp.float32))
    _, h = lax.associative_scan(_linear_combine, (a, u), axis=1)
    y = h * jax.nn.gelu(gate_in.astype(jnp.float32))
    return y, xp[:, -(CONV_W - 1):], h[:, -1]


def merge_branches(y_lru, y_nsa, mg, w_lru_out, w_nsa_out, w_out):
    gate = jax.nn.sigmoid(mg.astype(jnp.float32))
    mixed = gate[..., 0, :] * (y_lru @ w_lru_out) + gate[..., 1, :] * (y_nsa @ w_nsa_out)
    return mixed @ w_out


def hier_moe(h, w_rg, b_rg, w_re, b_re, w_eg, w_eu, w_ed):
    t, d = h.shape
    glog = h @ w_rg + b_rg
    elog = (h @ w_re + b_re).reshape(t, N_GROUPS, EXPERTS_PER_GROUP)
    grp = jnp.argmax(glog, axis=-1)
    p_grp = jnp.take_along_axis(jax.nn.softmax(glog, axis=-1), grp[:, None], axis=1)
    el = jnp.take_along_axis(elog, grp[:, None, None], axis=1)[:, 0]
    top_v, top_i = lax.top_k(el, TOP_K)
    wts = (jax.nn.softmax(top_v, axis=-1) * p_grp).reshape(-1)
    eid = (grp[:, None] * EXPERTS_PER_GROUP + top_i).reshape(-1)
    n_assign = t * TOP_K
    order = jnp.argsort(eid)
    e_sorted = eid[order]
    tok = order // TOP_K
    counts = jnp.bincount(eid, length=N_EXPERTS)
    padded = (counts + MOE_BLOCK - 1) // MOE_BLOCK * MOE_BLOCK
    pad_end = jnp.cumsum(padded)
    pad_start = pad_end - padded
    start = jnp.cumsum(counts) - counts
    dest = pad_start[e_sorted] + jnp.arange(n_assign) - start[e_sorted]
    n_blocks = -(-n_assign // MOE_BLOCK) + N_EXPERTS
    buf = jnp.zeros((n_blocks * MOE_BLOCK, d), h.dtype).at[dest].set(h[tok])
    blk_e = jnp.minimum(jnp.searchsorted(pad_end, jnp.arange(n_blocks) * MOE_BLOCK, side='right'), N_EXPERTS - 1)

    def expert_block(args):
        xb, e = args
        return (jax.nn.silu(xb @ w_eg[e]) * (xb @ w_eu[e])) @ w_ed[e]

    yb = lax.map(expert_block, (buf.reshape(n_blocks, MOE_BLOCK, d), blk_e)).reshape(n_blocks * MOE_BLOCK, d)
    contrib = yb[dest] * wts[order][:, None]
    return jnp.zeros((t, d), contrib.dtype).at[tok].add(contrib)


def decoder_layer(xp, xs, cache_kv, layer, win_buf, conv_buf, h_buf, page_table,
                  g_mix, w_in, conv_w, conv_b, w_gate_a, b_gate_a, w_gate_x, b_gate_x, lru_lambda,
                  cmp_w1, cmp_b1, cmp_w2, w_lru_out, w_nsa_out, w_out,
                  g_ffn, w_rg, b_rg, w_re, b_re, w_eg, w_eu, w_ed, rel_bias):
    bsz, seq, _ = xp.shape
    lru_w = (conv_w, conv_b, w_gate_a, b_gate_a, w_gate_x, b_gate_x, lru_lambda)

    lx, lg, q, kv, kvw, ng, mg = split_projection(rms_norm(xp, g_mix) @ w_in)
    y_lru, conv_p, h_p = rglru_branch(lx, lg, jnp.zeros((bsz, CONV_W - 1, D_LRU), jnp.float32),
                                      jnp.zeros((bsz, D_LRU), jnp.float32), *lru_w)
    kc = compress_blocks(chunk_project(kv[:, :, 0], cmp_w1[0]), cmp_b1[0], cmp_w2[0])
    vc = compress_blocks(chunk_project(kv[:, :, 1], cmp_w1[1]), cmp_b1[1], cmp_w2[1])

    def prompt_seq(q1, g1, kc1, vc1, kvs1, kvw1):
        return nsa_prompt_seq(q1, g1, kc1, vc1, kvs1, kvw1, rel_bias)

    y_nsa = jax.vmap(prompt_seq)(q, ng, kc, vc, kv[:, :, 2:], kvw)
    xp = xp + merge_branches(y_lru, y_nsa, mg, w_lru_out, w_nsa_out, w_out)
    kv_p = kv
    win_p = kvw[:, -min(WINDOW, seq):]

    lx, lg, q, kv, kvw, ng, mg = split_projection(rms_norm(xs, g_mix) @ w_in)
    y_lru, conv_s, h_s = rglru_branch(lx, lg, conv_buf, h_buf, *lru_w)

    def sample_seq(q1, g1, kv1, kvw1, wbuf1, pt1):
        return nsa_sample_seq(q1, g1, kv1, kvw1, wbuf1, pt1, cache_kv, layer, cmp_w1, cmp_b1, cmp_w2, rel_bias)

    y_nsa = jax.vmap(sample_seq)(q, ng, kv, kvw, win_buf, page_table)
    xs = xs + merge_branches(y_lru, y_nsa, mg, w_lru_out, w_nsa_out, w_out)
    win_all = jnp.concatenate([win_buf, kvw], axis=1)
    kv_s = kv
    win_s = win_all[:, -min(WINDOW, win_all.shape[1]):]

    tokens = jnp.concatenate([rms_norm(xp, g_ffn).reshape(-1, D_MODEL), rms_norm(xs, g_ffn).reshape(-1, D_MODEL)], axis=0)
    ff = hier_moe(tokens, w_rg, b_rg, w_re, b_re, w_eg, w_eu, w_ed)
    n_p = bsz * seq
    xp = xp + ff[:n_p].reshape(xp.shape)
    xs = xs + ff[n_p:].reshape(xs.shape)
    return xp, xs, (kv_p, win_p, conv_p, h_p, kv_s, win_s, conv_s, h_s)


def setup_inputs(seed: int = 0) -> dict:
    key = jax.random.key(seed)
    ks = jax.random.split(key, 32)
    f32 = jnp.float32
    n_pages = PAST_LEN // PAGE_SIZE
    n_used = DEC_BATCH * n_pages
    n_phys = n_used + (n_used + 3) // 4
    wb = min(WINDOW, PAST_LEN)

    def nrm(k, shape, scale):
        return jax.random.normal(k, shape, f32) * scale

    u = jax.random.uniform(ks[15], (DEPTH, D_LRU), f32, 0.9, 0.999)
    s = u ** (1.0 / LRU_C)
    return {
        'x_prompt': nrm(ks[0], (BATCH, SEQ, D_MODEL), 1.0),
        'x_sample': nrm(ks[1], (DEC_BATCH, DEC_SEQ, D_MODEL), 1.0),
        'cache_kv': nrm(ks[2], (DEPTH, n_phys, PAGE_SIZE, 4, KV_HEADS, HEAD_DIM), 1.0),
        'state_kv_win': nrm(ks[3], (DEPTH, DEC_BATCH, wb, 2, KV_HEADS, HEAD_DIM), 1.0),
        'state_conv': nrm(ks[4], (DEPTH, DEC_BATCH, CONV_W - 1, D_LRU), 1.0),
        'state_h': nrm(ks[5], (DEPTH, DEC_BATCH, D_LRU), 0.5),
        'page_table': jax.random.permutation(ks[6], n_phys)[:n_used].reshape(DEC_BATCH, n_pages).astype(jnp.int32),
        'g_mix': 1.0 + nrm(ks[7], (DEPTH, D_MODEL), 0.01),
        'w_in': nrm(ks[8], (DEPTH, D_MODEL, D_IN), D_MODEL ** -0.5),
        'conv_w': nrm(ks[9], (DEPTH, CONV_W, D_LRU), CONV_W ** -0.5),
        'conv_b': nrm(ks[10], (DEPTH, D_LRU), 0.01),
        'w_gate_a': nrm(ks[11], (DEPTH, LRU_HEADS, LRU_BLOCK, LRU_BLOCK), LRU_BLOCK ** -0.5),
        'b_gate_a': nrm(ks[12], (DEPTH, D_LRU), 0.01),
        'w_gate_x': nrm(ks[13], (DEPTH, LRU_HEADS, LRU_BLOCK, LRU_BLOCK), LRU_BLOCK ** -0.5),
        'b_gate_x': nrm(ks[14], (DEPTH, D_LRU), 0.01),
        'lru_lambda': jnp.log(s) - jnp.log1p(-s),
        'cmp_w1': nrm(ks[16], (DEPTH, 2, L_CMP * HEAD_DIM, CMP_HIDDEN), (L_CMP * HEAD_DIM) ** -0.5),
        'cmp_b1': nrm(ks[17], (DEPTH, 2, CMP_HIDDEN), 0.01),
        'cmp_w2': nrm(ks[18], (DEPTH, 2, CMP_HIDDEN, HEAD_DIM), CMP_HIDDEN ** -0.5),
        'w_lru_out': nrm(ks[19], (DEPTH, D_LRU, D_MODEL), D_LRU ** -0.5),
        'w_nsa_out': nrm(ks[20], (DEPTH, Q_DIM, D_MODEL), Q_DIM ** -0.5),
        'w_out': nrm(ks[21], (DEPTH, D_MODEL, D_MODEL), D_MODEL ** -0.5),
        'g_ffn': 1.0 + nrm(ks[22], (DEPTH, D_MODEL), 0.01),
        'w_router_group': nrm(ks[23], (DEPTH, D_MODEL, N_GROUPS), D_MODEL ** -0.5),
        'b_router_group': nrm(ks[24], (DEPTH, N_GROUPS), 0.01),
        'w_router_expert': nrm(ks[25], (DEPTH, D_MODEL, N_EXPERTS), D_MODEL ** -0.5),
        'b_router_expert': nrm(ks[26], (DEPTH, N_EXPERTS), 0.01),
        'w_exp_gate': nrm(ks[27], (DEPTH, N_EXPERTS, D_MODEL, D_EXPERT), D_MODEL ** -0.5),
        'w_exp_up': nrm(ks[28], (DEPTH, N_EXPERTS, D_MODEL, D_EXPERT), D_MODEL ** -0.5),
        'w_exp_down': nrm(ks[29], (DEPTH, N_EXPERTS, D_EXPERT, D_MODEL), D_EXPERT ** -0.5),
        'rel_bias': nrm(ks[30], (N_BUCKETS, N_HEADS), 0.5),
        'g_final': 1.0 + nrm(ks[31], (D_MODEL,), 0.01),
    }


def reference(x_prompt, x_sample, cache_kv, state_kv_win, state_conv, state_h, page_table,
              g_mix, w_in, conv_w, conv_b, w_gate_a, b_gate_a, w_gate_x, b_gate_x, lru_lambda,
              cmp_w1, cmp_b1, cmp_w2, w_lru_out, w_nsa_out, w_out,
              g_ffn, w_router_group, b_router_group, w_router_expert, b_router_expert,
              w_exp_gate, w_exp_up, w_exp_down, rel_bias, g_final):
    xp, xs = x_prompt, x_sample
    layer_states = []
    for l in range(DEPTH):
        xp, xs, st = decoder_layer(
            xp, xs, cache_kv, l, state_kv_win[l], state_conv[l], state_h[l], page_table,
            g_mix[l], w_in[l], conv_w[l], conv_b[l], w_gate_a[l], b_gate_a[l], w_gate_x[l], b_gate_x[l],
            lru_lambda[l], cmp_w1[l], cmp_b1[l], cmp_w2[l], w_lru_out[l], w_nsa_out[l], w_out[l],
            g_ffn[l], w_router_group[l], b_router_group[l], w_router_expert[l], b_router_expert[l],
            w_exp_gate[l], w_exp_up[l], w_exp_down[l], rel_bias)
        layer_states.append(st)
    (kv_rows_prompt, win_prompt, conv_prompt, h_prompt,
     kv_rows_sample, win_sample, conv_sample, h_sample) = [jnp.stack(s) for s in zip(*layer_states)]
    y_prompt = rms_norm(xp, g_final)
    y_sample = rms_norm(xs, g_final)
    return (y_prompt, y_sample, kv_rows_prompt, win_prompt, conv_prompt, h_prompt,
            kv_rows_sample, win_sample, conv_sample, h_sample)
```

```python
import functools
import math

import numpy as np
import jax
import jax.numpy as jnp
from jax import lax
from jax.experimental import pallas as pl
from jax.experimental.pallas import tpu as pltpu

F32 = jnp.float32
BF16 = jnp.bfloat16

D_MODEL = 1024
D_LRU = 1280
LRU_HEADS = 16
LRU_BLOCK = D_LRU // LRU_HEADS
CONV_W = 4
LRU_C = 8.0
N_HEADS = 16
HEAD_DIM = 64
KV_HEADS = 4
GROUP = N_HEADS // KV_HEADS
L_CMP = 32
STRIDE = 16
CMP_HIDDEN = 128
L_SEL = 64
N_SEL = 16
N_LOCAL = 2
WINDOW = 512
PAGE_SIZE = 128
N_BUCKETS = 32
MAX_DISTANCE = 128
N_GROUPS = 4
EXPERTS_PER_GROUP = 8
N_EXPERTS = N_GROUPS * EXPERTS_PER_GROUP
D_EXPERT = 256
EPS = 1e-6
NEG = -1e30
FORCED = 1e9
Q_DIM = N_HEADS * HEAD_DIM
KV_DIM = KV_HEADS * HEAD_DIM
QT = 128
GQ = GROUP * QT
LANES = 128
VMEM_LIMIT = 56 * 1024 * 1024
KNOCKED = -3e38
ABSENT = -2e38


def _cparams(sem):
    return pltpu.CompilerParams(dimension_semantics=sem, vmem_limit_bytes=VMEM_LIMIT)


def _dot(a, b):
    return jnp.dot(a, b, preferred_element_type=F32)


def _dot_tn(a, b):
    return lax.dot_general(a, b, (((0,), (0,)), ((), ())), preferred_element_type=F32)


def _split3(x):
    hi = x.astype(BF16)
    r1 = x - hi.astype(F32)
    mid = r1.astype(BF16)
    lo = (r1 - mid.astype(F32)).astype(BF16)
    return hi, mid, lo


def _dot_exact_rhs(a_bf16, x):
    hi, mid, lo = _split3(x)
    return _dot(a_bf16, hi) + _dot(a_bf16, mid) + _dot(a_bf16, lo)


def _proj_kernel(x_ref, g_ref, *refs):
    n = len(refs) // 2
    x = x_ref[...]
    xn = x * lax.rsqrt(jnp.mean(x * x, axis=-1, keepdims=True) + EPS) * g_ref[...]
    xb = xn.astype(BF16)
    for w_ref, o_ref in zip(refs[:n], refs[n:]):
        o_ref[...] = _dot(xb, w_ref[...])


def _proj(x2d, g, ws, tm=256):
    t = x2d.shape[0]
    tm = min(tm, t)
    in_specs = [pl.BlockSpec((tm, D_MODEL), lambda i: (i, 0)), pl.BlockSpec((1, D_MODEL), lambda i: (0, 0))]
    in_specs += [pl.BlockSpec(w.shape, lambda i: (0, 0), pipeline_mode=pl.Buffered(1)) for w in ws]
    out_specs = [pl.BlockSpec((tm, w.shape[1]), lambda i: (i, 0)) for w in ws]
    out_shape = [jax.ShapeDtypeStruct((t, w.shape[1]), F32) for w in ws]
    return pl.pallas_call(_proj_kernel, grid=(t // tm,), in_specs=in_specs, out_specs=out_specs,
                          out_shape=out_shape, compiler_params=_cparams(("parallel",)), name="proj")(x2d, g, *ws)


def _softplus(x):
    return jnp.maximum(x, 0.0) + jnp.log1p(jnp.exp(-jnp.abs(x)))


def _lru_gates(xc, wg_ref, bgx_ref, bga_ref, lam_ref):
    gates = _dot(xc.astype(BF16), wg_ref[...])
    gx = jax.nn.sigmoid(gates[:, :D_LRU] + bgx_ref[...])
    ga = jax.nn.sigmoid(gates[:, D_LRU:] + bga_ref[...])
    log_a = -LRU_C * ga * _softplus(-lam_ref[...])
    a = jnp.exp(log_a)
    th = jnp.tanh(log_a)
    u = jnp.sqrt(-2.0 * th / (1.0 - th)) * (gx * xc)
    return a, u


def _rglru_prompt_kernel(lx_ref, lg_ref, cw_ref, cb_ref, wg_ref, bgx_ref, bga_ref, lam_ref,
                         y_ref, conv_ref, h_ref, xext, hc, *, tc):
    t = pl.program_id(1)

    @pl.when(t == 0)
    def _():
        xext[0:8, :] = jnp.zeros((8, D_LRU), F32)
        hc[...] = jnp.zeros_like(hc)

    x = lx_ref[0]
    xext[8:8 + tc, :] = x
    cw = cw_ref[...]
    xc = cb_ref[...] + cw[3:4] * x
    for j in range(CONV_W - 1):
        xc = xc + cw[j:j + 1] * xext[5 + j:5 + j + tc, :]
    a, u = _lru_gates(xc, wg_ref, bgx_ref, bga_ref, lam_ref)
    row = lax.broadcasted_iota(jnp.int32, (tc, D_LRU), 0)
    s = 1
    while s < tc:
        a_sh = pltpu.roll(a, s, 0)
        u_sh = pltpu.roll(u, s, 0)
        m = row >= s
        u = jnp.where(m, a * u_sh + u, u)
        a = jnp.where(m, a * a_sh, a)
        s *= 2
    h = a * hc[0:1, :] + u
    hc[0:1, :] = h[tc - 1:tc, :]
    xext[0:8, :] = x[tc - 8:tc, :]
    y_ref[0] = h * jax.nn.gelu(lg_ref[0])
    conv_ref[0] = x[tc - (CONV_W - 1):tc, :]
    h_ref[0] = h[tc - 1:tc, :]


def _rglru_prompt(lx, lg, cw, cb, wg, bgx, bga, lam, tc=256):
    b, s, _ = lx.shape
    row = lambda shape: pl.BlockSpec(shape, lambda bi, ti: (0, 0))
    return pl.pallas_call(
        functools.partial(_rglru_prompt_kernel, tc=tc), grid=(b, s // tc),
        in_specs=[pl.BlockSpec((1, tc, D_LRU), lambda bi, ti: (bi, ti, 0)),
                  pl.BlockSpec((1, tc, D_LRU), lambda bi, ti: (bi, ti, 0)),
                  row((CONV_W, D_LRU)), row((1, D_LRU)), row((D_LRU, 2 * D_LRU)),
                  row((1, D_LRU)), row((1, D_LRU)), row((1, D_LRU))],
        out_specs=[pl.BlockSpec((1, tc, D_LRU), lambda bi, ti: (bi, ti, 0)),
                   pl.BlockSpec((1, CONV_W - 1, D_LRU), lambda bi, ti: (bi, 0, 0)),
                   pl.BlockSpec((1, 1, D_LRU), lambda bi, ti: (bi, 0, 0))],
        out_shape=[jax.ShapeDtypeStruct((b, s, D_LRU), F32),
                   jax.ShapeDtypeStruct((b, CONV_W - 1, D_LRU), F32),
                   jax.ShapeDtypeStruct((b, 1, D_LRU), F32)],
        scratch_shapes=[pltpu.VMEM((tc + 8, D_LRU), F32), pltpu.VMEM((8, D_LRU), F32)],
        compiler_params=_cparams(("parallel", "arbitrary")), name="rglru_prompt",
    )(lx, lg, cw, cb, wg, bgx, bga, lam)


def _rglru_sample_kernel(lx_ref, lg_ref, cbuf_ref, h0_ref, cw_ref, cb_ref, wg_ref, bgx_ref, bga_ref, lam_ref,
                         y_ref, conv_ref, h_ref, *, ds):
    cw = cw_ref[...]
    xp = [cbuf_ref[j] for j in range(CONV_W - 1)] + [lx_ref[j] for j in range(ds)]
    h = h0_ref[...]
    for t in range(ds):
        xc = cb_ref[...]
        for j in range(CONV_W):
            xc = xc + cw[j:j + 1] * xp[t + j]
        a, u = _lru_gates(xc, wg_ref, bgx_ref, bga_ref, lam_ref)
        h = a * h + u
        y_ref[t] = h * jax.nn.gelu(lg_ref[t])
    for j in range(CONV_W - 1):
        conv_ref[j] = xp[ds + j]
    h_ref[...] = h


def _rglru_sample(lx_t, lg_t, cbuf_t, h0, cw, cb, wg, bgx, bga, lam):
    ds, n, _ = lx_t.shape
    full = lambda a: pl.BlockSpec(a.shape, lambda i: (0,) * a.ndim)
    args = (lx_t, lg_t, cbuf_t, h0, cw, cb, wg, bgx, bga, lam)
    out_shape = [jax.ShapeDtypeStruct((ds, n, D_LRU), F32), jax.ShapeDtypeStruct((CONV_W - 1, n, D_LRU), F32),
                 jax.ShapeDtypeStruct((n, D_LRU), F32)]
    return pl.pallas_call(
        functools.partial(_rglru_sample_kernel, ds=ds), grid=(1,),
        in_specs=[full(a) for a in args], out_specs=[full(o) for o in out_shape], out_shape=out_shape,
        compiler_params=_cparams(("arbitrary",)), name="rglru_sample")(*args)


HEADS_PER_TILE = LANES // HEAD_DIM


def _chunk_features(load, k):
    c, o = k // HEADS_PER_TILE, (k % HEADS_PER_TILE) * HEAD_DIM
    return jnp.concatenate([load(s, c)[:, o:o + HEAD_DIM] for s in range(STRIDE)], axis=1).astype(BF16)


def _compress_prompt_kernel(xa_ref, xb_ref, w1_ref, b1_ref, w2_ref, o_ref, *, ncp):
    rows = lax.broadcasted_iota(jnp.int32, (ncp, HEAD_DIM), 0)
    xs = (xa_ref, xb_ref)
    for k in range(KV_HEADS):
        feat = _chunk_features(lambda s, c: xs[c][0, pl.ds(s, ncp, stride=STRIDE), :], k)
        p = _dot(feat, w1_ref[0])
        pre = b1_ref[0] + p[:, :CMP_HIDDEN] + pltpu.roll(p[:, CMP_HIDDEN:], ncp - 1, 0)
        phi = _dot(jax.nn.gelu(pre).astype(BF16), w2_ref[0])
        o_ref[0, 0, k, 0:ncp, :] = jnp.zeros((ncp, HEAD_DIM), F32)
        o_ref[0, 0, k, ncp:2 * ncp, :] = jnp.where(rows < ncp - 1, phi, 0.0)


def _compress_prompt(kv, w1f, b1, w2):
    b, s, _ = kv.shape
    ncp = s // STRIDE
    return pl.pallas_call(
        functools.partial(_compress_prompt_kernel, ncp=ncp), grid=(b, 2),
        in_specs=[pl.BlockSpec((1, s, LANES), lambda bi, sl: (bi, 0, 2 * sl)),
                  pl.BlockSpec((1, s, LANES), lambda bi, sl: (bi, 0, 2 * sl + 1)),
                  pl.BlockSpec((1, STRIDE * HEAD_DIM, 2 * CMP_HIDDEN), lambda bi, sl: (sl, 0, 0)),
                  pl.BlockSpec((1, 1, CMP_HIDDEN), lambda bi, sl: (sl, 0, 0)),
                  pl.BlockSpec((1, CMP_HIDDEN, HEAD_DIM), lambda bi, sl: (sl, 0, 0))],
        out_specs=pl.BlockSpec((1, 1, KV_HEADS, 2 * ncp, HEAD_DIM), lambda bi, sl: (sl, bi, 0, 0, 0)),
        out_shape=jax.ShapeDtypeStruct((2, b, KV_HEADS, 2 * ncp, HEAD_DIM), F32),
        compiler_params=_cparams(("parallel", "parallel")), name="compress_prompt")(kv, kv, w1f, b1, w2)


def _compress_sample_kernel(pt_ref, *refs, pps, nsteps):
    pages = refs[:pps]
    new_ref, w1_ref, b1_ref, w2_ref, o_ref, xs, pscr = refs[pps:]
    st = pl.program_id(1)
    cps = PAGE_SIZE // STRIDE
    m = pps * cps
    ntok = nsteps * m
    tiles_per_slot = KV_DIM // LANES

    def stage(ref, j):
        for c in range(2 * tiles_per_slot):
            xs[c, j * PAGE_SIZE:(j + 1) * PAGE_SIZE, :] = ref[0, :, c * LANES:(c + 1) * LANES]

    def project(nrows, row0):
        for sl in range(2):
            for k in range(KV_HEADS):
                feat = _chunk_features(
                    lambda s, c: xs[sl * tiles_per_slot + c, pl.ds(s, nrows, stride=STRIDE), :], k)
                pscr[sl, k, pl.ds(row0, nrows), :] = _dot(feat, w1_ref[sl])

    for j, pg in enumerate(pages):
        stage(pg, j)
    project(m, pl.multiple_of(st * m, m))

    @pl.when(st == nsteps - 1)
    def _():
        stage(new_ref, 0)
        project(cps, ntok)
        for sl in range(2):
            toks = []
            for k in range(KV_HEADS):
                pre = (b1_ref[sl] + pscr[sl, k, 0:ntok, 0:CMP_HIDDEN]
                       + pscr[sl, k, 1:ntok + 1, CMP_HIDDEN:2 * CMP_HIDDEN])
                toks.append(_dot(jax.nn.gelu(pre).astype(BF16), w2_ref[sl]))
            o_ref[0, sl] = jnp.concatenate(toks, axis=1)


def _compress_sample(cache, page_table, new_rows, w1f, b1, w2, pps=16):
    db, npg = page_table.shape
    pps = min(pps, npg)
    nsteps = npg // pps
    cps = PAGE_SIZE // STRIDE
    ntok = npg * cps

    def page_spec(j):
        return pl.BlockSpec((1, PAGE_SIZE, 2 * KV_DIM), lambda bi, st, pt: (pt[bi, st * pps + j], 0, 0))

    const = lambda shape: pl.BlockSpec(shape, lambda bi, st, pt: (0,) * len(shape))
    grid_spec = pltpu.PrefetchScalarGridSpec(
        num_scalar_prefetch=1, grid=(db, nsteps),
        in_specs=[page_spec(j) for j in range(pps)] + [
            pl.BlockSpec((1, PAGE_SIZE, 2 * KV_DIM), lambda bi, st, pt: (bi, 0, 0)),
            const((2, STRIDE * HEAD_DIM, 2 * CMP_HIDDEN)), const((2, 1, CMP_HIDDEN)),
            const((2, CMP_HIDDEN, HEAD_DIM))],
        out_specs=pl.BlockSpec((1, 2, ntok, KV_DIM), lambda bi, st, pt: (bi, 0, 0, 0)),
        scratch_shapes=[pltpu.VMEM((2 * KV_DIM // LANES, pps * PAGE_SIZE, LANES), F32),
                        pltpu.VMEM((2, KV_HEADS, ntok + cps, 2 * CMP_HIDDEN), F32)])
    return pl.pallas_call(
        functools.partial(_compress_sample_kernel, pps=pps, nsteps=nsteps), grid_spec=grid_spec,
        out_shape=jax.ShapeDtypeStruct((db, 2, ntok, KV_DIM), F32),
        compiler_params=_cparams(("parallel", "arbitrary")), name="compress_sample",
    )(page_table, *([cache] * pps), new_rows, w1f, b1, w2)


def _topk_mask(score, nrows):
    ridx = lax.broadcasted_iota(jnp.int32, score.shape, 0).astype(F32)

    def body(_, carry):
        sc, sel = carry
        mx = jnp.max(sc, axis=0, keepdims=True)
        first = jnp.min(jnp.where(sc == mx, ridx, float(nrows)), axis=0, keepdims=True)
        hit = ridx == first
        return jnp.where(hit, KNOCKED, sc), jnp.where(hit, 1.0, sel)

    return lax.fori_loop(0, N_SEL, body, (score, jnp.zeros_like(score)))[1]


def _softmax_cols(s):
    m = jnp.max(s, axis=0, keepdims=True)
    e = jnp.where(s > 0.5 * NEG, jnp.exp(s - m), 0.0)
    l = jnp.sum(e, axis=0, keepdims=True)
    return e * (1.0 / jnp.maximum(l, 1e-30))


def _online_update(s, vt, m_ref, l_ref, acc_ref, idx, tn):
    m_old = m_ref[idx]
    m_new = jnp.maximum(m_old, jnp.max(s, axis=0, keepdims=True))
    alpha = jnp.exp(m_old - m_new)
    p = jnp.exp(s - m_new)
    l_ref[idx] = alpha * l_ref[idx] + jnp.sum(p, axis=0, keepdims=True)
    pv = _dot_tn(vt, p.astype(BF16)) if tn else _dot(vt, p.astype(BF16))
    acc_ref[idx] = alpha * acc_ref[idx] + pv
    m_ref[idx] = m_new


def _nsa_prompt_kernel(qT_ref, gT_ref, kc_ref, vc_ref, ksel_ref, vselT_ref, kwin_ref, vwinT_ref,
                       tc_ref, tiles_ref, crow_ref, ov_ref, y_ref,
                       qk_s, madd_s, oc_s, m_s, l_s, acc_s, *, ncp, nsb):
    i = pl.program_id(1)
    qT = qT_ref[0]
    for k in range(KV_HEADS):
        qk_s[k] = (jnp.concatenate([qT[(GROUP * k + g) * HEAD_DIM:(GROUP * k + g + 1) * HEAD_DIM, :]
                                    for g in range(GROUP)], axis=1) * (HEAD_DIM ** -0.5)).astype(BF16)
    m_s[...] = jnp.full(m_s.shape, NEG, F32)
    l_s[...] = jnp.zeros_like(l_s)
    acc_s[...] = jnp.zeros_like(acc_s)

    start = pl.multiple_of(8 * i + 8, 8)
    tok_ok = lax.broadcasted_iota(jnp.int32, (ncp, 1), 0) >= ncp - 8 - 8 * i
    rblk = lax.broadcasted_iota(jnp.int32, (nsb, QT), 0)
    qhalf = jnp.where(lax.broadcasted_iota(jnp.int32, (nsb, QT), 1) >= L_SEL, 1, 0)
    r_qb = nsb - 2 + qhalf
    exists = rblk >= nsb - 2 - 2 * i
    forced = (rblk == nsb - 2 - 2 * i) | (rblk > r_qb - N_LOCAL)
    for k in range(KV_HEADS):
        kc = kc_ref[0, 0, k, pl.ds(start, ncp), :].astype(BF16)
        vc = vc_ref[0, 0, k, pl.ds(start, ncp), :].astype(BF16)
        s = jnp.where(tok_ok, _dot(kc, qk_s[k]) + tc_ref[k], NEG)
        pn = _softmax_cols(s)
        oc_s[k] = _dot_tn(vc, pn.astype(BF16))
        psum = pn[:, 0:QT]
        for g in range(1, GROUP):
            psum = psum + pn[:, g * QT:(g + 1) * QT]
        imp = _dot_exact_rhs(ov_ref[...], psum)
        score = jnp.where(rblk > r_qb, -FORCED, jnp.where(forced, FORCED, imp))
        score = jnp.where(exists, score, ABSENT)
        madd_s[k] = (_topk_mask(score, nsb) - 1.0) * (-NEG)

    def tile(t, bias_of, k_ref, vT_ref, branch):
        kt = k_ref[0, t]
        vt = vT_ref[0, t]
        for k in range(KV_HEADS):
            s = _dot(kt[:, k * HEAD_DIM:(k + 1) * HEAD_DIM], qk_s[k]) + bias_of(k)
            if branch == 0:
                r0 = 2 * t - 2 * i + nsb - 2
                mt = jnp.concatenate([jnp.broadcast_to(madd_s[k, pl.ds(r0 + j, 1), :], (L_SEL, QT))
                                      for j in range(QT // L_SEL)], axis=0)
                s = s + jnp.concatenate([mt] * GROUP, axis=1)
            _online_update(s, vt[k * HEAD_DIM:(k + 1) * HEAD_DIM, :], m_s, l_s, acc_s, branch * KV_HEADS + k, False)

    far = lambda k: crow_ref[k]
    near = lambda d: (lambda k: tiles_ref[d, k])

    def far_body(t, c):
        tile(t, far, ksel_ref, vselT_ref, 0)
        return c

    lax.fori_loop(0, jnp.maximum(i - 1, 0), far_body, 0)

    @pl.when(i >= 1)
    def _():
        tile(i - 1, near(1), ksel_ref, vselT_ref, 0)

    tile(i, near(0), ksel_ref, vselT_ref, 0)

    for d, bias_of in ((4, near(2)), (3, far), (2, far), (1, near(1))):
        @pl.when(i >= d)
        def _(d=d, bias_of=bias_of):
            tile(i - d, bias_of, kwin_ref, vwinT_ref, 1)
    tile(i, near(0), kwin_ref, vwinT_ref, 1)

    gate = jax.nn.sigmoid(gT_ref[0])
    for k in range(KV_HEADS):
        o_s = acc_s[k] * (1.0 / l_s[k])
        o_w = acc_s[KV_HEADS + k] * (1.0 / l_s[KV_HEADS + k])
        o_c = oc_s[k]
        for g in range(GROUP):
            h = GROUP * k + g
            cols = slice(g * QT, (g + 1) * QT)
            y_ref[0, h * HEAD_DIM:(h + 1) * HEAD_DIM, :] = (
                gate[3 * h:3 * h + 1] * o_c[:, cols] + gate[3 * h + 1:3 * h + 2] * o_s[:, cols]
                + gate[3 * h + 2:3 * h + 3] * o_w[:, cols])


def _nsa_prompt(qT, gT, kcvc, ksel, vselT, kwin, vwinT, tc, tiles, crow, ov):
    b, _, s = qT.shape
    nq = s // QT
    ncp = s // STRIDE
    nsb = s // L_SEL
    seq4 = lambda a: pl.BlockSpec((1,) + a.shape[1:], lambda bi, qi: (bi, 0, 0, 0), pipeline_mode=pl.Buffered(1))
    const = lambda a: pl.BlockSpec(a.shape, lambda bi, qi: (0,) * a.ndim, pipeline_mode=pl.Buffered(1))
    cmp_spec = lambda sl: pl.BlockSpec((1, 1, KV_HEADS, 2 * ncp, HEAD_DIM), lambda bi, qi: (sl, bi, 0, 0, 0),
                                       pipeline_mode=pl.Buffered(1))
    return pl.pallas_call(
        functools.partial(_nsa_prompt_kernel, ncp=ncp, nsb=nsb), grid=(b, nq),
        in_specs=[pl.BlockSpec((1, Q_DIM, QT), lambda bi, qi: (bi, 0, qi)),
                  pl.BlockSpec((1, 3 * N_HEADS, QT), lambda bi, qi: (bi, 0, qi)),
                  cmp_spec(0), cmp_spec(1), seq4(ksel), seq4(vselT), seq4(kwin), seq4(vwinT),
                  const(tc), const(tiles), const(crow), const(ov)],
        out_specs=pl.BlockSpec((1, Q_DIM, QT), lambda bi, qi: (bi, 0, qi)),
        out_shape=jax.ShapeDtypeStruct((b, Q_DIM, s), F32),
        scratch_shapes=[pltpu.VMEM((KV_HEADS, HEAD_DIM, GQ), BF16), pltpu.VMEM((KV_HEADS, nsb, QT), F32),
                        pltpu.VMEM((KV_HEADS, HEAD_DIM, GQ), F32), pltpu.VMEM((2 * KV_HEADS, 1, GQ), F32),
                        pltpu.VMEM((2 * KV_HEADS, 1, GQ), F32), pltpu.VMEM((2 * KV_HEADS, HEAD_DIM, GQ), F32)],
        compiler_params=_cparams(("parallel", "arbitrary")), name="nsa_prompt",
    )(qT, gT, kcvc, kcvc, ksel, vselT, kwin, vwinT, tc, tiles, crow, ov)


def _nsa_sample_kernel(pt_ref, *refs, pps, nsteps, nsb):
    pages = refs[:pps]
    (qbd_ref, gate_ref, kcvc_ref, win_ref, new_ref, bc_ref, bw_ref, blast_ref, bnew_ref, crow_ref, ov_ref, gg_ref,
     o_ref, madd_s, oc_s, ow_s, m_s, l_s, acc_s) = refs[pps:]
    st = pl.program_id(1)
    qbd = qbd_ref[0]
    ncol = qbd.shape[1]

    @pl.when(st == 0)
    def _():
        m_s[...] = jnp.full(m_s.shape, NEG, F32)
        l_s[...] = jnp.zeros_like(l_s)
        acc_s[...] = jnp.zeros_like(acc_s)
        pn = _softmax_cols(_dot(kcvc_ref[0, 0].astype(BF16), qbd) + bc_ref[...])
        oc_s[...] = _dot_tn(kcvc_ref[0, 1].astype(BF16), pn.astype(BF16))
        imp = _dot_exact_lhs(_dot_exact_rhs(ov_ref[...], pn), gg_ref[...])
        nrow = imp.shape[0]
        rblk = lax.broadcasted_iota(jnp.int32, (nrow, ncol), 0)
        qb = nsb - 1
        forced = (rblk == 0) | (rblk > qb - N_LOCAL)
        score = jnp.where(rblk > qb, -FORCED, jnp.where(forced, FORCED, imp))
        score = jnp.where(rblk < nsb, score, ABSENT)
        madd_s[...] = (_topk_mask(score, nrow) - 1.0) * (-NEG)
        wk = jnp.concatenate([win_ref[0, :, 0:KV_DIM], new_ref[0, :, 4 * KV_DIM:5 * KV_DIM]], axis=0).astype(BF16)
        wv = jnp.concatenate([win_ref[0, :, KV_DIM:2 * KV_DIM], new_ref[0, :, 5 * KV_DIM:6 * KV_DIM]],
                             axis=0).astype(BF16)
        pw = _softmax_cols(_dot(wk, qbd) + bw_ref[...])
        ow_s[...] = _dot_tn(wv, pw.astype(BF16))

    def keys_block(kv, bias, r0, nblk):
        rows_per = kv.shape[0] // nblk
        s = _dot(kv[:, 0:KV_DIM].astype(BF16), qbd) + bias
        s = s + jnp.concatenate([jnp.broadcast_to(madd_s[pl.ds(r0 + j, 1), :], (rows_per, ncol))
                                 for j in range(nblk)], axis=0)
        _online_update(s, kv[:, KV_DIM:2 * KV_DIM].astype(BF16), m_s, l_s, acc_s, 0, True)

    bpp = PAGE_SIZE // L_SEL
    for j, pg in enumerate(pages):
        last = (nsteps - 1) * pps + j == nsteps * pps - 1
        if last:
            @pl.when(st == nsteps - 1)
            def _(pg=pg, j=j):
                keys_block(pg[0], blast_ref[...], (st * pps + j) * bpp, bpp)

            @pl.when(st < nsteps - 1)
            def _(pg=pg, j=j):
                keys_block(pg[0], crow_ref[...], (st * pps + j) * bpp, bpp)
        else:
            keys_block(pg[0], crow_ref[...], (st * pps + j) * bpp, bpp)

    @pl.when(st == nsteps - 1)
    def _():
        keys_block(new_ref[0, :, 2 * KV_DIM:4 * KV_DIM], bnew_ref[...], nsb - 1, 1)
        gate = jax.nn.sigmoid(gate_ref[0])
        o_ref[0] = gate[0:1] * oc_s[...] + gate[1:2] * (acc_s[0] * (1.0 / l_s[0])) + gate[2:3] * ow_s[...]


def _dot_exact_lhs(x, b_bf16):
    hi, mid, lo = _split3(x)
    return _dot(hi, b_bf16) + _dot(mid, b_bf16) + _dot(lo, b_bf16)


def _nsa_sample(cache, page_table, qbd, gates, kcvc, win, new_rows, bc, bw, blast, bnew, crow, ov, gg, pps=16):
    db, npg = page_table.shape
    pps = min(pps, npg)
    nsteps = npg // pps
    nsb = npg * (PAGE_SIZE // L_SEL) + 1
    ncol = qbd.shape[2]
    nrow = ov.shape[0]

    def page_spec(j):
        return pl.BlockSpec((1, PAGE_SIZE, 2 * KV_DIM), lambda bi, st, pt: (pt[bi, st * pps + j], 0, 1))

    const = lambda a: pl.BlockSpec(a.shape, lambda bi, st, pt: (0,) * a.ndim)
    seq = lambda a: pl.BlockSpec((1,) + a.shape[1:], lambda bi, st, pt: (bi,) + (0,) * (a.ndim - 1))
    grid_spec = pltpu.PrefetchScalarGridSpec(
        num_scalar_prefetch=1, grid=(db, nsteps),
        in_specs=[page_spec(j) for j in range(pps)] + [seq(qbd), seq(gates), seq(kcvc), seq(win), seq(new_rows)]
        + [const(a) for a in (bc, bw, blast, bnew, crow, ov, gg)],
        out_specs=pl.BlockSpec((1, KV_DIM, ncol), lambda bi, st, pt: (bi, 0, 0)),
        scratch_shapes=[pltpu.VMEM((nrow, ncol), F32), pltpu.VMEM((KV_DIM, ncol), F32), pltpu.VMEM((KV_DIM, ncol), F32),
                        pltpu.VMEM((1, 1, ncol), F32), pltpu.VMEM((1, 1, ncol), F32),
                        pltpu.VMEM((1, KV_DIM, ncol), F32)])
    return pl.pallas_call(
        functools.partial(_nsa_sample_kernel, pps=pps, nsteps=nsteps, nsb=nsb), grid_spec=grid_spec,
        out_shape=jax.ShapeDtypeStruct((db, KV_DIM, ncol), F32),
        compiler_params=_cparams(("parallel", "arbitrary")), name="nsa_sample",
    )(page_table, *([cache] * pps), qbd, gates, kcvc, win, new_rows, bc, bw, blast, bnew, crow, ov, gg)


def _merge_kernel(x_ref, ylru_ref, ynsa_ref, mg_ref, wl_ref, wn_ref, wo_ref, gf_ref, wr_ref, br_ref,
                  x2_ref, hn_ref, wt_ref):
    gate = jax.nn.sigmoid(mg_ref[...])
    mixed = (gate[:, :D_MODEL] * _dot(ylru_ref[...].astype(BF16), wl_ref[...])
             + gate[:, D_MODEL:] * _dot(ynsa_ref[...].astype(BF16), wn_ref[...]))
    x2 = x_ref[...] + _dot(mixed.astype(BF16), wo_ref[...])
    x2_ref[...] = x2
    hn = x2 * lax.rsqrt(jnp.mean(x2 * x2, axis=-1, keepdims=True) + EPS) * gf_ref[...]
    hn_ref[...] = hn.astype(BF16)
    logits = jnp.dot(hn, wr_ref[...], precision=lax.Precision.HIGHEST, preferred_element_type=F32) + br_ref[...]
    lane = lax.broadcasted_iota(jnp.int32, logits.shape, 1)
    lanef = lane.astype(F32)
    big = float(LANES)
    gl = jnp.where(lane < N_GROUPS, logits, NEG)
    gmax = jnp.max(gl, axis=-1, keepdims=True)
    grp = jnp.min(jnp.where(gl == gmax, lanef, big), axis=-1, keepdims=True)
    p_grp = 1.0 / jnp.sum(jnp.where(lane < N_GROUPS, jnp.exp(gl - gmax), 0.0), axis=-1, keepdims=True)
    lo = N_GROUPS + grp * EXPERTS_PER_GROUP
    el = jnp.where((lanef >= lo) & (lanef < lo + EXPERTS_PER_GROUP), logits, NEG)
    v1 = jnp.max(el, axis=-1, keepdims=True)
    i1 = jnp.min(jnp.where(el == v1, lanef, big), axis=-1, keepdims=True)
    el2 = jnp.where(lanef == i1, NEG, el)
    v2 = jnp.max(el2, axis=-1, keepdims=True)
    i2 = jnp.min(jnp.where(el2 == v2, lanef, big), axis=-1, keepdims=True)
    e2 = jnp.exp(v2 - v1)
    den = 1.0 / (1.0 + e2)
    wt_ref[...] = jnp.where(lanef == i1, den * p_grp, jnp.where(lanef == i2, e2 * den * p_grp, 0.0))


def _merge(x2d, ylru, ynsa, mg, wl, wn, wo, gf, wr, br, tm=256):
    t = x2d.shape[0]
    tm = min(tm, t)
    tile = lambda a: pl.BlockSpec((tm, a.shape[1]), lambda i: (i, 0))
    const = lambda a: pl.BlockSpec(a.shape, lambda i: (0, 0), pipeline_mode=pl.Buffered(1))
    return pl.pallas_call(
        _merge_kernel, grid=(t // tm,),
        in_specs=[tile(x2d), tile(ylru), tile(ynsa), tile(mg)] + [const(a) for a in (wl, wn, wo, gf, wr, br)],
        out_specs=[pl.BlockSpec((tm, D_MODEL), lambda i: (i, 0)), pl.BlockSpec((tm, D_MODEL), lambda i: (i, 0)),
                   pl.BlockSpec((tm, LANES), lambda i: (i, 0))],
        out_shape=[jax.ShapeDtypeStruct((t, D_MODEL), F32), jax.ShapeDtypeStruct((t, D_MODEL), BF16),
                   jax.ShapeDtypeStruct((t, LANES), F32)],
        compiler_params=_cparams(("parallel",)), name="merge")(x2d, ylru, ynsa, mg, wl, wn, wo, gf, wr, br)


def _moe_kernel(hn_ref, wt_ref, x2_ref, wg_ref, wu_ref, wd_ref, gfin_ref, y_ref, acc):
    c = pl.program_id(1)

    @pl.when(c == 0)
    def _():
        acc[...] = jnp.zeros_like(acc)

    h = hn_ref[...]
    wt = wt_ref[...]
    lane = lax.broadcasted_iota(jnp.int32, wt.shape, 1)
    total = acc[...]
    for e in range(EXPERTS_PER_GROUP):
        act = jax.nn.silu(_dot(h, wg_ref[e])) * _dot(h, wu_ref[e])
        w_e = jnp.sum(jnp.where(lane == N_GROUPS + c * EXPERTS_PER_GROUP + e, wt, 0.0), axis=-1, keepdims=True)
        act = jnp.where(w_e != 0.0, act * w_e, 0.0)
        total = total + _dot(act.astype(BF16), wd_ref[e])
    acc[...] = total

    @pl.when(c == N_GROUPS - 1)
    def _():
        x = x2_ref[...] + total
        y_ref[...] = x * lax.rsqrt(jnp.mean(x * x, axis=-1, keepdims=True) + EPS) * gfin_ref[...]


def _moe(hn, wt, x2, wg, wu, wd, gfin, tm=512):
    t = hn.shape[0]
    tm = min(tm, t)
    return pl.pallas_call(
        _moe_kernel, grid=(t // tm, N_GROUPS),
        in_specs=[pl.BlockSpec((tm, D_MODEL), lambda i, c: (i, 0)), pl.BlockSpec((tm, LANES), lambda i, c: (i, 0)),
                  pl.BlockSpec((tm, D_MODEL), lambda i, c: (i, 0)),
                  pl.BlockSpec((EXPERTS_PER_GROUP, D_MODEL, D_EXPERT), lambda i, c: (c, 0, 0)),
                  pl.BlockSpec((EXPERTS_PER_GROUP, D_MODEL, D_EXPERT), lambda i, c: (c, 0, 0)),
                  pl.BlockSpec((EXPERTS_PER_GROUP, D_EXPERT, D_MODEL), lambda i, c: (c, 0, 0)),
                  pl.BlockSpec((1, D_MODEL), lambda i, c: (0, 0))],
        out_specs=pl.BlockSpec((tm, D_MODEL), lambda i, c: (i, 0)),
        out_shape=jax.ShapeDtypeStruct((t, D_MODEL), F32),
        scratch_shapes=[pltpu.VMEM((tm, D_MODEL), F32)],
        compiler_params=_cparams(("parallel", "arbitrary")), name="moe")(hn, wt, x2, wg, wu, wd, gfin)


def _bucket_table():
    n = np.arange(MAX_DISTANCE + 1)
    exact = N_BUCKETS // 2
    nf = np.maximum(n, exact).astype(np.float64)
    large = exact + (np.log(nf / exact) / math.log(MAX_DISTANCE / exact) * (N_BUCKETS - exact)).astype(np.int32)
    return np.where(n < exact, n, np.minimum(large, N_BUCKETS - 1))


def _prompt_tile(rel_b, rel, mask):
    r = rel.shape[0]
    t = rel_b[np.clip(rel, 0, MAX_DISTANCE)]
    t = jnp.where(jnp.asarray(mask)[..., None], t, NEG)
    return t.reshape(r, QT, KV_HEADS, GROUP).transpose(2, 0, 3, 1).reshape(KV_HEADS, r, GQ)


def _sample_tile(rel_b, rel, mask):
    r, ds = rel.shape
    t = rel_b[np.clip(rel, 0, MAX_DISTANCE)]
    t = jnp.where(jnp.asarray(mask)[..., None], t, NEG)
    return t.transpose(0, 2, 1).reshape(r, N_HEADS * ds)


def _block_diag(w):
    eye = jnp.eye(LRU_HEADS, dtype=w.dtype)
    return jnp.einsum('hij,hk->hikj', w, eye).reshape(D_LRU, D_LRU)


def kernel(x_prompt, x_sample, cache_kv, state_kv_win, state_conv, state_h, page_table, g_mix, w_in, conv_w, conv_b,
           w_gate_a, b_gate_a, w_gate_x, b_gate_x, lru_lambda, cmp_w1, cmp_b1, cmp_w2, w_lru_out, w_nsa_out, w_out,
           g_ffn, w_router_group, b_router_group, w_router_expert, b_router_expert, w_exp_gate, w_exp_up,
           w_exp_down, rel_bias, g_final):
    assert w_in.shape[0] == 1, "single layer"
    b, s, _ = x_prompt.shape
    db, ds, _ = x_sample.shape
    npg = page_table.shape[1]
    past = npg * PAGE_SIZE
    wb = state_kv_win.shape[2]
    assert s % 256 == 0 and s // L_SEL >= N_SEL and CONV_W - 1 <= ds <= STRIDE and wb == WINDOW and past >= WINDOW

    w = w_in[0].astype(BF16)
    o = 0
    ws = []
    for width in (D_LRU, D_LRU, Q_DIM, 4 * KV_DIM, 2 * KV_DIM, 3 * N_HEADS, 2 * D_MODEL):
        ws.append(w[:, o:o + width])
        o += width
    ws[5] = jnp.pad(ws[5], ((0, 0), (0, LANES - 3 * N_HEADS)))
    g_mix2 = g_mix[0][None]
    wg = jnp.concatenate([_block_diag(w_gate_x[0]), _block_diag(w_gate_a[0])], axis=1).astype(BF16)
    lru_args = (conv_w[0], conv_b[0][None], wg, b_gate_x[0][None], b_gate_a[0][None], lru_lambda[0][None])
    w1 = cmp_w1[0].reshape(2, 2, STRIDE * HEAD_DIM, CMP_HIDDEN)
    w1f = jnp.concatenate([w1[:, 0], w1[:, 1]], axis=-1).astype(BF16)
    b1 = cmp_b1[0][:, None, :]
    w2 = cmp_w2[0].astype(BF16)
    wl, wn, wo = w_lru_out[0].astype(BF16), w_nsa_out[0].astype(BF16), w_out[0].astype(BF16)
    n_r = N_GROUPS + N_EXPERTS
    wr = jnp.pad(jnp.concatenate([w_router_group[0], w_router_expert[0]], axis=1), ((0, 0), (0, LANES - n_r)))
    br = jnp.pad(jnp.concatenate([b_router_group[0], b_router_expert[0]]), (0, LANES - n_r))[None]
    weg, weu, wed = w_exp_gate[0].astype(BF16), w_exp_up[0].astype(BF16), w_exp_down[0].astype(BF16)
    gf, gfin = g_ffn[0][None], g_final[None]
    rel_b = rel_bias.astype(F32)[_bucket_table()]

    def mix_ffn(x2d, ylru, ynsa, mg):
        x2, hn, wt = _merge(x2d, ylru, ynsa, mg, wl, wn, wo, gf, wr, br)
        return _moe(hn, wt, x2, weg, weu, wed, gfin)

    xp2 = x_prompt.reshape(b * s, D_MODEL)
    lx, lg, q, kv, kvw, ng, mg = _proj(xp2, g_mix2, ws)
    y_lru, conv_p, h_p = _rglru_prompt(lx.reshape(b, s, D_LRU), lg.reshape(b, s, D_LRU), *lru_args)
    kv3 = kv.reshape(b, s, 4 * KV_DIM)
    kvw3 = kvw.reshape(b, s, 2 * KV_DIM)
    kcvc = _compress_prompt(kv3, w1f, b1, w2)
    nt = s // QT
    ncp = s // STRIDE
    nsb = s // L_SEL

    def key_tiles(x):
        return x.astype(BF16).reshape(b, nt, QT, KV_DIM)

    def val_tiles(x):
        return x.astype(BF16).reshape(b, nt, QT, KV_DIM).swapaxes(2, 3)

    qi = np.arange(QT)[None, :]
    lrow = np.arange(ncp)[:, None]
    rel_c = qi - STRIDE * lrow + STRIDE * ncp - (STRIDE * 8 + L_CMP - 1)
    tc = _prompt_tile(rel_b, rel_c, rel_c >= 0)
    kj = np.arange(QT)[:, None]
    tiles = jnp.stack([_prompt_tile(rel_b, qi - kj, qi - kj >= 0),
                       _prompt_tile(rel_b, QT + qi - kj, np.ones((QT, QT), bool)),
                       _prompt_tile(rel_b, WINDOW + qi - kj, qi - kj < 0)])
    crow = _prompt_tile(rel_b, np.full((1, QT), MAX_DISTANCE), np.ones((1, QT), bool))
    rr = np.arange(nsb)[:, None]
    ll = np.arange(ncp)[None, :]
    ov = jnp.asarray((ll >= 4 * rr - 1) & (ll <= 4 * rr + 3), BF16)
    qT = q.reshape(b, s, Q_DIM).swapaxes(1, 2)
    gT = ng[:, :3 * N_HEADS].reshape(b, s, 3 * N_HEADS).swapaxes(1, 2)
    y_nsaT = _nsa_prompt(qT, gT, kcvc, key_tiles(kv3[..., 2 * KV_DIM:3 * KV_DIM]),
                         val_tiles(kv3[..., 3 * KV_DIM:]), key_tiles(kvw3[..., :KV_DIM]),
                         val_tiles(kvw3[..., KV_DIM:]), tc, tiles, crow, ov)
    y_nsa = y_nsaT.swapaxes(1, 2).reshape(b * s, Q_DIM)
    y_prompt = mix_ffn(xp2, y_lru.reshape(b * s, D_LRU), y_nsa, mg).reshape(b, s, D_MODEL)
    kv_rows_prompt = kv3.reshape(1, b, s, 4, KV_HEADS, HEAD_DIM)
    win_prompt = kvw3[:, s - min(WINDOW, s):].reshape(1, b, min(WINDOW, s), 2, KV_HEADS, HEAD_DIM)

    xs2 = x_sample.reshape(db * ds, D_MODEL)
    lx, lg, q, kv, kvw, ng, mg = _proj(xs2, g_mix2, ws)
    tmaj = lambda a: a.reshape(db, ds, D_LRU).swapaxes(0, 1)
    y_lru_t, conv_t, h_s = _rglru_sample(tmaj(lx), tmaj(lg), state_conv[0].swapaxes(0, 1), state_h[0], *lru_args)
    y_lru = y_lru_t.swapaxes(0, 1).reshape(db * ds, D_LRU)
    kv_s3 = kv.reshape(db, ds, 4 * KV_DIM)
    kvw_s3 = kvw.reshape(db, ds, 2 * KV_DIM)
    cache2 = cache_kv[0].reshape(cache_kv.shape[1], PAGE_SIZE, 4 * KV_DIM)
    new_c = jnp.pad(kv_s3[..., :2 * KV_DIM], ((0, 0), (0, PAGE_SIZE - ds), (0, 0)))
    kcvc_s = _compress_sample(cache2, page_table, new_c, w1f, b1, w2)

    nsb_s = past // L_SEL + 1
    ntok = past // STRIDE
    jq = np.arange(ds)[None, :]
    nrow = np.arange(ntok)[:, None]
    rel = past + jq - STRIDE * nrow - (L_CMP - 1)
    bc = _sample_tile(rel_b, rel, rel >= 0)
    r8 = np.arange(8)[:, None]
    rw = np.arange(wb)[:, None]
    rel_w = np.concatenate([wb + jq - rw, jq - r8], axis=0)
    mask_w = np.concatenate([(wb + jq - rw < WINDOW) & (past - wb + rw >= 0), (jq - r8 >= 0) & (r8 < ds)], axis=0)
    bw = _sample_tile(rel_b, rel_w, mask_w)
    rp = np.arange(PAGE_SIZE)[:, None]
    blast = _sample_tile(rel_b, PAGE_SIZE + jq - rp, np.ones((PAGE_SIZE, ds), bool))
    bnew = _sample_tile(rel_b, jq - r8, (jq - r8 >= 0) & (r8 < ds))
    crow_s = _sample_tile(rel_b, np.full((1, ds), MAX_DISTANCE), np.ones((1, ds), bool))
    nrow_pad = -(-nsb_s // 8) * 8
    rr = np.arange(nrow_pad)[:, None]
    ll = np.arange(ntok)[None, :]
    ov_s = jnp.asarray((ll >= 4 * rr - 1) & (ll <= 4 * rr + 3) & (rr < nsb_s), BF16)
    ncol = N_HEADS * ds
    col = np.arange(ncol)
    kvh_c, j_c = col // (GROUP * ds), col % ds
    gg = jnp.asarray((kvh_c[:, None] == kvh_c[None, :]) & (j_c[:, None] == j_c[None, :]), BF16)
    q5 = (q.reshape(db, ds, KV_HEADS, GROUP, HEAD_DIM) * (HEAD_DIM ** -0.5)).transpose(0, 2, 4, 3, 1)
    qbd = jnp.einsum('bkdgj,kc->bkdcgj', q5, jnp.eye(KV_HEADS, dtype=F32)).reshape(db, KV_DIM, ncol).astype(BF16)
    gates_s = ng[:, :3 * N_HEADS].reshape(db, ds, N_HEADS, 3).transpose(0, 3, 2, 1).reshape(db, 3, ncol)
    win2 = state_kv_win[0].reshape(db, wb, 2 * KV_DIM)
    new_all = jnp.pad(jnp.concatenate([kv_s3, kvw_s3], axis=-1), ((0, 0), (0, 8 - ds), (0, 0)))
    o_bd = _nsa_sample(cache2, page_table, qbd, gates_s, kcvc_s, win2, new_all, bc, bw, blast, bnew, crow_s,
                       ov_s, gg)
    o6 = o_bd.reshape(db, KV_HEADS, HEAD_DIM, KV_HEADS, GROUP, ds)
    y_nsa = jnp.einsum('bkdkgj->bjkgd', o6).reshape(db * ds, Q_DIM)
    y_sample = mix_ffn(xs2, y_lru, y_nsa, mg).reshape(db, ds, D_MODEL)
    kv_rows_sample = kv_s3.reshape(1, db, ds, 4, KV_HEADS, HEAD_DIM)
    win_all = jnp.concatenate([win2, kvw_s3], axis=1)
    win_sample = win_all[:, win_all.shape[1] - WINDOW:].reshape(1, db, WINDOW, 2, KV_HEADS, HEAD_DIM)
    conv_s = conv_t.swapaxes(0, 1)

    return (y_prompt, y_sample, kv_rows_prompt, win_prompt, conv_p[None], h_p.reshape(1, b, D_LRU),
            kv_rows_sample, win_sample, conv_s[None], h_s[None])
```

```python
import functools
import math

import numpy as np
import jax
import jax.numpy as jnp
from jax import lax
from jax.experimental import pallas as pl
from jax.experimental.pallas import tpu as pltpu

F32 = jnp.float32
BF16 = jnp.bfloat16

D_MODEL = 1024
D_LRU = 1280
LRU_HEADS = 16
LRU_BLOCK = D_LRU // LRU_HEADS
CONV_W = 4
LRU_C = 8.0
N_HEADS = 16
HEAD_DIM = 64
KV_HEADS = 4
GROUP = N_HEADS // KV_HEADS
L_CMP = 32
STRIDE = 16
CMP_HIDDEN = 128
L_SEL = 64
N_SEL = 16
N_LOCAL = 2
WINDOW = 512
PAGE_SIZE = 128
N_BUCKETS = 32
MAX_DISTANCE = 128
N_GROUPS = 4
EXPERTS_PER_GROUP = 8
N_EXPERTS = N_GROUPS * EXPERTS_PER_GROUP
D_EXPERT = 256
EPS = 1e-6
NEG = -1e30
FORCED = 1e9
Q_DIM = N_HEADS * HEAD_DIM
KV_DIM = KV_HEADS * HEAD_DIM
QT = 128
GQ = GROUP * QT
LANES = 128
VMEM_LIMIT = 56 * 1024 * 1024
KNOCKED = -3e38
ABSENT = -2e38
LOG2E = math.log2(math.e)
QK_SCALE = HEAD_DIM ** -0.5 * LOG2E


def _cparams(sem):
    return pltpu.CompilerParams(dimension_semantics=sem, vmem_limit_bytes=VMEM_LIMIT)


def _dot(a, b):
    return jnp.dot(a, b, preferred_element_type=F32)


def _dot_tn(a, b):
    return lax.dot_general(a, b, (((0,), (0,)), ((), ())), preferred_element_type=F32)


def _dot_nt(a, b):
    return lax.dot_general(a, b, (((1,), (1,)), ((), ())), preferred_element_type=F32)


def _split3(x):
    hi = x.astype(BF16)
    r1 = x - hi.astype(F32)
    mid = r1.astype(BF16)
    lo = (r1 - mid.astype(F32)).astype(BF16)
    return hi, mid, lo


def _dot_exact_rhs(a_bf16, x):
    hi, mid, lo = _split3(x)
    return _dot(a_bf16, hi) + _dot(a_bf16, mid) + _dot(a_bf16, lo)


def _dot_exact_lhs(x, b_bf16):
    hi, mid, lo = _split3(x)
    return _dot(hi, b_bf16) + _dot(mid, b_bf16) + _dot(lo, b_bf16)


def _proj_kernel(x_ref, g_ref, *refs):
    n = len(refs) // 2
    x = x_ref[...]
    xn = x * lax.rsqrt(jnp.mean(x * x, axis=-1, keepdims=True) + EPS) * g_ref[...]
    xb = xn.astype(BF16)
    for w_ref, o_ref in zip(refs[:n], refs[n:]):
        o_ref[...] = _dot(xb, w_ref[...])


def _proj(x2d, g, ws, tm=256):
    t = x2d.shape[0]
    tm = min(tm, t)
    in_specs = [pl.BlockSpec((tm, D_MODEL), lambda i: (i, 0)), pl.BlockSpec((1, D_MODEL), lambda i: (0, 0))]
    in_specs += [pl.BlockSpec(w.shape, lambda i: (0, 0), pipeline_mode=pl.Buffered(1)) for w in ws]
    out_specs = [pl.BlockSpec((tm, w.shape[1]), lambda i: (i, 0)) for w in ws]
    out_shape = [jax.ShapeDtypeStruct((t, w.shape[1]), F32) for w in ws]
    return pl.pallas_call(_proj_kernel, grid=(t // tm,), in_specs=in_specs, out_specs=out_specs,
                          out_shape=out_shape, compiler_params=_cparams(("parallel",)), name="proj")(x2d, g, *ws)


def _softplus(x):
    return jnp.maximum(x, 0.0) + jnp.log1p(jnp.exp(-jnp.abs(x)))


def _lru_gates(xc, wg_ref, bgx_ref, bga_ref, lam_ref):
    gates = _dot(xc.astype(BF16), wg_ref[...])
    gx = jax.nn.sigmoid(gates[:, :D_LRU] + bgx_ref[...])
    ga = jax.nn.sigmoid(gates[:, D_LRU:] + bga_ref[...])
    log_a = -LRU_C * ga * _softplus(-lam_ref[...])
    a = jnp.exp(log_a)
    th = jnp.tanh(log_a)
    u = jnp.sqrt(-2.0 * th / (1.0 - th)) * (gx * xc)
    return a, u


def _rglru_prompt_kernel(lx_ref, lg_ref, cw_ref, cb_ref, wg_ref, bgx_ref, bga_ref, lam_ref,
                         y_ref, conv_ref, h_ref, xext, hc, *, tc):
    t = pl.program_id(1)

    @pl.when(t == 0)
    def _():
        xext[0:8, :] = jnp.zeros((8, D_LRU), F32)
        hc[...] = jnp.zeros_like(hc)

    x = lx_ref[0]
    xext[8:8 + tc, :] = x
    cw = cw_ref[...]
    xc = cb_ref[...] + cw[3:4] * x
    for j in range(CONV_W - 1):
        xc = xc + cw[j:j + 1] * xext[5 + j:5 + j + tc, :]
    a, u = _lru_gates(xc, wg_ref, bgx_ref, bga_ref, lam_ref)
    row = lax.broadcasted_iota(jnp.int32, (tc, D_LRU), 0)
    s = 1
    while s < tc:
        a_sh = pltpu.roll(a, s, 0)
        u_sh = pltpu.roll(u, s, 0)
        m = row >= s
        u = jnp.where(m, a * u_sh + u, u)
        a = jnp.where(m, a * a_sh, a)
        s *= 2
    h = a * hc[0:1, :] + u
    hc[0:1, :] = h[tc - 1:tc, :]
    xext[0:8, :] = x[tc - 8:tc, :]
    y_ref[0] = h * jax.nn.gelu(lg_ref[0])
    conv_ref[0] = x[tc - (CONV_W - 1):tc, :]
    h_ref[0] = h[tc - 1:tc, :]


def _rglru_prompt(lx, lg, cw, cb, wg, bgx, bga, lam, tc=256):
    b, s, _ = lx.shape
    row = lambda shape: pl.BlockSpec(shape, lambda bi, ti: (0, 0))
    return pl.pallas_call(
        functools.partial(_rglru_prompt_kernel, tc=tc), grid=(b, s // tc),
        in_specs=[pl.BlockSpec((1, tc, D_LRU), lambda bi, ti: (bi, ti, 0)),
                  pl.BlockSpec((1, tc, D_LRU), lambda bi, ti: (bi, ti, 0)),
                  row((CONV_W, D_LRU)), row((1, D_LRU)), row((D_LRU, 2 * D_LRU)),
                  row((1, D_LRU)), row((1, D_LRU)), row((1, D_LRU))],
        out_specs=[pl.BlockSpec((1, tc, D_LRU), lambda bi, ti: (bi, ti, 0)),
                   pl.BlockSpec((1, CONV_W - 1, D_LRU), lambda bi, ti: (bi, 0, 0)),
                   pl.BlockSpec((1, 1, D_LRU), lambda bi, ti: (bi, 0, 0))],
        out_shape=[jax.ShapeDtypeStruct((b, s, D_LRU), F32),
                   jax.ShapeDtypeStruct((b, CONV_W - 1, D_LRU), F32),
                   jax.ShapeDtypeStruct((b, 1, D_LRU), F32)],
        scratch_shapes=[pltpu.VMEM((tc + 8, D_LRU), F32), pltpu.VMEM((8, D_LRU), F32)],
        compiler_params=_cparams(("parallel", "arbitrary")), name="rglru_prompt",
    )(lx, lg, cw, cb, wg, bgx, bga, lam)


def _rglru_sample_kernel(lx_ref, lg_ref, cbuf_ref, h0_ref, cw_ref, cb_ref, wg_ref, bgx_ref, bga_ref, lam_ref,
                         y_ref, conv_ref, h_ref, *, ds):
    cw = cw_ref[...]
    xp = [cbuf_ref[j] for j in range(CONV_W - 1)] + [lx_ref[j] for j in range(ds)]
    h = h0_ref[...]
    for t in range(ds):
        xc = cb_ref[...]
        for j in range(CONV_W):
            xc = xc + cw[j:j + 1] * xp[t + j]
        a, u = _lru_gates(xc, wg_ref, bgx_ref, bga_ref, lam_ref)
        h = a * h + u
        y_ref[t] = h * jax.nn.gelu(lg_ref[t])
    for j in range(CONV_W - 1):
        conv_ref[j] = xp[ds + j]
    h_ref[...] = h


def _rglru_sample(lx_t, lg_t, cbuf_t, h0, cw, cb, wg, bgx, bga, lam):
    ds, n, _ = lx_t.shape
    full = lambda a: pl.BlockSpec(a.shape, lambda i: (0,) * a.ndim)
    args = (lx_t, lg_t, cbuf_t, h0, cw, cb, wg, bgx, bga, lam)
    out_shape = [jax.ShapeDtypeStruct((ds, n, D_LRU), F32), jax.ShapeDtypeStruct((CONV_W - 1, n, D_LRU), F32),
                 jax.ShapeDtypeStruct((n, D_LRU), F32)]
    return pl.pallas_call(
        functools.partial(_rglru_sample_kernel, ds=ds), grid=(1,),
        in_specs=[full(a) for a in args], out_specs=[full(o) for o in out_shape], out_shape=out_shape,
        compiler_params=_cparams(("arbitrary",)), name="rglru_sample")(*args)


HEADS_PER_TILE = LANES // HEAD_DIM


def _chunk_features(load, k):
    c, o = k // HEADS_PER_TILE, (k % HEADS_PER_TILE) * HEAD_DIM
    return jnp.concatenate([load(s, c)[:, o:o + HEAD_DIM] for s in range(STRIDE)], axis=1).astype(BF16)


def _compress_prompt_kernel(xa_ref, xb_ref, w1_ref, b1_ref, w2_ref, o_ref, *, ncp):
    rows = lax.broadcasted_iota(jnp.int32, (ncp, HEAD_DIM), 0)
    xs = (xa_ref, xb_ref)
    for k in range(KV_HEADS):
        feat = _chunk_features(lambda s, c: xs[c][0, pl.ds(s, ncp, stride=STRIDE), :], k)
        p = _dot(feat, w1_ref[0])
        pre = b1_ref[0] + p[:, :CMP_HIDDEN] + pltpu.roll(p[:, CMP_HIDDEN:], ncp - 1, 0)
        phi = _dot(jax.nn.gelu(pre).astype(BF16), w2_ref[0])
        o_ref[0, 0, k, 0:ncp, :] = jnp.zeros((ncp, HEAD_DIM), F32)
        o_ref[0, 0, k, ncp:2 * ncp, :] = jnp.where(rows < ncp - 1, phi, 0.0)


def _compress_prompt(kv, w1f, b1, w2):
    b, s, _ = kv.shape
    ncp = s // STRIDE
    return pl.pallas_call(
        functools.partial(_compress_prompt_kernel, ncp=ncp), grid=(b, 2),
        in_specs=[pl.BlockSpec((1, s, LANES), lambda bi, sl: (bi, 0, 2 * sl)),
                  pl.BlockSpec((1, s, LANES), lambda bi, sl: (bi, 0, 2 * sl + 1)),
                  pl.BlockSpec((1, STRIDE * HEAD_DIM, 2 * CMP_HIDDEN), lambda bi, sl: (sl, 0, 0)),
                  pl.BlockSpec((1, 1, CMP_HIDDEN), lambda bi, sl: (sl, 0, 0)),
                  pl.BlockSpec((1, CMP_HIDDEN, HEAD_DIM), lambda bi, sl: (sl, 0, 0))],
        out_specs=pl.BlockSpec((1, 1, KV_HEADS, 2 * ncp, HEAD_DIM), lambda bi, sl: (sl, bi, 0, 0, 0)),
        out_shape=jax.ShapeDtypeStruct((2, b, KV_HEADS, 2 * ncp, HEAD_DIM), F32),
        compiler_params=_cparams(("parallel", "parallel")), name="compress_prompt")(kv, kv, w1f, b1, w2)


def _compress_sample_kernel(pt_ref, *refs, pps, nsteps):
    pages = refs[:pps]
    new_ref, w1_ref, b1_ref, w2_ref, o_ref, xs, pscr = refs[pps:]
    st = pl.program_id(1)
    cps = PAGE_SIZE // STRIDE
    m = pps * cps
    ntok = nsteps * m
    tiles_per_slot = KV_DIM // LANES

    def stage(ref, j, transposed):
        for c in range(2 * tiles_per_slot):
            if transposed:
                blk = ref[0, c * LANES:(c + 1) * LANES, :].T
            else:
                blk = ref[0, :, c * LANES:(c + 1) * LANES]
            xs[c, j * PAGE_SIZE:(j + 1) * PAGE_SIZE, :] = blk

    def project(nrows, row0):
        for sl in range(2):
            for k in range(KV_HEADS):
                feat = _chunk_features(
                    lambda s, c: xs[sl * tiles_per_slot + c, pl.ds(s, nrows, stride=STRIDE), :], k)
                pscr[sl, k, pl.ds(row0, nrows), :] = _dot(feat, w1_ref[sl])

    for j, pg in enumerate(pages):
        stage(pg, j, True)
    project(m, pl.multiple_of(st * m, m))

    @pl.when(st == nsteps - 1)
    def _():
        stage(new_ref, 0, False)
        project(cps, ntok)
        for sl in range(2):
            toks = []
            for k in range(KV_HEADS):
                pre = (b1_ref[sl] + pscr[sl, k, 0:ntok, 0:CMP_HIDDEN]
                       + pscr[sl, k, 1:ntok + 1, CMP_HIDDEN:2 * CMP_HIDDEN])
                toks.append(_dot(jax.nn.gelu(pre).astype(BF16), w2_ref[sl]))
            o_ref[0, sl] = jnp.concatenate(toks, axis=1)


def _compress_sample(cache, page_table, new_rows, w1f, b1, w2, pps=16):
    db, npg = page_table.shape
    pps = min(pps, npg)
    nsteps = npg // pps
    cps = PAGE_SIZE // STRIDE
    ntok = npg * cps

    def page_spec(j):
        return pl.BlockSpec((1, 2 * KV_DIM, PAGE_SIZE), lambda bi, st, pt: (pt[bi, st * pps + j], 0, 0))

    const = lambda shape: pl.BlockSpec(shape, lambda bi, st, pt: (0,) * len(shape))
    grid_spec = pltpu.PrefetchScalarGridSpec(
        num_scalar_prefetch=1, grid=(db, nsteps),
        in_specs=[page_spec(j) for j in range(pps)] + [
            pl.BlockSpec((1, PAGE_SIZE, 2 * KV_DIM), lambda bi, st, pt: (bi, 0, 0)),
            const((2, STRIDE * HEAD_DIM, 2 * CMP_HIDDEN)), const((2, 1, CMP_HIDDEN)),
            const((2, CMP_HIDDEN, HEAD_DIM))],
        out_specs=pl.BlockSpec((1, 2, ntok, KV_DIM), lambda bi, st, pt: (bi, 0, 0, 0)),
        scratch_shapes=[pltpu.VMEM((2 * KV_DIM // LANES, pps * PAGE_SIZE, LANES), F32),
                        pltpu.VMEM((2, KV_HEADS, ntok + cps, 2 * CMP_HIDDEN), F32)])
    return pl.pallas_call(
        functools.partial(_compress_sample_kernel, pps=pps, nsteps=nsteps), grid_spec=grid_spec,
        out_shape=jax.ShapeDtypeStruct((db, 2, ntok, KV_DIM), F32),
        compiler_params=_cparams(("parallel", "arbitrary")), name="compress_sample",
    )(page_table, *([cache] * pps), new_rows, w1f, b1, w2)


def _topk_mask(score, axis):
    n = score.shape[axis]
    idx = lax.broadcasted_iota(jnp.int32, score.shape, axis).astype(F32)

    def body(_, carry):
        sc, sel = carry
        mx = jnp.max(sc, axis=axis, keepdims=True)
        first = jnp.min(jnp.where(sc == mx, idx, float(n)), axis=axis, keepdims=True)
        hit = idx == first
        return jnp.where(hit, KNOCKED, sc), jnp.where(hit, 1.0, sel)

    return lax.fori_loop(0, N_SEL, body, (score, jnp.zeros_like(score)))[1]


def _softmax2(s, axis):
    m = jnp.max(s, axis=axis, keepdims=True)
    e = jnp.where(s > 0.5 * NEG, jnp.exp2(s - m), 0.0)
    l = jnp.sum(e, axis=axis, keepdims=True)
    return e * (1.0 / jnp.maximum(l, 1e-30))


def _nsa_prompt_kernel(qT_ref, gT_ref, kc_ref, vc_ref, ksel_ref, vselT_ref, kwin_ref, vwinT_ref,
                       tc_ref, tiles_ref, near_ref, crow_ref, ov_ref, y_ref,
                       qk_s, madd_s, oc_s, m_s, l_s, acc_s, s_scr, p_scr, *, ncp, nsb):
    i = pl.program_id(1)
    qT = qT_ref[0]
    for k in range(KV_HEADS):
        qk_s[k] = (jnp.concatenate([qT[(GROUP * k + g) * HEAD_DIM:(GROUP * k + g + 1) * HEAD_DIM, :]
                                    for g in range(GROUP)], axis=1) * QK_SCALE).astype(BF16)
    m_s[...] = jnp.full(m_s.shape, NEG, F32)
    l_s[...] = jnp.zeros_like(l_s)
    acc_s[...] = jnp.zeros_like(acc_s)

    start = pl.multiple_of(8 * i + 8, 8)
    tok_ok = lax.broadcasted_iota(jnp.int32, (ncp, 1), 0) >= ncp - 8 - 8 * i
    rblk = lax.broadcasted_iota(jnp.int32, (nsb, QT), 0)
    qhalf = jnp.where(lax.broadcasted_iota(jnp.int32, (nsb, QT), 1) >= L_SEL, 1, 0)
    r_qb = nsb - 2 + qhalf
    exists = rblk >= nsb - 2 - 2 * i
    forced = (rblk == nsb - 2 - 2 * i) | (rblk > r_qb - N_LOCAL)
    for k in range(KV_HEADS):
        kc = kc_ref[0, 0, k, pl.ds(start, ncp), :].astype(BF16)
        vc = vc_ref[0, 0, k, pl.ds(start, ncp), :].astype(BF16)
        s = jnp.where(tok_ok, _dot(kc, qk_s[k]) + tc_ref[k], NEG)
        pn = _softmax2(s, 0)
        oc_s[k] = _dot_tn(vc, pn.astype(BF16))
        psum = pn[:, 0:QT]
        for g in range(1, GROUP):
            psum = psum + pn[:, g * QT:(g + 1) * QT]
        imp = _dot_exact_rhs(ov_ref[...], psum)
        score = jnp.where(rblk > r_qb, -FORCED, jnp.where(forced, FORCED, imp))
        score = jnp.where(exists, score, ABSENT)
        madd_s[k] = (_topk_mask(score, 0) - 1.0) * (-NEG)

    bpt = QT // L_SEL

    def attend(t, n, k_ref, vT_ref, add_of, branch):
        nk = n * QT
        kt = k_ref[0, t] if n == 1 else jnp.concatenate([k_ref[0, t + j] for j in range(n)], axis=0)
        vt = vT_ref[0, t] if n == 1 else jnp.concatenate([vT_ref[0, t + j] for j in range(n)], axis=1)
        for k in range(KV_HEADS):
            s_scr[k, 0:nk, :] = _dot(kt[:, k * HEAD_DIM:(k + 1) * HEAD_DIM], qk_s[k])
        alphas = []
        for k in range(KV_HEADS):
            idx = branch * KV_HEADS + k
            s = s_scr[k, 0:nk, :] + add_of(k)
            m_old = m_s[idx]
            m_new = jnp.maximum(m_old, jnp.max(s, axis=0, keepdims=True))
            alpha = jnp.exp2(m_old - m_new)
            p = jnp.exp2(s - m_new)
            l_s[idx] = alpha * l_s[idx] + jnp.sum(p, axis=0, keepdims=True)
            m_s[idx] = m_new
            p_scr[k, 0:nk, :] = p.astype(BF16)
            alphas.append(alpha)
        for k in range(KV_HEADS):
            idx = branch * KV_HEADS + k
            acc_s[idx] = alphas[k] * acc_s[idx] + _dot(vt[k * HEAD_DIM:(k + 1) * HEAD_DIM, :], p_scr[k, 0:nk, :])

    def sel_rows(k, t, n):
        r0 = bpt * (t - i) + nsb - bpt
        return [madd_s[k, pl.ds(r0 + j, 1), :] for j in range(bpt * n)]

    def far_add(t, n, masked):
        def add_of(k):
            c = crow_ref[k]
            if not masked:
                return c
            return jnp.concatenate(
                [jnp.concatenate([jnp.broadcast_to(r + c[:, g * QT:(g + 1) * QT], (L_SEL, QT)) for g in range(GROUP)],
                                 axis=1) for r in sel_rows(k, t, n)], axis=0)
        return add_of

    def table_add(table_of, t, n, masked):
        def add_of(k):
            if not masked:
                return table_of(k)
            mt = jnp.concatenate([jnp.broadcast_to(r, (L_SEL, QT)) for r in sel_rows(k, t, n)], axis=0)
            return table_of(k) + jnp.concatenate([mt] * GROUP, axis=1)
        return add_of

    nfar = jnp.maximum(i - 1, 0)

    def far_body(tt, c):
        attend(2 * tt, 2, ksel_ref, vselT_ref, far_add(2 * tt, 2, True), 0)
        return c

    lax.fori_loop(0, nfar // 2, far_body, 0)

    @pl.when(nfar % 2 == 1)
    def _():
        attend(nfar - 1, 1, ksel_ref, vselT_ref, far_add(nfar - 1, 1, True), 0)

    @pl.when(i >= 4)
    def _():
        attend(i - 4, 1, kwin_ref, vwinT_ref, table_add(lambda k: tiles_ref[1, k], i - 4, 1, False), 1)

    @pl.when(i >= 3)
    def _():
        attend(i - 3, 2, kwin_ref, vwinT_ref, far_add(i - 3, 2, False), 1)

    @pl.when(i == 2)
    def _():
        attend(0, 1, kwin_ref, vwinT_ref, far_add(0, 1, False), 1)

    @pl.when(i >= 1)
    def _():
        attend(i - 1, 2, ksel_ref, vselT_ref, table_add(lambda k: near_ref[k], i - 1, 2, True), 0)
        attend(i - 1, 2, kwin_ref, vwinT_ref, table_add(lambda k: near_ref[k], i - 1, 2, False), 1)

    @pl.when(i == 0)
    def _():
        attend(0, 1, ksel_ref, vselT_ref, table_add(lambda k: tiles_ref[0, k], 0, 1, True), 0)
        attend(0, 1, kwin_ref, vwinT_ref, table_add(lambda k: tiles_ref[0, k], 0, 1, False), 1)

    gate = jax.nn.sigmoid(gT_ref[0])
    for k in range(KV_HEADS):
        o_s = acc_s[k] * (1.0 / l_s[k])
        o_w = acc_s[KV_HEADS + k] * (1.0 / l_s[KV_HEADS + k])
        o_c = oc_s[k]
        for g in range(GROUP):
            h = GROUP * k + g
            cols = slice(g * QT, (g + 1) * QT)
            y_ref[0, h * HEAD_DIM:(h + 1) * HEAD_DIM, :] = (
                gate[3 * h:3 * h + 1] * o_c[:, cols] + gate[3 * h + 1:3 * h + 2] * o_s[:, cols]
                + gate[3 * h + 2:3 * h + 3] * o_w[:, cols])


def _nsa_prompt(qT, gT, kcvc, ksel, vselT, kwin, vwinT, tc, tiles, near, crow, ov):
    b, _, s = qT.shape
    nq = s // QT
    ncp = s // STRIDE
    nsb = s // L_SEL
    seq4 = lambda a: pl.BlockSpec((1,) + a.shape[1:], lambda bi, qi: (bi, 0, 0, 0), pipeline_mode=pl.Buffered(1))
    const = lambda a: pl.BlockSpec(a.shape, lambda bi, qi: (0,) * a.ndim, pipeline_mode=pl.Buffered(1))
    cmp_spec = lambda sl: pl.BlockSpec((1, 1, KV_HEADS, 2 * ncp, HEAD_DIM), lambda bi, qi: (sl, bi, 0, 0, 0),
                                       pipeline_mode=pl.Buffered(1))
    return pl.pallas_call(
        functools.partial(_nsa_prompt_kernel, ncp=ncp, nsb=nsb), grid=(b, nq),
        in_specs=[pl.BlockSpec((1, Q_DIM, QT), lambda bi, qi: (bi, 0, qi)),
                  pl.BlockSpec((1, 3 * N_HEADS, QT), lambda bi, qi: (bi, 0, qi)),
                  cmp_spec(0), cmp_spec(1), seq4(ksel), seq4(vselT), seq4(kwin), seq4(vwinT),
                  const(tc), const(tiles), const(near), const(crow), const(ov)],
        out_specs=pl.BlockSpec((1, Q_DIM, QT), lambda bi, qi: (bi, 0, qi)),
        out_shape=jax.ShapeDtypeStruct((b, Q_DIM, s), F32),
        scratch_shapes=[pltpu.VMEM((KV_HEADS, HEAD_DIM, GQ), BF16), pltpu.VMEM((KV_HEADS, nsb, QT), F32),
                        pltpu.VMEM((KV_HEADS, HEAD_DIM, GQ), F32), pltpu.VMEM((2 * KV_HEADS, 1, GQ), F32),
                        pltpu.VMEM((2 * KV_HEADS, 1, GQ), F32), pltpu.VMEM((2 * KV_HEADS, HEAD_DIM, GQ), F32),
                        pltpu.VMEM((KV_HEADS, 2 * QT, GQ), F32), pltpu.VMEM((KV_HEADS, 2 * QT, GQ), BF16)],
        compiler_params=_cparams(("parallel", "arbitrary")), name="nsa_prompt",
    )(qT, gT, kcvc, kcvc, ksel, vselT, kwin, vwinT, tc, tiles, near, crow, ov)


def _nsa_sample_kernel(pt_ref, *refs, pps, nsteps, nsb):
    pages = refs[:pps]
    (qbd_ref, gate_ref, kcvc_ref, win_ref, new_ref, bc_ref, bw_ref, bstep_ref, bnew_ref, ovt_ref, gg_ref, exp_ref,
     o_ref, madd_s, oc_s, ow_s, m_s, l_s, acc_s) = refs[pps:]
    st = pl.program_id(1)
    qbd = qbd_ref[0]
    ncol = qbd.shape[0]
    wb = win_ref.shape[2]
    bps = pps * (PAGE_SIZE // L_SEL)
    nlane = ovt_ref.shape[1]

    def online(s, pv_of):
        m_old = m_s[...]
        m_new = jnp.maximum(m_old, jnp.max(s, axis=1, keepdims=True))
        alpha = jnp.exp2(m_old - m_new)
        p = jnp.exp2(s - m_new)
        l_s[...] = alpha * l_s[...] + jnp.sum(p, axis=1, keepdims=True)
        acc_s[...] = alpha * acc_s[...] + pv_of(p.astype(BF16))
        m_s[...] = m_new

    @pl.when(st == 0)
    def _():
        m_s[...] = jnp.full(m_s.shape, NEG, F32)
        l_s[...] = jnp.zeros_like(l_s)
        acc_s[...] = jnp.zeros_like(acc_s)
        pn = _softmax2(_dot_nt(qbd, kcvc_ref[0, 0].astype(BF16)) + bc_ref[...], 1)
        oc_s[...] = _dot(pn.astype(BF16), kcvc_ref[0, 1].astype(BF16))
        imp = _dot_exact_rhs(gg_ref[...], _dot_exact_lhs(pn, ovt_ref[...]))
        blk = lax.broadcasted_iota(jnp.int32, (ncol, nlane), 1)
        qb = nsb - 1
        forced = (blk == 0) | (blk > qb - N_LOCAL)
        score = jnp.where(blk > qb, -FORCED, jnp.where(forced, FORCED, imp))
        score = jnp.where(blk < nsb, score, ABSENT)
        madd = jnp.where(_topk_mask(score, 1) > 0.0, 0.0, NEG).astype(BF16)
        for tl in range(nlane // LANES):
            madd_s[tl] = madd[:, tl * LANES:(tl + 1) * LANES]
        new = new_ref[0]
        s_w = jnp.concatenate([_dot(qbd, win_ref[0, 0:KV_DIM, :].astype(BF16)),
                               _dot_nt(qbd, new[:, 2 * KV_DIM:3 * KV_DIM].astype(BF16))], axis=1) + bw_ref[...]
        pw = _softmax2(s_w, 1).astype(BF16)
        ow_s[...] = (_dot_nt(pw[:, 0:wb], win_ref[0, KV_DIM:2 * KV_DIM, :].astype(BF16))
                     + _dot(pw[:, wb:], new[:, 3 * KV_DIM:4 * KV_DIM].astype(BF16)))

    kt = jnp.concatenate([pg[0, 0:KV_DIM, :] for pg in pages], axis=1).astype(BF16)
    vt = jnp.concatenate([pg[0, KV_DIM:2 * KV_DIM, :] for pg in pages], axis=1).astype(BF16)
    b0 = st * bps
    mask = _dot(madd_s[b0 // LANES], exp_ref[(b0 % LANES) // bps])
    bias = jnp.where(st == nsteps - 1, bstep_ref[1], bstep_ref[0])
    online(_dot(qbd, kt) + mask + bias, lambda p: _dot_nt(p, vt))

    @pl.when(st == nsteps - 1)
    def _():
        new = new_ref[0]
        lb = nsb - 1
        mcol = madd_s[lb // LANES][:, lb % LANES:lb % LANES + 1].astype(F32)
        online(_dot_nt(qbd, new[:, 0:KV_DIM].astype(BF16)) + bnew_ref[...] + mcol,
               lambda p: _dot(p, new[:, KV_DIM:2 * KV_DIM].astype(BF16)))
        gate = jax.nn.sigmoid(gate_ref[0])
        o_ref[0] = (gate[:, 0:1] * oc_s[...] + gate[:, 1:2] * (acc_s[...] * (1.0 / l_s[...]))
                    + gate[:, 2:3] * ow_s[...])


def _nsa_sample(cache, page_table, qbd, gates, kcvc, win, new_rows, bc, bw, bstep, bnew, ovt, gg, expand, pps):
    db, npg = page_table.shape
    nsteps = npg // pps
    nsb = npg * (PAGE_SIZE // L_SEL) + 1
    ncol = qbd.shape[1]
    nlane = ovt.shape[1]

    def page_spec(j):
        return pl.BlockSpec((1, 2 * KV_DIM, PAGE_SIZE), lambda bi, st, pt: (pt[bi, st * pps + j], 1, 0))

    const = lambda a: pl.BlockSpec(a.shape, lambda bi, st, pt: (0,) * a.ndim)
    seq = lambda a: pl.BlockSpec((1,) + a.shape[1:], lambda bi, st, pt: (bi,) + (0,) * (a.ndim - 1))
    grid_spec = pltpu.PrefetchScalarGridSpec(
        num_scalar_prefetch=1, grid=(db, nsteps),
        in_specs=[page_spec(j) for j in range(pps)] + [seq(qbd), seq(gates), seq(kcvc), seq(win), seq(new_rows)]
        + [const(a) for a in (bc, bw, bstep, bnew, ovt, gg, expand)],
        out_specs=pl.BlockSpec((1, ncol, KV_DIM), lambda bi, st, pt: (bi, 0, 0)),
        scratch_shapes=[pltpu.VMEM((nlane // LANES, ncol, LANES), BF16), pltpu.VMEM((ncol, KV_DIM), F32),
                        pltpu.VMEM((ncol, KV_DIM), F32), pltpu.VMEM((ncol, 1), F32), pltpu.VMEM((ncol, 1), F32),
                        pltpu.VMEM((ncol, KV_DIM), F32)])
    return pl.pallas_call(
        functools.partial(_nsa_sample_kernel, pps=pps, nsteps=nsteps, nsb=nsb), grid_spec=grid_spec,
        out_shape=jax.ShapeDtypeStruct((db, ncol, KV_DIM), F32),
        compiler_params=_cparams(("parallel", "arbitrary")), name="nsa_sample",
    )(page_table, *([cache] * pps), qbd, gates, kcvc, win, new_rows, bc, bw, bstep, bnew, ovt, gg, expand)


def _merge_kernel(x_ref, ylru_ref, ynsa_ref, mg_ref, wl_ref, wn_ref, wo_ref, gf_ref, wr_ref, br_ref,
                  x2_ref, hn_ref, wt_ref):
    gate = jax.nn.sigmoid(mg_ref[...])
    mixed = (gate[:, :D_MODEL] * _dot(ylru_ref[...].astype(BF16), wl_ref[...])
             + gate[:, D_MODEL:] * _dot(ynsa_ref[...].astype(BF16), wn_ref[...]))
    x2 = x_ref[...] + _dot(mixed.astype(BF16), wo_ref[...])
    x2_ref[...] = x2
    hn = x2 * lax.rsqrt(jnp.mean(x2 * x2, axis=-1, keepdims=True) + EPS) * gf_ref[...]
    hn_ref[...] = hn.astype(BF16)
    logits = jnp.dot(hn, wr_ref[...], precision=lax.Precision.HIGHEST, preferred_element_type=F32) + br_ref[...]
    lane = lax.broadcasted_iota(jnp.int32, logits.shape, 1)
    lanef = lane.astype(F32)
    big = float(LANES)
    gl = jnp.where(lane < N_GROUPS, logits, NEG)
    gmax = jnp.max(gl, axis=-1, keepdims=True)
    grp = jnp.min(jnp.where(gl == gmax, lanef, big), axis=-1, keepdims=True)
    p_grp = 1.0 / jnp.sum(jnp.where(lane < N_GROUPS, jnp.exp(gl - gmax), 0.0), axis=-1, keepdims=True)
    lo = N_GROUPS + grp * EXPERTS_PER_GROUP
    el = jnp.where((lanef >= lo) & (lanef < lo + EXPERTS_PER_GROUP), logits, NEG)
    v1 = jnp.max(el, axis=-1, keepdims=True)
    i1 = jnp.min(jnp.where(el == v1, lanef, big), axis=-1, keepdims=True)
    el2 = jnp.where(lanef == i1, NEG, el)
    v2 = jnp.max(el2, axis=-1, keepdims=True)
    i2 = jnp.min(jnp.where(el2 == v2, lanef, big), axis=-1, keepdims=True)
    e2 = jnp.exp(v2 - v1)
    den = 1.0 / (1.0 + e2)
    wt_ref[...] = jnp.where(lanef == i1, den * p_grp, jnp.where(lanef == i2, e2 * den * p_grp, 0.0))


def _merge(x2d, ylru, ynsa, mg, wl, wn, wo, gf, wr, br, tm=256):
    t = x2d.shape[0]
    tm = min(tm, t)
    tile = lambda a: pl.BlockSpec((tm, a.shape[1]), lambda i: (i, 0))
    const = lambda a: pl.BlockSpec(a.shape, lambda i: (0, 0), pipeline_mode=pl.Buffered(1))
    return pl.pallas_call(
        _merge_kernel, grid=(t // tm,),
        in_specs=[tile(x2d), tile(ylru), tile(ynsa), tile(mg)] + [const(a) for a in (wl, wn, wo, gf, wr, br)],
        out_specs=[pl.BlockSpec((tm, D_MODEL), lambda i: (i, 0)), pl.BlockSpec((tm, D_MODEL), lambda i: (i, 0)),
                   pl.BlockSpec((tm, LANES), lambda i: (i, 0))],
        out_shape=[jax.ShapeDtypeStruct((t, D_MODEL), F32), jax.ShapeDtypeStruct((t, D_MODEL), BF16),
                   jax.ShapeDtypeStruct((t, LANES), F32)],
        compiler_params=_cparams(("parallel",)), name="merge")(x2d, ylru, ynsa, mg, wl, wn, wo, gf, wr, br)


def _moe_kernel(hn_ref, wt_ref, x2_ref, wg_ref, wu_ref, wd_ref, gfin_ref, y_ref, acc):
    c = pl.program_id(1)

    @pl.when(c == 0)
    def _():
        acc[...] = jnp.zeros_like(acc)

    h = hn_ref[...]
    wt = wt_ref[...]
    lane = lax.broadcasted_iota(jnp.int32, wt.shape, 1)
    total = acc[...]
    for e in range(EXPERTS_PER_GROUP):
        act = jax.nn.silu(_dot(h, wg_ref[e])) * _dot(h, wu_ref[e])
        w_e = jnp.sum(jnp.where(lane == N_GROUPS + c * EXPERTS_PER_GROUP + e, wt, 0.0), axis=-1, keepdims=True)
        act = jnp.where(w_e != 0.0, act * w_e, 0.0)
        total = total + _dot(act.astype(BF16), wd_ref[e])
    acc[...] = total

    @pl.when(c == N_GROUPS - 1)
    def _():
        x = x2_ref[...] + total
        y_ref[...] = x * lax.rsqrt(jnp.mean(x * x, axis=-1, keepdims=True) + EPS) * gfin_ref[...]


def _moe(hn, wt, x2, wg, wu, wd, gfin, tm=512):
    t = hn.shape[0]
    tm = min(tm, t)
    return pl.pallas_call(
        _moe_kernel, grid=(t // tm, N_GROUPS),
        in_specs=[pl.BlockSpec((tm, D_MODEL), lambda i, c: (i, 0)), pl.BlockSpec((tm, LANES), lambda i, c: (i, 0)),
                  pl.BlockSpec((tm, D_MODEL), lambda i, c: (i, 0)),
                  pl.BlockSpec((EXPERTS_PER_GROUP, D_MODEL, D_EXPERT), lambda i, c: (c, 0, 0)),
                  pl.BlockSpec((EXPERTS_PER_GROUP, D_MODEL, D_EXPERT), lambda i, c: (c, 0, 0)),
                  pl.BlockSpec((EXPERTS_PER_GROUP, D_EXPERT, D_MODEL), lambda i, c: (c, 0, 0)),
                  pl.BlockSpec((1, D_MODEL), lambda i, c: (0, 0))],
        out_specs=pl.BlockSpec((tm, D_MODEL), lambda i, c: (i, 0)),
        out_shape=jax.ShapeDtypeStruct((t, D_MODEL), F32),
        scratch_shapes=[pltpu.VMEM((tm, D_MODEL), F32)],
        compiler_params=_cparams(("parallel", "arbitrary")), name="moe")(hn, wt, x2, wg, wu, wd, gfin)


def _bucket_table():
    n = np.arange(MAX_DISTANCE + 1)
    exact = N_BUCKETS // 2
    nf = np.maximum(n, exact).astype(np.float64)
    large = exact + (np.log(nf / exact) / math.log(MAX_DISTANCE / exact) * (N_BUCKETS - exact)).astype(np.int32)
    return np.where(n < exact, n, np.minimum(large, N_BUCKETS - 1))


def _prompt_tile(rel_b, rel, mask):
    r = rel.shape[0]
    t = rel_b[np.clip(rel, 0, MAX_DISTANCE)] * LOG2E
    t = jnp.where(jnp.asarray(mask)[..., None], t, NEG)
    return t.reshape(r, QT, KV_HEADS, GROUP).transpose(2, 0, 3, 1).reshape(KV_HEADS, r, GQ)


def _sample_tile(rel_b, rel, mask):
    r, ds = rel.shape
    t = rel_b[np.clip(rel, 0, MAX_DISTANCE)] * LOG2E
    t = jnp.where(jnp.asarray(mask)[..., None], t, NEG)
    return t.transpose(2, 1, 0).reshape(N_HEADS * ds, r)


def _block_diag(w):
    eye = jnp.eye(LRU_HEADS, dtype=w.dtype)
    return jnp.einsum('hij,hk->hikj', w, eye).reshape(D_LRU, D_LRU)


def kernel(x_prompt, x_sample, cache_kv, state_kv_win, state_conv, state_h, page_table, g_mix, w_in, conv_w, conv_b,
           w_gate_a, b_gate_a, w_gate_x, b_gate_x, lru_lambda, cmp_w1, cmp_b1, cmp_w2, w_lru_out, w_nsa_out, w_out,
           g_ffn, w_router_group, b_router_group, w_router_expert, b_router_expert, w_exp_gate, w_exp_up,
           w_exp_down, rel_bias, g_final):
    assert w_in.shape[0] == 1, "single layer"
    b, s, _ = x_prompt.shape
    db, ds, _ = x_sample.shape
    npg = page_table.shape[1]
    past = npg * PAGE_SIZE
    wb = state_kv_win.shape[2]
    assert s % 256 == 0 and s // L_SEL >= N_SEL and CONV_W - 1 <= ds <= STRIDE and wb == WINDOW and past >= WINDOW

    w = w_in[0].astype(BF16)
    o = 0
    ws = []
    for width in (D_LRU, D_LRU, Q_DIM, 4 * KV_DIM, 2 * KV_DIM, 3 * N_HEADS, 2 * D_MODEL):
        ws.append(w[:, o:o + width])
        o += width
    ws[5] = jnp.pad(ws[5], ((0, 0), (0, LANES - 3 * N_HEADS)))
    g_mix2 = g_mix[0][None]
    wg = jnp.concatenate([_block_diag(w_gate_x[0]), _block_diag(w_gate_a[0])], axis=1).astype(BF16)
    lru_args = (conv_w[0], conv_b[0][None], wg, b_gate_x[0][None], b_gate_a[0][None], lru_lambda[0][None])
    w1 = cmp_w1[0].reshape(2, 2, STRIDE * HEAD_DIM, CMP_HIDDEN)
    w1f = jnp.concatenate([w1[:, 0], w1[:, 1]], axis=-1).astype(BF16)
    b1 = cmp_b1[0][:, None, :]
    w2 = cmp_w2[0].astype(BF16)
    wl, wn, wo = w_lru_out[0].astype(BF16), w_nsa_out[0].astype(BF16), w_out[0].astype(BF16)
    n_r = N_GROUPS + N_EXPERTS
    wr = jnp.pad(jnp.concatenate([w_router_group[0], w_router_expert[0]], axis=1), ((0, 0), (0, LANES - n_r)))
    br = jnp.pad(jnp.concatenate([b_router_group[0], b_router_expert[0]]), (0, LANES - n_r))[None]
    weg, weu, wed = w_exp_gate[0].astype(BF16), w_exp_up[0].astype(BF16), w_exp_down[0].astype(BF16)
    gf, gfin = g_ffn[0][None], g_final[None]
    rel_b = rel_bias.astype(F32)[_bucket_table()]

    def mix_ffn(x2d, ylru, ynsa, mg):
        x2, hn, wt = _merge(x2d, ylru, ynsa, mg, wl, wn, wo, gf, wr, br)
        return _moe(hn, wt, x2, weg, weu, wed, gfin)

    xp2 = x_prompt.reshape(b * s, D_MODEL)
    lx, lg, q, kv, kvw, ng, mg = _proj(xp2, g_mix2, ws)
    y_lru, conv_p, h_p = _rglru_prompt(lx.reshape(b, s, D_LRU), lg.reshape(b, s, D_LRU), *lru_args)
    kv3 = kv.reshape(b, s, 4 * KV_DIM)
    kvw3 = kvw.reshape(b, s, 2 * KV_DIM)
    kcvc = _compress_prompt(kv3, w1f, b1, w2)
    nt = s // QT
    ncp = s // STRIDE
    nsb = s // L_SEL

    def key_tiles(x):
        return x.astype(BF16).reshape(b, nt, QT, KV_DIM)

    def val_tiles(x):
        return x.astype(BF16).reshape(b, nt, QT, KV_DIM).swapaxes(2, 3)

    qi = np.arange(QT)[None, :]
    lrow = np.arange(ncp)[:, None]
    rel_c = qi - STRIDE * lrow + STRIDE * ncp - (STRIDE * 8 + L_CMP - 1)
    tc = _prompt_tile(rel_b, rel_c, rel_c >= 0)
    kj = np.arange(QT)[:, None]
    diag = _prompt_tile(rel_b, qi - kj, qi - kj >= 0)
    tiles = jnp.stack([diag, _prompt_tile(rel_b, WINDOW + qi - kj, qi - kj < 0)])
    near = jnp.concatenate([_prompt_tile(rel_b, QT + qi - kj, np.ones((QT, QT), bool)), diag], axis=1)
    crow = _prompt_tile(rel_b, np.full((1, QT), MAX_DISTANCE), np.ones((1, QT), bool))
    rr = np.arange(nsb)[:, None]
    ll = np.arange(ncp)[None, :]
    ov = jnp.asarray((ll >= 4 * rr - 1) & (ll <= 4 * rr + 3), BF16)
    qT = q.reshape(b, s, Q_DIM).swapaxes(1, 2)
    gT = ng[:, :3 * N_HEADS].reshape(b, s, 3 * N_HEADS).swapaxes(1, 2)
    y_nsaT = _nsa_prompt(qT, gT, kcvc, key_tiles(kv3[..., 2 * KV_DIM:3 * KV_DIM]),
                         val_tiles(kv3[..., 3 * KV_DIM:]), key_tiles(kvw3[..., :KV_DIM]),
                         val_tiles(kvw3[..., KV_DIM:]), tc, tiles, near, crow, ov)
    y_nsa = y_nsaT.swapaxes(1, 2).reshape(b * s, Q_DIM)
    y_prompt = mix_ffn(xp2, y_lru.reshape(b * s, D_LRU), y_nsa, mg).reshape(b, s, D_MODEL)
    kv_rows_prompt = kv3.reshape(1, b, s, 4, KV_HEADS, HEAD_DIM)
    win_prompt = kvw3[:, s - min(WINDOW, s):].reshape(1, b, min(WINDOW, s), 2, KV_HEADS, HEAD_DIM)

    xs2 = x_sample.reshape(db * ds, D_MODEL)
    lx, lg, q, kv, kvw, ng, mg = _proj(xs2, g_mix2, ws)
    tmaj = lambda a: a.reshape(db, ds, D_LRU).swapaxes(0, 1)
    y_lru_t, conv_t, h_s = _rglru_sample(tmaj(lx), tmaj(lg), state_conv[0].swapaxes(0, 1), state_h[0], *lru_args)
    y_lru = y_lru_t.swapaxes(0, 1).reshape(db * ds, D_LRU)
    kv_s3 = kv.reshape(db, ds, 4 * KV_DIM)
    kvw_s3 = kvw.reshape(db, ds, 2 * KV_DIM)
    cache_t = cache_kv[0].transpose(0, 2, 3, 4, 1).reshape(cache_kv.shape[1], 4 * KV_DIM, PAGE_SIZE)
    win_t = state_kv_win[0].transpose(0, 2, 3, 4, 1).reshape(db, 2 * KV_DIM, wb)
    new_c = jnp.pad(kv_s3[..., :2 * KV_DIM], ((0, 0), (0, PAGE_SIZE - ds), (0, 0)))
    pps = 16
    assert npg % pps == 0 and LANES % (pps * PAGE_SIZE // L_SEL) == 0
    kcvc_s = _compress_sample(cache_t, page_table, new_c, w1f, b1, w2, pps)

    nsb_s = past // L_SEL + 1
    ntok = past // STRIDE
    jq = np.arange(ds)[None, :]
    nrow = np.arange(ntok)[:, None]
    rel = past + jq - STRIDE * nrow - (L_CMP - 1)
    bc = _sample_tile(rel_b, rel, rel >= 0)
    rn = np.arange(PAGE_SIZE)[:, None]
    new_rel, new_ok = jq - rn, (jq - rn >= 0) & (rn < ds)
    rw = np.arange(wb)[:, None]
    bw = _sample_tile(rel_b, np.concatenate([wb + jq - rw, new_rel], axis=0),
                      np.concatenate([(wb + jq - rw < WINDOW) & (past - wb + rw >= 0), new_ok], axis=0))
    bnew = _sample_tile(rel_b, new_rel, new_ok)
    step_keys = pps * PAGE_SIZE
    far_rel = np.full((step_keys, ds), MAX_DISTANCE)
    last_rel = far_rel.copy()
    last_rel[step_keys - PAGE_SIZE:] = PAGE_SIZE + jq - rn
    all_ok = np.ones((step_keys, ds), bool)
    bstep = jnp.stack([_sample_tile(rel_b, far_rel, all_ok), _sample_tile(rel_b, last_rel, all_ok)])
    nlane = -(-nsb_s // LANES) * LANES
    bb = np.arange(nlane)[None, :]
    tt = np.arange(ntok)[:, None]
    ovt = jnp.asarray((tt >= 4 * bb - 1) & (tt <= 4 * bb + 3) & (bb < nsb_s), BF16)
    ncol = N_HEADS * ds
    col = np.arange(ncol)
    kvh_c, j_c = col // (GROUP * ds), col % ds
    gg = jnp.asarray((kvh_c[:, None] == kvh_c[None, :]) & (j_c[:, None] == j_c[None, :]), BF16)
    bps = pps * PAGE_SIZE // L_SEL
    off = np.arange(LANES // bps)[:, None, None]
    expand = jnp.asarray(np.arange(LANES)[None, :, None] == bps * off + np.arange(step_keys)[None, None, :] // L_SEL,
                         BF16)
    q5 = q.reshape(db, ds, KV_HEADS, GROUP, HEAD_DIM) * QK_SCALE
    qbd = jnp.einsum('bjkgd,kc->bkgjcd', q5, jnp.eye(KV_HEADS, dtype=F32)).reshape(db, ncol, KV_DIM).astype(BF16)
    gates_s = ng[:, :3 * N_HEADS].reshape(db, ds, N_HEADS, 3).transpose(0, 2, 1, 3).reshape(db, ncol, 3)
    new_all = jnp.pad(jnp.concatenate([kv_s3[..., 2 * KV_DIM:], kvw_s3], axis=-1),
                      ((0, 0), (0, PAGE_SIZE - ds), (0, 0)))
    o_bd = _nsa_sample(cache_t, page_table, qbd, gates_s, kcvc_s, win_t, new_all, bc, bw, bstep, bnew, ovt, gg,
                       expand, pps)
    o6 = o_bd.reshape(db, KV_HEADS, GROUP, ds, KV_HEADS, HEAD_DIM)
    y_nsa = jnp.einsum('bkgjkd->bjkgd', o6).reshape(db * ds, Q_DIM)
    y_sample = mix_ffn(xs2, y_lru, y_nsa, mg).reshape(db, ds, D_MODEL)
    kv_rows_sample = kv_s3.reshape(1, db, ds, 4, KV_HEADS, HEAD_DIM)
    win_all = jnp.concatenate([state_kv_win[0].reshape(db, wb, 2 * KV_DIM), kvw_s3], axis=1)
    win_sample = win_all[:, win_all.shape[1] - WINDOW:].reshape(1, db, WINDOW, 2, KV_HEADS, HEAD_DIM)
    conv_s = conv_t.swapaxes(0, 1)

    return (y_prompt, y_sample, kv_rows_prompt, win_prompt, conv_p[None], h_p.reshape(1, b, D_LRU),
            kv_rows_sample, win_sample, conv_s[None], h_s[None])
```

```python
import functools
import math

import numpy as np
import jax
import jax.numpy as jnp
from jax import lax
from jax.experimental import pallas as pl
from jax.experimental.pallas import tpu as pltpu

F32 = jnp.float32
BF16 = jnp.bfloat16

D_MODEL = 1024
D_LRU = 1280
LRU_HEADS = 16
LRU_BLOCK = D_LRU // LRU_HEADS
CONV_W = 4
LRU_C = 8.0
N_HEADS = 16
HEAD_DIM = 64
KV_HEADS = 4
GROUP = N_HEADS // KV_HEADS
L_CMP = 32
STRIDE = 16
CMP_HIDDEN = 128
L_SEL = 64
N_SEL = 16
N_LOCAL = 2
WINDOW = 512
PAGE_SIZE = 128
N_BUCKETS = 32
MAX_DISTANCE = 128
N_GROUPS = 4
EXPERTS_PER_GROUP = 8
N_EXPERTS = N_GROUPS * EXPERTS_PER_GROUP
D_EXPERT = 256
EPS = 1e-6
NEG = -1e30
FORCED = 1e9
Q_DIM = N_HEADS * HEAD_DIM
KV_DIM = KV_HEADS * HEAD_DIM
QT = 128
GQ = GROUP * QT
LANES = 128
VMEM_LIMIT = 56 * 1024 * 1024
KNOCKED = -3e38
ABSENT = -2e38
LOG2E = math.log2(math.e)
QK_SCALE = HEAD_DIM ** -0.5 * LOG2E


def _cparams(sem):
    return pltpu.CompilerParams(dimension_semantics=sem, vmem_limit_bytes=VMEM_LIMIT)


def _dot(a, b):
    return jnp.dot(a, b, preferred_element_type=F32)


def _dot_tn(a, b):
    return lax.dot_general(a, b, (((0,), (0,)), ((), ())), preferred_element_type=F32)


def _dot_nt(a, b):
    return lax.dot_general(a, b, (((1,), (1,)), ((), ())), preferred_element_type=F32)


def _split3(x):
    hi = x.astype(BF16)
    r1 = x - hi.astype(F32)
    mid = r1.astype(BF16)
    lo = (r1 - mid.astype(F32)).astype(BF16)
    return hi, mid, lo


def _dot_exact_rhs(a_bf16, x):
    hi, mid, lo = _split3(x)
    return _dot(a_bf16, hi) + _dot(a_bf16, mid) + _dot(a_bf16, lo)


def _dot_exact_lhs(x, b_bf16):
    hi, mid, lo = _split3(x)
    return _dot(hi, b_bf16) + _dot(mid, b_bf16) + _dot(lo, b_bf16)


def _proj_kernel(x_ref, g_ref, *refs):
    n = len(refs) // 2
    x = x_ref[...]
    xn = x * lax.rsqrt(jnp.mean(x * x, axis=-1, keepdims=True) + EPS) * g_ref[...]
    xb = xn.astype(BF16)
    for w_ref, o_ref in zip(refs[:n], refs[n:]):
        o_ref[...] = _dot(xb, w_ref[...])


def _proj(x2d, g, ws, tm=256):
    t = x2d.shape[0]
    tm = min(tm, t)
    in_specs = [pl.BlockSpec((tm, D_MODEL), lambda i: (i, 0)), pl.BlockSpec((1, D_MODEL), lambda i: (0, 0))]
    in_specs += [pl.BlockSpec(w.shape, lambda i: (0, 0), pipeline_mode=pl.Buffered(1)) for w in ws]
    out_specs = [pl.BlockSpec((tm, w.shape[1]), lambda i: (i, 0)) for w in ws]
    out_shape = [jax.ShapeDtypeStruct((t, w.shape[1]), F32) for w in ws]
    return pl.pallas_call(_proj_kernel, grid=(t // tm,), in_specs=in_specs, out_specs=out_specs,
                          out_shape=out_shape, compiler_params=_cparams(("parallel",)), name="proj")(x2d, g, *ws)


def _softplus(x):
    return jnp.maximum(x, 0.0) + jnp.log1p(jnp.exp(-jnp.abs(x)))


def _lru_gates(xc, wg_ref, bgx_ref, bga_ref, lam_ref):
    gates = _dot(xc.astype(BF16), wg_ref[...])
    gx = jax.nn.sigmoid(gates[:, :D_LRU] + bgx_ref[...])
    ga = jax.nn.sigmoid(gates[:, D_LRU:] + bga_ref[...])
    log_a = -LRU_C * ga * _softplus(-lam_ref[...])
    a = jnp.exp(log_a)
    th = jnp.tanh(log_a)
    u = jnp.sqrt(-2.0 * th / (1.0 - th)) * (gx * xc)
    return a, u


def _rglru_prompt_kernel(lx_ref, lg_ref, cw_ref, cb_ref, wg_ref, bgx_ref, bga_ref, lam_ref,
                         y_ref, conv_ref, h_ref, xext, hc, *, tc):
    t = pl.program_id(1)

    @pl.when(t == 0)
    def _():
        xext[0:8, :] = jnp.zeros((8, D_LRU), F32)
        hc[...] = jnp.zeros_like(hc)

    x = lx_ref[0]
    xext[8:8 + tc, :] = x
    cw = cw_ref[...]
    xc = cb_ref[...] + cw[3:4] * x
    for j in range(CONV_W - 1):
        xc = xc + cw[j:j + 1] * xext[5 + j:5 + j + tc, :]
    a, u = _lru_gates(xc, wg_ref, bgx_ref, bga_ref, lam_ref)
    row = lax.broadcasted_iota(jnp.int32, (tc, D_LRU), 0)
    s = 1
    while s < tc:
        a_sh = pltpu.roll(a, s, 0)
        u_sh = pltpu.roll(u, s, 0)
        m = row >= s
        u = jnp.where(m, a * u_sh + u, u)
        a = jnp.where(m, a * a_sh, a)
        s *= 2
    h = a * hc[0:1, :] + u
    hc[0:1, :] = h[tc - 1:tc, :]
    xext[0:8, :] = x[tc - 8:tc, :]
    y_ref[0] = h * jax.nn.gelu(lg_ref[0])
    conv_ref[0] = x[tc - (CONV_W - 1):tc, :]
    h_ref[0] = h[tc - 1:tc, :]


def _rglru_prompt(lx, lg, cw, cb, wg, bgx, bga, lam, tc=256):
    b, s, _ = lx.shape
    row = lambda shape: pl.BlockSpec(shape, lambda bi, ti: (0, 0))
    return pl.pallas_call(
        functools.partial(_rglru_prompt_kernel, tc=tc), grid=(b, s // tc),
        in_specs=[pl.BlockSpec((1, tc, D_LRU), lambda bi, ti: (bi, ti, 0)),
                  pl.BlockSpec((1, tc, D_LRU), lambda bi, ti: (bi, ti, 0)),
                  row((CONV_W, D_LRU)), row((1, D_LRU)), row((D_LRU, 2 * D_LRU)),
                  row((1, D_LRU)), row((1, D_LRU)), row((1, D_LRU))],
        out_specs=[pl.BlockSpec((1, tc, D_LRU), lambda bi, ti: (bi, ti, 0)),
                   pl.BlockSpec((1, CONV_W - 1, D_LRU), lambda bi, ti: (bi, 0, 0)),
                   pl.BlockSpec((1, 1, D_LRU), lambda bi, ti: (bi, 0, 0))],
        out_shape=[jax.ShapeDtypeStruct((b, s, D_LRU), F32),
                   jax.ShapeDtypeStruct((b, CONV_W - 1, D_LRU), F32),
                   jax.ShapeDtypeStruct((b, 1, D_LRU), F32)],
        scratch_shapes=[pltpu.VMEM((tc + 8, D_LRU), F32), pltpu.VMEM((8, D_LRU), F32)],
        compiler_params=_cparams(("parallel", "arbitrary")), name="rglru_prompt",
    )(lx, lg, cw, cb, wg, bgx, bga, lam)


def _rglru_sample_kernel(lx_ref, lg_ref, cbuf_ref, h0_ref, cw_ref, cb_ref, wg_ref, bgx_ref, bga_ref, lam_ref,
                         y_ref, conv_ref, h_ref, *, ds):
    cw = cw_ref[...]
    xp = [cbuf_ref[j] for j in range(CONV_W - 1)] + [lx_ref[j] for j in range(ds)]
    h = h0_ref[...]
    for t in range(ds):
        xc = cb_ref[...]
        for j in range(CONV_W):
            xc = xc + cw[j:j + 1] * xp[t + j]
        a, u = _lru_gates(xc, wg_ref, bgx_ref, bga_ref, lam_ref)
        h = a * h + u
        y_ref[t] = h * jax.nn.gelu(lg_ref[t])
    for j in range(CONV_W - 1):
        conv_ref[j] = xp[ds + j]
    h_ref[...] = h


def _rglru_sample(lx_t, lg_t, cbuf_t, h0, cw, cb, wg, bgx, bga, lam):
    ds, n, _ = lx_t.shape
    full = lambda a: pl.BlockSpec(a.shape, lambda i: (0,) * a.ndim)
    args = (lx_t, lg_t, cbuf_t, h0, cw, cb, wg, bgx, bga, lam)
    out_shape = [jax.ShapeDtypeStruct((ds, n, D_LRU), F32), jax.ShapeDtypeStruct((CONV_W - 1, n, D_LRU), F32),
                 jax.ShapeDtypeStruct((n, D_LRU), F32)]
    return pl.pallas_call(
        functools.partial(_rglru_sample_kernel, ds=ds), grid=(1,),
        in_specs=[full(a) for a in args], out_specs=[full(o) for o in out_shape], out_shape=out_shape,
        compiler_params=_cparams(("arbitrary",)), name="rglru_sample")(*args)


HEADS_PER_TILE = LANES // HEAD_DIM


def _chunk_project(load, w):
    feat = jnp.concatenate([load(s) for s in range(STRIDE)], axis=1).astype(BF16)
    return _dot(feat, w)


def _compress_prompt_kernel(xa_ref, xb_ref, w1_ref, b1_ref, w2_ref, o_ref, *, ncp):
    rows = lax.broadcasted_iota(jnp.int32, (ncp, HEAD_DIM), 0)
    for c, x_ref in enumerate((xa_ref, xb_ref)):
        p2 = _chunk_project(lambda s: x_ref[0, pl.ds(s, ncp, stride=STRIDE), :], w1_ref[0])
        for hh in range(HEADS_PER_TILE):
            k = c * HEADS_PER_TILE + hh
            p = p2[:, hh * 2 * CMP_HIDDEN:(hh + 1) * 2 * CMP_HIDDEN]
            pre = b1_ref[0] + p[:, :CMP_HIDDEN] + pltpu.roll(p[:, CMP_HIDDEN:], ncp - 1, 0)
            phi = _dot(jax.nn.gelu(pre).astype(BF16), w2_ref[0])
            o_ref[0, 0, k, 0:ncp, :] = jnp.zeros((ncp, HEAD_DIM), F32)
            o_ref[0, 0, k, ncp:2 * ncp, :] = jnp.where(rows < ncp - 1, phi, 0.0)


def _compress_prompt(kv, w1f, b1, w2):
    b, s, _ = kv.shape
    ncp = s // STRIDE
    return pl.pallas_call(
        functools.partial(_compress_prompt_kernel, ncp=ncp), grid=(b, 2),
        in_specs=[pl.BlockSpec((1, s, LANES), lambda bi, sl: (bi, 0, 2 * sl)),
                  pl.BlockSpec((1, s, LANES), lambda bi, sl: (bi, 0, 2 * sl + 1)),
                  pl.BlockSpec((1,) + w1f.shape[1:], lambda bi, sl: (sl, 0, 0)),
                  pl.BlockSpec((1, 1, CMP_HIDDEN), lambda bi, sl: (sl, 0, 0)),
                  pl.BlockSpec((1, CMP_HIDDEN, HEAD_DIM), lambda bi, sl: (sl, 0, 0))],
        out_specs=pl.BlockSpec((1, 1, KV_HEADS, 2 * ncp, HEAD_DIM), lambda bi, sl: (sl, bi, 0, 0, 0)),
        out_shape=jax.ShapeDtypeStruct((2, b, KV_HEADS, 2 * ncp, HEAD_DIM), F32),
        compiler_params=_cparams(("parallel", "parallel")), name="compress_prompt")(kv, kv, w1f, b1, w2)


def _compress_sample_kernel(pt_ref, *refs, pps, nsteps):
    pages = refs[:pps]
    new_ref, w1_ref, b1_ref, w2_ref, o_ref, xs, pscr = refs[pps:]
    st = pl.program_id(1)
    cps = PAGE_SIZE // STRIDE
    m = pps * cps
    ntok = nsteps * m
    tiles_per_slot = KV_DIM // LANES

    def stage(ref, j, transposed):
        for c in range(2 * tiles_per_slot):
            if transposed:
                blk = ref[0, c * LANES:(c + 1) * LANES, :].T
            else:
                blk = ref[0, :, c * LANES:(c + 1) * LANES]
            xs[c, j * PAGE_SIZE:(j + 1) * PAGE_SIZE, :] = blk

    def project(nrows, row0):
        for sl in range(2):
            for c in range(tiles_per_slot):
                p2 = _chunk_project(lambda s: xs[sl * tiles_per_slot + c, pl.ds(s, nrows, stride=STRIDE), :],
                                    w1_ref[sl])
                for hh in range(HEADS_PER_TILE):
                    pscr[sl, c * HEADS_PER_TILE + hh, pl.ds(row0, nrows), :] = (
                        p2[:, hh * 2 * CMP_HIDDEN:(hh + 1) * 2 * CMP_HIDDEN])

    for j, pg in enumerate(pages):
        stage(pg, j, True)
    project(m, pl.multiple_of(st * m, m))

    @pl.when(st == nsteps - 1)
    def _():
        stage(new_ref, 0, False)
        project(cps, ntok)
        for sl in range(2):
            toks = []
            for k in range(KV_HEADS):
                pre = (b1_ref[sl] + pscr[sl, k, 0:ntok, 0:CMP_HIDDEN]
                       + pscr[sl, k, 1:ntok + 1, CMP_HIDDEN:2 * CMP_HIDDEN])
                toks.append(_dot(jax.nn.gelu(pre).astype(BF16), w2_ref[sl]))
            o_ref[0, sl] = jnp.concatenate(toks, axis=1)


def _compress_sample(cache, page_table, new_rows, w1f, b1, w2, pps=16):
    db, npg = page_table.shape
    pps = min(pps, npg)
    nsteps = npg // pps
    cps = PAGE_SIZE // STRIDE
    ntok = npg * cps

    def page_spec(j):
        return pl.BlockSpec((1, 2 * KV_DIM, PAGE_SIZE), lambda bi, st, pt: (pt[bi, st * pps + j], 0, 0))

    const = lambda shape: pl.BlockSpec(shape, lambda bi, st, pt: (0,) * len(shape))
    grid_spec = pltpu.PrefetchScalarGridSpec(
        num_scalar_prefetch=1, grid=(db, nsteps),
        in_specs=[page_spec(j) for j in range(pps)] + [
            pl.BlockSpec((1, PAGE_SIZE, 2 * KV_DIM), lambda bi, st, pt: (bi, 0, 0)),
            const(w1f.shape), const((2, 1, CMP_HIDDEN)),
            const((2, CMP_HIDDEN, HEAD_DIM))],
        out_specs=pl.BlockSpec((1, 2, ntok, KV_DIM), lambda bi, st, pt: (bi, 0, 0, 0)),
        scratch_shapes=[pltpu.VMEM((2 * KV_DIM // LANES, pps * PAGE_SIZE, LANES), F32),
                        pltpu.VMEM((2, KV_HEADS, ntok + cps, 2 * CMP_HIDDEN), F32)])
    return pl.pallas_call(
        functools.partial(_compress_sample_kernel, pps=pps, nsteps=nsteps), grid_spec=grid_spec,
        out_shape=jax.ShapeDtypeStruct((db, 2, ntok, KV_DIM), F32),
        compiler_params=_cparams(("parallel", "arbitrary")), name="compress_sample",
    )(page_table, *([cache] * pps), new_rows, w1f, b1, w2)


def _topk_mask(score, axis):
    n = score.shape[axis]
    idx = lax.broadcasted_iota(jnp.int32, score.shape, axis).astype(F32)

    def body(_, carry):
        sc, sel = carry
        mx = jnp.max(sc, axis=axis, keepdims=True)
        first = jnp.min(jnp.where(sc == mx, idx, float(n)), axis=axis, keepdims=True)
        hit = idx == first
        return jnp.where(hit, KNOCKED, sc), jnp.where(hit, 1.0, sel)

    return lax.fori_loop(0, N_SEL, body, (score, jnp.zeros_like(score)))[1]


def _softmax2(s, axis):
    m = jnp.max(s, axis=axis, keepdims=True)
    e = jnp.where(s > 0.5 * NEG, jnp.exp2(s - m), 0.0)
    l = jnp.sum(e, axis=axis, keepdims=True)
    return e * (1.0 / jnp.maximum(l, 1e-30))


def _nsa_prompt_kernel(qT_ref, gT_ref, kc_ref, vc_ref, ksel_ref, vselT_ref, kwin_ref, vwinT_ref,
                       tc_ref, tiles_ref, near_ref, crow_ref, ov_ref, y_ref,
                       qk_s, madd_s, oc_s, m_s, l_s, acc_s, s_scr, p_scr, *, ncp, nsb):
    i = pl.program_id(1)
    qT = qT_ref[0]
    for k in range(KV_HEADS):
        qk_s[k] = (jnp.concatenate([qT[(GROUP * k + g) * HEAD_DIM:(GROUP * k + g + 1) * HEAD_DIM, :]
                                    for g in range(GROUP)], axis=1) * QK_SCALE).astype(BF16)
    m_s[...] = jnp.full(m_s.shape, NEG, F32)
    l_s[...] = jnp.zeros_like(l_s)
    acc_s[...] = jnp.zeros_like(acc_s)

    start = pl.multiple_of(8 * i + 8, 8)
    tok_ok = lax.broadcasted_iota(jnp.int32, (ncp, 1), 0) >= ncp - 8 - 8 * i
    rblk = lax.broadcasted_iota(jnp.int32, (nsb, QT), 0)
    qhalf = jnp.where(lax.broadcasted_iota(jnp.int32, (nsb, QT), 1) >= L_SEL, 1, 0)
    r_qb = nsb - 2 + qhalf
    exists = rblk >= nsb - 2 - 2 * i
    forced = (rblk == nsb - 2 - 2 * i) | (rblk > r_qb - N_LOCAL)
    for k in range(KV_HEADS):
        kc = kc_ref[0, 0, k, pl.ds(start, ncp), :].astype(BF16)
        vc = vc_ref[0, 0, k, pl.ds(start, ncp), :].astype(BF16)
        s = jnp.where(tok_ok, _dot(kc, qk_s[k]) + tc_ref[k], NEG)
        pn = _softmax2(s, 0)
        oc_s[k] = _dot_tn(vc, pn.astype(BF16))
        psum = pn[:, 0:QT]
        for g in range(1, GROUP):
            psum = psum + pn[:, g * QT:(g + 1) * QT]
        imp = _dot_exact_rhs(ov_ref[...], psum)
        score = jnp.where(rblk > r_qb, -FORCED, jnp.where(forced, FORCED, imp))
        score = jnp.where(exists, score, ABSENT)
        madd_s[k] = (_topk_mask(score, 0) - 1.0) * (-NEG)

    bpt = QT // L_SEL

    def attend(t, n, k_ref, vT_ref, add_of, branch):
        nk = n * QT
        kt = k_ref[0, t] if n == 1 else jnp.concatenate([k_ref[0, t + j] for j in range(n)], axis=0)
        vt = vT_ref[0, t] if n == 1 else jnp.concatenate([vT_ref[0, t + j] for j in range(n)], axis=1)
        for k in range(KV_HEADS):
            s_scr[k, 0:nk, :] = _dot(kt[:, k * HEAD_DIM:(k + 1) * HEAD_DIM], qk_s[k])
        alphas = []
        for k in range(KV_HEADS):
            idx = branch * KV_HEADS + k
            s = s_scr[k, 0:nk, :] + add_of(k)
            m_old = m_s[idx]
            m_new = jnp.maximum(m_old, jnp.max(s, axis=0, keepdims=True))
            alpha = jnp.exp2(m_old - m_new)
            p = jnp.exp2(s - m_new)
            l_s[idx] = alpha * l_s[idx] + jnp.sum(p, axis=0, keepdims=True)
            m_s[idx] = m_new
            p_scr[k, 0:nk, :] = p.astype(BF16)
            alphas.append(alpha)
        for k in range(KV_HEADS):
            idx = branch * KV_HEADS + k
            acc_s[idx] = alphas[k] * acc_s[idx] + _dot(vt[k * HEAD_DIM:(k + 1) * HEAD_DIM, :], p_scr[k, 0:nk, :])

    def sel_rows(k, t, n):
        r0 = bpt * (t - i) + nsb - bpt
        return [madd_s[k, pl.ds(r0 + j, 1), :] for j in range(bpt * n)]

    def far_add(t, n, masked):
        def add_of(k):
            c = crow_ref[k]
            if not masked:
                return c
            return jnp.concatenate(
                [jnp.concatenate([jnp.broadcast_to(r + c[:, g * QT:(g + 1) * QT], (L_SEL, QT)) for g in range(GROUP)],
                                 axis=1) for r in sel_rows(k, t, n)], axis=0)
        return add_of

    def table_add(table_of, t, n, masked):
        def add_of(k):
            if not masked:
                return table_of(k)
            mt = jnp.concatenate([jnp.broadcast_to(r, (L_SEL, QT)) for r in sel_rows(k, t, n)], axis=0)
            return table_of(k) + jnp.concatenate([mt] * GROUP, axis=1)
        return add_of

    nfar = jnp.maximum(i - 1, 0)

    def far_body(tt, c):
        attend(2 * tt, 2, ksel_ref, vselT_ref, far_add(2 * tt, 2, True), 0)
        return c

    lax.fori_loop(0, nfar // 2, far_body, 0)

    @pl.when(nfar % 2 == 1)
    def _():
        attend(nfar - 1, 1, ksel_ref, vselT_ref, far_add(nfar - 1, 1, True), 0)

    @pl.when(i >= 4)
    def _():
        attend(i - 4, 1, kwin_ref, vwinT_ref, table_add(lambda k: tiles_ref[1, k], i - 4, 1, False), 1)

    @pl.when(i >= 3)
    def _():
        attend(i - 3, 2, kwin_ref, vwinT_ref, far_add(i - 3, 2, False), 1)

    @pl.when(i == 2)
    def _():
        attend(0, 1, kwin_ref, vwinT_ref, far_add(0, 1, False), 1)

    @pl.when(i >= 1)
    def _():
        attend(i - 1, 2, ksel_ref, vselT_ref, table_add(lambda k: near_ref[k], i - 1, 2, True), 0)
        attend(i - 1, 2, kwin_ref, vwinT_ref, table_add(lambda k: near_ref[k], i - 1, 2, False), 1)

    @pl.when(i == 0)
    def _():
        attend(0, 1, ksel_ref, vselT_ref, table_add(lambda k: tiles_ref[0, k], 0, 1, True), 0)
        attend(0, 1, kwin_ref, vwinT_ref, table_add(lambda k: tiles_ref[0, k], 0, 1, False), 1)

    gate = jax.nn.sigmoid(gT_ref[0])
    for k in range(KV_HEADS):
        o_s = acc_s[k] * (1.0 / l_s[k])
        o_w = acc_s[KV_HEADS + k] * (1.0 / l_s[KV_HEADS + k])
        o_c = oc_s[k]
        for g in range(GROUP):
            h = GROUP * k + g
            cols = slice(g * QT, (g + 1) * QT)
            y_ref[0, h * HEAD_DIM:(h + 1) * HEAD_DIM, :] = (
                gate[3 * h:3 * h + 1] * o_c[:, cols] + gate[3 * h + 1:3 * h + 2] * o_s[:, cols]
                + gate[3 * h + 2:3 * h + 3] * o_w[:, cols])


def _nsa_prompt(qT, gT, kcvc, ksel, vselT, kwin, vwinT, tc, tiles, near, crow, ov):
    b, _, s = qT.shape
    nq = s // QT
    ncp = s // STRIDE
    nsb = s // L_SEL
    seq4 = lambda a: pl.BlockSpec((1,) + a.shape[1:], lambda bi, qi: (bi, 0, 0, 0), pipeline_mode=pl.Buffered(1))
    const = lambda a: pl.BlockSpec(a.shape, lambda bi, qi: (0,) * a.ndim, pipeline_mode=pl.Buffered(1))
    cmp_spec = lambda sl: pl.BlockSpec((1, 1, KV_HEADS, 2 * ncp, HEAD_DIM), lambda bi, qi: (sl, bi, 0, 0, 0),
                                       pipeline_mode=pl.Buffered(1))
    return pl.pallas_call(
        functools.partial(_nsa_prompt_kernel, ncp=ncp, nsb=nsb), grid=(b, nq),
        in_specs=[pl.BlockSpec((1, Q_DIM, QT), lambda bi, qi: (bi, 0, qi)),
                  pl.BlockSpec((1, 3 * N_HEADS, QT), lambda bi, qi: (bi, 0, qi)),
                  cmp_spec(0), cmp_spec(1), seq4(ksel), seq4(vselT), seq4(kwin), seq4(vwinT),
                  const(tc), const(tiles), const(near), const(crow), const(ov)],
        out_specs=pl.BlockSpec((1, Q_DIM, QT), lambda bi, qi: (bi, 0, qi)),
        out_shape=jax.ShapeDtypeStruct((b, Q_DIM, s), F32),
        scratch_shapes=[pltpu.VMEM((KV_HEADS, HEAD_DIM, GQ), BF16), pltpu.VMEM((KV_HEADS, nsb, QT), F32),
                        pltpu.VMEM((KV_HEADS, HEAD_DIM, GQ), F32), pltpu.VMEM((2 * KV_HEADS, 1, GQ), F32),
                        pltpu.VMEM((2 * KV_HEADS, 1, GQ), F32), pltpu.VMEM((2 * KV_HEADS, HEAD_DIM, GQ), F32),
                        pltpu.VMEM((KV_HEADS, 2 * QT, GQ), F32), pltpu.VMEM((KV_HEADS, 2 * QT, GQ), BF16)],
        compiler_params=_cparams(("parallel", "arbitrary")), name="nsa_prompt",
    )(qT, gT, kcvc, kcvc, ksel, vselT, kwin, vwinT, tc, tiles, near, crow, ov)


def _nsa_sample_kernel(pt_ref, *refs, pps, nsteps, nsb):
    pages = refs[:pps]
    (qbd_ref, gate_ref, kcvc_ref, win_ref, new_ref, bc_ref, bw_ref, bstep_ref, bnew_ref, ovt_ref, gg_ref, exp_ref,
     o_ref, madd_s, oc_s, ow_s, m_s, l_s, acc_s) = refs[pps:]
    st = pl.program_id(1)
    qbd = qbd_ref[0]
    ncol = qbd.shape[0]
    wb = win_ref.shape[2]
    bps = pps * (PAGE_SIZE // L_SEL)
    nlane = ovt_ref.shape[1]

    def online(s, pv_of):
        m_old = m_s[...]
        m_new = jnp.maximum(m_old, jnp.max(s, axis=1, keepdims=True))
        alpha = jnp.exp2(m_old - m_new)
        p = jnp.exp2(s - m_new)
        l_s[...] = alpha * l_s[...] + jnp.sum(p, axis=1, keepdims=True)
        acc_s[...] = alpha * acc_s[...] + pv_of(p.astype(BF16))
        m_s[...] = m_new

    @pl.when(st == 0)
    def _():
        m_s[...] = jnp.full(m_s.shape, NEG, F32)
        l_s[...] = jnp.zeros_like(l_s)
        acc_s[...] = jnp.zeros_like(acc_s)
        pn = _softmax2(_dot_nt(qbd, kcvc_ref[0, 0].astype(BF16)) + bc_ref[...], 1)
        oc_s[...] = _dot(pn.astype(BF16), kcvc_ref[0, 1].astype(BF16))
        imp = sum(_dot_tn(part, gg_ref[...]) for part in _split3(_dot_exact_lhs(pn, ovt_ref[...])))
        nrow = -(-nsb // 8) * 8
        blk = lax.broadcasted_iota(jnp.int32, (nrow, ncol), 0)
        qb = nsb - 1
        forced = (blk == 0) | (blk > qb - N_LOCAL)
        score = jnp.where(blk > qb, -FORCED, jnp.where(forced, FORCED, imp[0:nrow]))
        score = jnp.where(blk < nsb, score, ABSENT)
        madd = jnp.concatenate([jnp.where(_topk_mask(score, 0) > 0.0, 0.0, NEG),
                                jnp.full((nlane - nrow, ncol), NEG, F32)], axis=0).astype(BF16)
        for tl in range(nlane // LANES):
            madd_s[tl] = madd[tl * LANES:(tl + 1) * LANES, :]
        new = new_ref[0]
        s_w = jnp.concatenate([_dot(qbd, win_ref[0, 0:KV_DIM, :].astype(BF16)),
                               _dot_nt(qbd, new[:, 2 * KV_DIM:3 * KV_DIM].astype(BF16))], axis=1) + bw_ref[...]
        pw = _softmax2(s_w, 1).astype(BF16)
        ow_s[...] = (_dot_nt(pw[:, 0:wb], win_ref[0, KV_DIM:2 * KV_DIM, :].astype(BF16))
                     + _dot(pw[:, wb:], new[:, 3 * KV_DIM:4 * KV_DIM].astype(BF16)))

    kt = jnp.concatenate([pg[0, 0:KV_DIM, :] for pg in pages], axis=1).astype(BF16)
    vt = jnp.concatenate([pg[0, KV_DIM:2 * KV_DIM, :] for pg in pages], axis=1).astype(BF16)
    b0 = st * bps
    mask = _dot_tn(madd_s[b0 // LANES], exp_ref[(b0 % LANES) // bps])
    bias = jnp.where(st == nsteps - 1, bstep_ref[1], bstep_ref[0])
    online(_dot(qbd, kt) + mask + bias, lambda p: _dot_nt(p, vt))

    @pl.when(st == nsteps - 1)
    def _():
        new = new_ref[0]
        lb = nsb - 1
        spread = (lax.broadcasted_iota(jnp.int32, (LANES, PAGE_SIZE), 0) == lb % LANES).astype(BF16)
        online(_dot_nt(qbd, new[:, 0:KV_DIM].astype(BF16)) + bnew_ref[...] + _dot_tn(madd_s[lb // LANES], spread),
               lambda p: _dot(p, new[:, KV_DIM:2 * KV_DIM].astype(BF16)))
        gate = jax.nn.sigmoid(gate_ref[0])
        o_ref[0] = (gate[:, 0:1] * oc_s[...] + gate[:, 1:2] * (acc_s[...] * (1.0 / l_s[...]))
                    + gate[:, 2:3] * ow_s[...])


def _nsa_sample(cache, page_table, qbd, gates, kcvc, win, new_rows, bc, bw, bstep, bnew, ovt, gg, expand, pps):
    db, npg = page_table.shape
    nsteps = npg // pps
    nsb = npg * (PAGE_SIZE // L_SEL) + 1
    ncol = qbd.shape[1]
    nlane = ovt.shape[1]

    def page_spec(j):
        return pl.BlockSpec((1, 2 * KV_DIM, PAGE_SIZE), lambda bi, st, pt: (pt[bi, st * pps + j], 1, 0))

    const = lambda a: pl.BlockSpec(a.shape, lambda bi, st, pt: (0,) * a.ndim)
    seq = lambda a: pl.BlockSpec((1,) + a.shape[1:], lambda bi, st, pt: (bi,) + (0,) * (a.ndim - 1))
    grid_spec = pltpu.PrefetchScalarGridSpec(
        num_scalar_prefetch=1, grid=(db, nsteps),
        in_specs=[page_spec(j) for j in range(pps)] + [seq(qbd), seq(gates), seq(kcvc), seq(win), seq(new_rows)]
        + [const(a) for a in (bc, bw, bstep, bnew, ovt, gg, expand)],
        out_specs=pl.BlockSpec((1, ncol, KV_DIM), lambda bi, st, pt: (bi, 0, 0)),
        scratch_shapes=[pltpu.VMEM((nlane // LANES, LANES, ncol), BF16), pltpu.VMEM((ncol, KV_DIM), F32),
                        pltpu.VMEM((ncol, KV_DIM), F32), pltpu.VMEM((ncol, 1), F32), pltpu.VMEM((ncol, 1), F32),
                        pltpu.VMEM((ncol, KV_DIM), F32)])
    return pl.pallas_call(
        functools.partial(_nsa_sample_kernel, pps=pps, nsteps=nsteps, nsb=nsb), grid_spec=grid_spec,
        out_shape=jax.ShapeDtypeStruct((db, ncol, KV_DIM), F32),
        compiler_params=_cparams(("parallel", "arbitrary")), name="nsa_sample",
    )(page_table, *([cache] * pps), qbd, gates, kcvc, win, new_rows, bc, bw, bstep, bnew, ovt, gg, expand)


def _merge_kernel(x_ref, ylru_ref, ynsa_ref, mg_ref, wl_ref, wn_ref, wo_ref, gf_ref, wr_ref, br_ref,
                  x2_ref, hn_ref, wt_ref):
    gate = jax.nn.sigmoid(mg_ref[...])
    mixed = (gate[:, :D_MODEL] * _dot(ylru_ref[...].astype(BF16), wl_ref[...])
             + gate[:, D_MODEL:] * _dot(ynsa_ref[...].astype(BF16), wn_ref[...]))
    x2 = x_ref[...] + _dot(mixed.astype(BF16), wo_ref[...])
    x2_ref[...] = x2
    hn = x2 * lax.rsqrt(jnp.mean(x2 * x2, axis=-1, keepdims=True) + EPS) * gf_ref[...]
    hn_ref[...] = hn.astype(BF16)
    logits = jnp.dot(hn, wr_ref[...], precision=lax.Precision.HIGHEST, preferred_element_type=F32) + br_ref[...]
    lane = lax.broadcasted_iota(jnp.int32, logits.shape, 1)
    lanef = lane.astype(F32)
    big = float(LANES)
    gl = jnp.where(lane < N_GROUPS, logits, NEG)
    gmax = jnp.max(gl, axis=-1, keepdims=True)
    grp = jnp.min(jnp.where(gl == gmax, lanef, big), axis=-1, keepdims=True)
    p_grp = 1.0 / jnp.sum(jnp.where(lane < N_GROUPS, jnp.exp(gl - gmax), 0.0), axis=-1, keepdims=True)
    lo = N_GROUPS + grp * EXPERTS_PER_GROUP
    el = jnp.where((lanef >= lo) & (lanef < lo + EXPERTS_PER_GROUP), logits, NEG)
    v1 = jnp.max(el, axis=-1, keepdims=True)
    i1 = jnp.min(jnp.where(el == v1, lanef, big), axis=-1, keepdims=True)
    el2 = jnp.where(lanef == i1, NEG, el)
    v2 = jnp.max(el2, axis=-1, keepdims=True)
    i2 = jnp.min(jnp.where(el2 == v2, lanef, big), axis=-1, keepdims=True)
    e2 = jnp.exp(v2 - v1)
    den = 1.0 / (1.0 + e2)
    wt_ref[...] = jnp.where(lanef == i1, den * p_grp, jnp.where(lanef == i2, e2 * den * p_grp, 0.0))


def _merge(x2d, ylru, ynsa, mg, wl, wn, wo, gf, wr, br, tm=256):
    t = x2d.shape[0]
    tm = min(tm, t)
    tile = lambda a: pl.BlockSpec((tm, a.shape[1]), lambda i: (i, 0))
    const = lambda a: pl.BlockSpec(a.shape, lambda i: (0, 0), pipeline_mode=pl.Buffered(1))
    return pl.pallas_call(
        _merge_kernel, grid=(t // tm,),
        in_specs=[tile(x2d), tile(ylru), tile(ynsa), tile(mg)] + [const(a) for a in (wl, wn, wo, gf, wr, br)],
        out_specs=[pl.BlockSpec((tm, D_MODEL), lambda i: (i, 0)), pl.BlockSpec((tm, D_MODEL), lambda i: (i, 0)),
                   pl.BlockSpec((tm, LANES), lambda i: (i, 0))],
        out_shape=[jax.ShapeDtypeStruct((t, D_MODEL), F32), jax.ShapeDtypeStruct((t, D_MODEL), BF16),
                   jax.ShapeDtypeStruct((t, LANES), F32)],
        compiler_params=_cparams(("parallel",)), name="merge")(x2d, ylru, ynsa, mg, wl, wn, wo, gf, wr, br)


def _moe_kernel(hn_ref, wt_ref, x2_ref, wg_ref, wu_ref, wd_ref, gfin_ref, y_ref, acc):
    c = pl.program_id(1)

    @pl.when(c == 0)
    def _():
        acc[...] = jnp.zeros_like(acc)

    h = hn_ref[...]
    wt = wt_ref[...]
    lane = lax.broadcasted_iota(jnp.int32, wt.shape, 1)
    total = acc[...]
    for e in range(EXPERTS_PER_GROUP):
        act = jax.nn.silu(_dot(h, wg_ref[e])) * _dot(h, wu_ref[e])
        w_e = jnp.sum(jnp.where(lane == N_GROUPS + c * EXPERTS_PER_GROUP + e, wt, 0.0), axis=-1, keepdims=True)
        act = jnp.where(w_e != 0.0, act * w_e, 0.0)
        total = total + _dot(act.astype(BF16), wd_ref[e])
    acc[...] = total

    @pl.when(c == N_GROUPS - 1)
    def _():
        x = x2_ref[...] + total
        y_ref[...] = x * lax.rsqrt(jnp.mean(x * x, axis=-1, keepdims=True) + EPS) * gfin_ref[...]


def _moe(hn, wt, x2, wg, wu, wd, gfin, tm=512):
    t = hn.shape[0]
    tm = min(tm, t)
    return pl.pallas_call(
        _moe_kernel, grid=(t // tm, N_GROUPS),
        in_specs=[pl.BlockSpec((tm, D_MODEL), lambda i, c: (i, 0)), pl.BlockSpec((tm, LANES), lambda i, c: (i, 0)),
                  pl.BlockSpec((tm, D_MODEL), lambda i, c: (i, 0)),
                  pl.BlockSpec((EXPERTS_PER_GROUP, D_MODEL, D_EXPERT), lambda i, c: (c, 0, 0)),
                  pl.BlockSpec((EXPERTS_PER_GROUP, D_MODEL, D_EXPERT), lambda i, c: (c, 0, 0)),
                  pl.BlockSpec((EXPERTS_PER_GROUP, D_EXPERT, D_MODEL), lambda i, c: (c, 0, 0)),
                  pl.BlockSpec((1, D_MODEL), lambda i, c: (0, 0))],
        out_specs=pl.BlockSpec((tm, D_MODEL), lambda i, c: (i, 0)),
        out_shape=jax.ShapeDtypeStruct((t, D_MODEL), F32),
        scratch_shapes=[pltpu.VMEM((tm, D_MODEL), F32)],
        compiler_params=_cparams(("parallel", "arbitrary")), name="moe")(hn, wt, x2, wg, wu, wd, gfin)


def _bucket_table():
    n = np.arange(MAX_DISTANCE + 1)
    exact = N_BUCKETS // 2
    nf = np.maximum(n, exact).astype(np.float64)
    large = exact + (np.log(nf / exact) / math.log(MAX_DISTANCE / exact) * (N_BUCKETS - exact)).astype(np.int32)
    return np.where(n < exact, n, np.minimum(large, N_BUCKETS - 1))


def _prompt_tile(rel_b, rel, mask):
    r = rel.shape[0]
    t = rel_b[np.clip(rel, 0, MAX_DISTANCE)] * LOG2E
    t = jnp.where(jnp.asarray(mask)[..., None], t, NEG)
    return t.reshape(r, QT, KV_HEADS, GROUP).transpose(2, 0, 3, 1).reshape(KV_HEADS, r, GQ)


def _sample_tile(rel_b, rel, mask):
    r, ds = rel.shape
    t = rel_b[np.clip(rel, 0, MAX_DISTANCE)] * LOG2E
    t = jnp.where(jnp.asarray(mask)[..., None], t, NEG)
    return t.transpose(2, 1, 0).reshape(N_HEADS * ds, r)


def _block_diag(w):
    eye = jnp.eye(LRU_HEADS, dtype=w.dtype)
    return jnp.einsum('hij,hk->hikj', w, eye).reshape(D_LRU, D_LRU)


def kernel(x_prompt, x_sample, cache_kv, state_kv_win, state_conv, state_h, page_table, g_mix, w_in, conv_w, conv_b,
           w_gate_a, b_gate_a, w_gate_x, b_gate_x, lru_lambda, cmp_w1, cmp_b1, cmp_w2, w_lru_out, w_nsa_out, w_out,
           g_ffn, w_router_group, b_router_group, w_router_expert, b_router_expert, w_exp_gate, w_exp_up,
           w_exp_down, rel_bias, g_final):
    assert w_in.shape[0] == 1, "single layer"
    b, s, _ = x_prompt.shape
    db, ds, _ = x_sample.shape
    npg = page_table.shape[1]
    past = npg * PAGE_SIZE
    wb = state_kv_win.shape[2]
    assert s % 256 == 0 and s // L_SEL >= N_SEL and CONV_W - 1 <= ds <= STRIDE and wb == WINDOW and past >= WINDOW

    w = w_in[0].astype(BF16)
    o = 0
    ws = []
    for width in (D_LRU, D_LRU, Q_DIM, 4 * KV_DIM, 2 * KV_DIM, 3 * N_HEADS, 2 * D_MODEL):
        ws.append(w[:, o:o + width])
        o += width
    ws[5] = jnp.pad(ws[5], ((0, 0), (0, LANES - 3 * N_HEADS)))
    g_mix2 = g_mix[0][None]
    wg = jnp.concatenate([_block_diag(w_gate_x[0]), _block_diag(w_gate_a[0])], axis=1).astype(BF16)
    lru_args = (conv_w[0], conv_b[0][None], wg, b_gate_x[0][None], b_gate_a[0][None], lru_lambda[0][None])
    w1 = cmp_w1[0].reshape(2, 2, STRIDE, HEAD_DIM, CMP_HIDDEN)
    w1f = jnp.einsum('armdn,hk->amhdkrn', w1, jnp.eye(HEADS_PER_TILE, dtype=F32)).reshape(
        2, STRIDE * LANES, HEADS_PER_TILE * 2 * CMP_HIDDEN).astype(BF16)
    b1 = cmp_b1[0][:, None, :]
    w2 = cmp_w2[0].astype(BF16)
    wl, wn, wo = w_lru_out[0].astype(BF16), w_nsa_out[0].astype(BF16), w_out[0].astype(BF16)
    n_r = N_GROUPS + N_EXPERTS
    wr = jnp.pad(jnp.concatenate([w_router_group[0], w_router_expert[0]], axis=1), ((0, 0), (0, LANES - n_r)))
    br = jnp.pad(jnp.concatenate([b_router_group[0], b_router_expert[0]]), (0, LANES - n_r))[None]
    weg, weu, wed = w_exp_gate[0].astype(BF16), w_exp_up[0].astype(BF16), w_exp_down[0].astype(BF16)
    gf, gfin = g_ffn[0][None], g_final[None]
    rel_b = rel_bias.astype(F32)[_bucket_table()]

    def mix_ffn(x2d, ylru, ynsa, mg):
        x2, hn, wt = _merge(x2d, ylru, ynsa, mg, wl, wn, wo, gf, wr, br)
        return _moe(hn, wt, x2, weg, weu, wed, gfin)

    xp2 = x_prompt.reshape(b * s, D_MODEL)
    lx, lg, q, kv, kvw, ng, mg = _proj(xp2, g_mix2, ws)
    y_lru, conv_p, h_p = _rglru_prompt(lx.reshape(b, s, D_LRU), lg.reshape(b, s, D_LRU), *lru_args)
    kv3 = kv.reshape(b, s, 4 * KV_DIM)
    kvw3 = kvw.reshape(b, s, 2 * KV_DIM)
    kcvc = _compress_prompt(kv3, w1f, b1, w2)
    nt = s // QT
    ncp = s // STRIDE
    nsb = s // L_SEL

    def key_tiles(x):
        return x.astype(BF16).reshape(b, nt, QT, KV_DIM)

    def val_tiles(x):
        return x.astype(BF16).reshape(b, nt, QT, KV_DIM).swapaxes(2, 3)

    qi = np.arange(QT)[None, :]
    lrow = np.arange(ncp)[:, None]
    rel_c = qi - STRIDE * lrow + STRIDE * ncp - (STRIDE * 8 + L_CMP - 1)
    n_far = int(np.sum(np.all(rel_c >= MAX_DISTANCE, axis=1)))
    crow = _prompt_tile(rel_b, np.full((1, QT), MAX_DISTANCE), np.ones((1, QT), bool))
    tc = jnp.concatenate([jnp.broadcast_to(crow, (KV_HEADS, n_far, GQ)),
                          _prompt_tile(rel_b, rel_c[n_far:], rel_c[n_far:] >= 0)], axis=1)
    kj = np.arange(QT)[:, None]
    diag = _prompt_tile(rel_b, qi - kj, qi - kj >= 0)
    tiles = jnp.stack([diag, _prompt_tile(rel_b, WINDOW + qi - kj, qi - kj < 0)])
    near = jnp.concatenate([_prompt_tile(rel_b, QT + qi - kj, np.ones((QT, QT), bool)), diag], axis=1)
    rr = np.arange(nsb)[:, None]
    ll = np.arange(ncp)[None, :]
    ov = jnp.asarray((ll >= 4 * rr - 1) & (ll <= 4 * rr + 3), BF16)
    qT = q.reshape(b, s, Q_DIM).swapaxes(1, 2)
    gT = ng[:, :3 * N_HEADS].reshape(b, s, 3 * N_HEADS).swapaxes(1, 2)
    y_nsaT = _nsa_prompt(qT, gT, kcvc, key_tiles(kv3[..., 2 * KV_DIM:3 * KV_DIM]),
                         val_tiles(kv3[..., 3 * KV_DIM:]), key_tiles(kvw3[..., :KV_DIM]),
                         val_tiles(kvw3[..., KV_DIM:]), tc, tiles, near, crow, ov)
    y_nsa = y_nsaT.swapaxes(1, 2).reshape(b * s, Q_DIM)
    y_prompt = mix_ffn(xp2, y_lru.reshape(b * s, D_LRU), y_nsa, mg).reshape(b, s, D_MODEL)
    kv_rows_prompt = kv3.reshape(1, b, s, 4, KV_HEADS, HEAD_DIM)
    win_prompt = kvw3[:, s - min(WINDOW, s):].reshape(1, b, min(WINDOW, s), 2, KV_HEADS, HEAD_DIM)

    xs2 = x_sample.reshape(db * ds, D_MODEL)
    lx, lg, q, kv, kvw, ng, mg = _proj(xs2, g_mix2, ws)
    tmaj = lambda a: a.reshape(db, ds, D_LRU).swapaxes(0, 1)
    y_lru_t, conv_t, h_s = _rglru_sample(tmaj(lx), tmaj(lg), state_conv[0].swapaxes(0, 1), state_h[0], *lru_args)
    y_lru = y_lru_t.swapaxes(0, 1).reshape(db * ds, D_LRU)
    kv_s3 = kv.reshape(db, ds, 4 * KV_DIM)
    kvw_s3 = kvw.reshape(db, ds, 2 * KV_DIM)
    cache_t = cache_kv[0].transpose(0, 2, 3, 4, 1).reshape(cache_kv.shape[1], 4 * KV_DIM, PAGE_SIZE)
    win_t = state_kv_win[0].transpose(0, 2, 3, 4, 1).reshape(db, 2 * KV_DIM, wb)
    new_c = jnp.pad(kv_s3[..., :2 * KV_DIM], ((0, 0), (0, PAGE_SIZE - ds), (0, 0)))
    pps = 32
    assert npg % pps == 0 and LANES % (pps * PAGE_SIZE // L_SEL) == 0
    kcvc_s = _compress_sample(cache_t, page_table, new_c, w1f, b1, w2, pps)

    nsb_s = past // L_SEL + 1
    ntok = past // STRIDE
    jq = np.arange(ds)[None, :]
    nrow = np.arange(ntok)[:, None]
    rel = past + jq - STRIDE * nrow - (L_CMP - 1)
    bc = _sample_tile(rel_b, rel, rel >= 0)
    rn = np.arange(PAGE_SIZE)[:, None]
    new_rel, new_ok = jq - rn, (jq - rn >= 0) & (rn < ds)
    rw = np.arange(wb)[:, None]
    bw = _sample_tile(rel_b, np.concatenate([wb + jq - rw, new_rel], axis=0),
                      np.concatenate([(wb + jq - rw < WINDOW) & (past - wb + rw >= 0), new_ok], axis=0))
    bnew = _sample_tile(rel_b, new_rel, new_ok)
    step_keys = pps * PAGE_SIZE
    far_rel = np.full((step_keys, ds), MAX_DISTANCE)
    last_rel = far_rel.copy()
    last_rel[step_keys - PAGE_SIZE:] = PAGE_SIZE + jq - rn
    all_ok = np.ones((step_keys, ds), bool)
    bstep = jnp.stack([_sample_tile(rel_b, far_rel, all_ok), _sample_tile(rel_b, last_rel, all_ok)])
    nlane = -(-nsb_s // LANES) * LANES
    bb = np.arange(nlane)[None, :]
    tt = np.arange(ntok)[:, None]
    ovt = jnp.asarray((tt >= 4 * bb - 1) & (tt <= 4 * bb + 3) & (bb < nsb_s), BF16)
    ncol = N_HEADS * ds
    col = np.arange(ncol)
    kvh_c, j_c = col // (GROUP * ds), col % ds
    gg = jnp.asarray((kvh_c[:, None] == kvh_c[None, :]) & (j_c[:, None] == j_c[None, :]), BF16)
    bps = pps * PAGE_SIZE // L_SEL
    off = np.arange(LANES // bps)[:, None, None]
    expand = jnp.asarray(np.arange(LANES)[None, :, None] == bps * off + np.arange(step_keys)[None, None, :] // L_SEL,
                         BF16)
    q5 = q.reshape(db, ds, KV_HEADS, GROUP, HEAD_DIM) * QK_SCALE
    qbd = jnp.einsum('bjkgd,kc->bkgjcd', q5, jnp.eye(KV_HEADS, dtype=F32)).reshape(db, ncol, KV_DIM).astype(BF16)
    gates_s = ng[:, :3 * N_HEADS].reshape(db, ds, N_HEADS, 3).transpose(0, 2, 1, 3).reshape(db, ncol, 3)
    new_all = jnp.pad(jnp.concatenate([kv_s3[..., 2 * KV_DIM:], kvw_s3], axis=-1),
                      ((0, 0), (0, PAGE_SIZE - ds), (0, 0)))
    o_bd = _nsa_sample(cache_t, page_table, qbd, gates_s, kcvc_s, win_t, new_all, bc, bw, bstep, bnew, ovt, gg,
                       expand, pps)
    o6 = o_bd.reshape(db, KV_HEADS, GROUP, ds, KV_HEADS, HEAD_DIM)
    y_nsa = jnp.einsum('bkgjkd->bjkgd', o6).reshape(db * ds, Q_DIM)
    y_sample = mix_ffn(xs2, y_lru, y_nsa, mg).reshape(db, ds, D_MODEL)
    kv_rows_sample = kv_s3.reshape(1, db, ds, 4, KV_HEADS, HEAD_DIM)
    win_all = jnp.concatenate([state_kv_win[0].reshape(db, wb, 2 * KV_DIM), kvw_s3], axis=1)
    win_sample = win_all[:, win_all.shape[1] - WINDOW:].reshape(1, db, WINDOW, 2, KV_HEADS, HEAD_DIM)
    conv_s = conv_t.swapaxes(0, 1)

    return (y_prompt, y_sample, kv_rows_prompt, win_prompt, conv_p[None], h_p.reshape(1, b, D_LRU),
            kv_rows_sample, win_sample, conv_s[None], h_s[None])
```

```python
import functools
import math

import numpy as np
import jax
import jax.numpy as jnp
from jax import lax
from jax.experimental import pallas as pl
from jax.experimental.pallas import tpu as pltpu

F32 = jnp.float32
BF16 = jnp.bfloat16

D_MODEL = 1024
D_LRU = 1280
LRU_HEADS = 16
LRU_BLOCK = D_LRU // LRU_HEADS
CONV_W = 4
LRU_C = 8.0
N_HEADS = 16
HEAD_DIM = 64
KV_HEADS = 4
GROUP = N_HEADS // KV_HEADS
L_CMP = 32
STRIDE = 16
CMP_HIDDEN = 128
L_SEL = 64
N_SEL = 16
N_LOCAL = 2
WINDOW = 512
PAGE_SIZE = 128
N_BUCKETS = 32
MAX_DISTANCE = 128
N_GROUPS = 4
EXPERTS_PER_GROUP = 8
N_EXPERTS = N_GROUPS * EXPERTS_PER_GROUP
D_EXPERT = 256
EPS = 1e-6
NEG = -1e30
FORCED = 1e9
Q_DIM = N_HEADS * HEAD_DIM
KV_DIM = KV_HEADS * HEAD_DIM
QT = 128
GQ = GROUP * QT
LANES = 128
VMEM_LIMIT = 56 * 1024 * 1024
KNOCKED = -3e38
ABSENT = -2e38
LOG2E = math.log2(math.e)
QK_SCALE = HEAD_DIM ** -0.5 * LOG2E
ONES_ROWS = 16


def _cparams(sem):
    return pltpu.CompilerParams(dimension_semantics=sem, vmem_limit_bytes=VMEM_LIMIT)


def _dot(a, b):
    return jnp.dot(a, b, preferred_element_type=F32)


def _dot_tn(a, b):
    return lax.dot_general(a, b, (((0,), (0,)), ((), ())), preferred_element_type=F32)


def _dot_nt(a, b):
    return lax.dot_general(a, b, (((1,), (1,)), ((), ())), preferred_element_type=F32)


def _split3(x):
    hi = x.astype(BF16)
    r1 = x - hi.astype(F32)
    mid = r1.astype(BF16)
    lo = (r1 - mid.astype(F32)).astype(BF16)
    return hi, mid, lo


def _dot_exact_rhs(a_bf16, x):
    hi, mid, lo = _split3(x)
    return _dot(a_bf16, hi) + _dot(a_bf16, mid) + _dot(a_bf16, lo)


def _dot_exact_lhs(x, b_bf16):
    hi, mid, lo = _split3(x)
    return _dot(hi, b_bf16) + _dot(mid, b_bf16) + _dot(lo, b_bf16)


def _proj_kernel(x_ref, g_ref, *refs):
    n = len(refs) // 2
    x = x_ref[...]
    xn = x * lax.rsqrt(jnp.mean(x * x, axis=-1, keepdims=True) + EPS) * g_ref[...]
    xb = xn.astype(BF16)
    for w_ref, o_ref in zip(refs[:n], refs[n:]):
        o_ref[...] = _dot(xb, w_ref[...])


def _proj(x2d, g, ws, tm=256):
    t = x2d.shape[0]
    tm = min(tm, t)
    in_specs = [pl.BlockSpec((tm, D_MODEL), lambda i: (i, 0)), pl.BlockSpec((1, D_MODEL), lambda i: (0, 0))]
    in_specs += [pl.BlockSpec(w.shape, lambda i: (0, 0), pipeline_mode=pl.Buffered(1)) for w in ws]
    out_specs = [pl.BlockSpec((tm, w.shape[1]), lambda i: (i, 0)) for w in ws]
    out_shape = [jax.ShapeDtypeStruct((t, w.shape[1]), F32) for w in ws]
    return pl.pallas_call(_proj_kernel, grid=(t // tm,), in_specs=in_specs, out_specs=out_specs,
                          out_shape=out_shape, compiler_params=_cparams(("parallel",)), name="proj")(x2d, g, *ws)


def _softplus(x):
    return jnp.maximum(x, 0.0) + jnp.log1p(jnp.exp(-jnp.abs(x)))


def _lru_gates(xc, wg_ref, bgx_ref, bga_ref, lam_ref):
    gates = _dot(xc.astype(BF16), wg_ref[...])
    gx = jax.nn.sigmoid(gates[:, :D_LRU] + bgx_ref[...])
    ga = jax.nn.sigmoid(gates[:, D_LRU:] + bga_ref[...])
    log_a = -LRU_C * ga * _softplus(-lam_ref[...])
    a = jnp.exp(log_a)
    th = jnp.tanh(log_a)
    u = jnp.sqrt(-2.0 * th / (1.0 - th)) * (gx * xc)
    return a, u


def _rglru_prompt_kernel(lx_ref, lg_ref, cw_ref, cb_ref, wg_ref, bgx_ref, bga_ref, lam_ref,
                         y_ref, conv_ref, h_ref, xext, hc, *, tc):
    t = pl.program_id(1)

    @pl.when(t == 0)
    def _():
        xext[0:8, :] = jnp.zeros((8, D_LRU), F32)
        hc[...] = jnp.zeros_like(hc)

    x = lx_ref[0]
    xext[8:8 + tc, :] = x
    cw = cw_ref[...]
    xc = cb_ref[...] + cw[3:4] * x
    for j in range(CONV_W - 1):
        xc = xc + cw[j:j + 1] * xext[5 + j:5 + j + tc, :]
    a, u = _lru_gates(xc, wg_ref, bgx_ref, bga_ref, lam_ref)
    row = lax.broadcasted_iota(jnp.int32, (tc, D_LRU), 0)
    s = 1
    while s < tc:
        a_sh = pltpu.roll(a, s, 0)
        u_sh = pltpu.roll(u, s, 0)
        m = row >= s
        u = jnp.where(m, a * u_sh + u, u)
        a = jnp.where(m, a * a_sh, a)
        s *= 2
    h = a * hc[0:1, :] + u
    hc[0:1, :] = h[tc - 1:tc, :]
    xext[0:8, :] = x[tc - 8:tc, :]
    y_ref[0] = h * jax.nn.gelu(lg_ref[0])
    conv_ref[0] = x[tc - (CONV_W - 1):tc, :]
    h_ref[0] = h[tc - 1:tc, :]


def _rglru_prompt(lx, lg, cw, cb, wg, bgx, bga, lam, tc=256):
    b, s, _ = lx.shape
    row = lambda shape: pl.BlockSpec(shape, lambda bi, ti: (0, 0))
    return pl.pallas_call(
        functools.partial(_rglru_prompt_kernel, tc=tc), grid=(b, s // tc),
        in_specs=[pl.BlockSpec((1, tc, D_LRU), lambda bi, ti: (bi, ti, 0)),
                  pl.BlockSpec((1, tc, D_LRU), lambda bi, ti: (bi, ti, 0)),
                  row((CONV_W, D_LRU)), row((1, D_LRU)), row((D_LRU, 2 * D_LRU)),
                  row((1, D_LRU)), row((1, D_LRU)), row((1, D_LRU))],
        out_specs=[pl.BlockSpec((1, tc, D_LRU), lambda bi, ti: (bi, ti, 0)),
                   pl.BlockSpec((1, CONV_W - 1, D_LRU), lambda bi, ti: (bi, 0, 0)),
                   pl.BlockSpec((1, 1, D_LRU), lambda bi, ti: (bi, 0, 0))],
        out_shape=[jax.ShapeDtypeStruct((b, s, D_LRU), F32),
                   jax.ShapeDtypeStruct((b, CONV_W - 1, D_LRU), F32),
                   jax.ShapeDtypeStruct((b, 1, D_LRU), F32)],
        scratch_shapes=[pltpu.VMEM((tc + 8, D_LRU), F32), pltpu.VMEM((8, D_LRU), F32)],
        compiler_params=_cparams(("parallel", "arbitrary")), name="rglru_prompt",
    )(lx, lg, cw, cb, wg, bgx, bga, lam)


def _rglru_sample_kernel(lx_ref, lg_ref, cbuf_ref, h0_ref, cw_ref, cb_ref, wg_ref, bgx_ref, bga_ref, lam_ref,
                         y_ref, conv_ref, h_ref, *, ds):
    cw = cw_ref[...]
    xp = [cbuf_ref[j] for j in range(CONV_W - 1)] + [lx_ref[j] for j in range(ds)]
    h = h0_ref[...]
    for t in range(ds):
        xc = cb_ref[...]
        for j in range(CONV_W):
            xc = xc + cw[j:j + 1] * xp[t + j]
        a, u = _lru_gates(xc, wg_ref, bgx_ref, bga_ref, lam_ref)
        h = a * h + u
        y_ref[t] = h * jax.nn.gelu(lg_ref[t])
    for j in range(CONV_W - 1):
        conv_ref[j] = xp[ds + j]
    h_ref[...] = h


def _rglru_sample(lx_t, lg_t, cbuf_t, h0, cw, cb, wg, bgx, bga, lam):
    ds, n, _ = lx_t.shape
    full = lambda a: pl.BlockSpec(a.shape, lambda i: (0,) * a.ndim)
    args = (lx_t, lg_t, cbuf_t, h0, cw, cb, wg, bgx, bga, lam)
    out_shape = [jax.ShapeDtypeStruct((ds, n, D_LRU), F32), jax.ShapeDtypeStruct((CONV_W - 1, n, D_LRU), F32),
                 jax.ShapeDtypeStruct((n, D_LRU), F32)]
    return pl.pallas_call(
        functools.partial(_rglru_sample_kernel, ds=ds), grid=(1,),
        in_specs=[full(a) for a in args], out_specs=[full(o) for o in out_shape], out_shape=out_shape,
        compiler_params=_cparams(("arbitrary",)), name="rglru_sample")(*args)


HEADS_PER_TILE = LANES // HEAD_DIM


def _chunk_project(load, w):
    feat = jnp.concatenate([load(s) for s in range(STRIDE)], axis=1).astype(BF16)
    return _dot(feat, w)


def _compress_prompt_kernel(xa_ref, xb_ref, w1_ref, b1_ref, w2_ref, o_ref, *, ncp):
    rows = lax.broadcasted_iota(jnp.int32, (ncp, HEAD_DIM), 0)
    for c, x_ref in enumerate((xa_ref, xb_ref)):
        p2 = _chunk_project(lambda s: x_ref[0, pl.ds(s, ncp, stride=STRIDE), :], w1_ref[0])
        for hh in range(HEADS_PER_TILE):
            k = c * HEADS_PER_TILE + hh
            p = p2[:, hh * 2 * CMP_HIDDEN:(hh + 1) * 2 * CMP_HIDDEN]
            pre = b1_ref[0] + p[:, :CMP_HIDDEN] + pltpu.roll(p[:, CMP_HIDDEN:], ncp - 1, 0)
            phi = _dot(jax.nn.gelu(pre).astype(BF16), w2_ref[0])
            o_ref[0, 0, k, 0:ncp, :] = jnp.zeros((ncp, HEAD_DIM), F32)
            o_ref[0, 0, k, ncp:2 * ncp, :] = jnp.where(rows < ncp - 1, phi, 0.0)


def _compress_prompt(kv, w1f, b1, w2):
    b, s, _ = kv.shape
    ncp = s // STRIDE
    return pl.pallas_call(
        functools.partial(_compress_prompt_kernel, ncp=ncp), grid=(b, 2),
        in_specs=[pl.BlockSpec((1, s, LANES), lambda bi, sl: (bi, 0, 2 * sl)),
                  pl.BlockSpec((1, s, LANES), lambda bi, sl: (bi, 0, 2 * sl + 1)),
                  pl.BlockSpec((1,) + w1f.shape[1:], lambda bi, sl: (sl, 0, 0)),
                  pl.BlockSpec((1, 1, CMP_HIDDEN), lambda bi, sl: (sl, 0, 0)),
                  pl.BlockSpec((1, CMP_HIDDEN, HEAD_DIM), lambda bi, sl: (sl, 0, 0))],
        out_specs=pl.BlockSpec((1, 1, KV_HEADS, 2 * ncp, HEAD_DIM), lambda bi, sl: (sl, bi, 0, 0, 0)),
        out_shape=jax.ShapeDtypeStruct((2, b, KV_HEADS, 2 * ncp, HEAD_DIM), F32),
        compiler_params=_cparams(("parallel", "parallel")), name="compress_prompt")(kv, kv, w1f, b1, w2)


def _compress_sample_kernel(pt_ref, *refs, pps, nsteps):
    pages = refs[:pps]
    new_ref, w1_ref, b1_ref, w2_ref, o_ref, xs, pscr = refs[pps:]
    st = pl.program_id(1)
    cps = PAGE_SIZE // STRIDE
    m = pps * cps
    ntok = nsteps * m
    tiles_per_slot = KV_DIM // LANES

    def stage(ref, j, transposed):
        for c in range(2 * tiles_per_slot):
            if transposed:
                blk = ref[0, c * LANES:(c + 1) * LANES, :].T
            else:
                blk = ref[0, :, c * LANES:(c + 1) * LANES]
            xs[c, j * PAGE_SIZE:(j + 1) * PAGE_SIZE, :] = blk

    def project(nrows, row0):
        for sl in range(2):
            for c in range(tiles_per_slot):
                p2 = _chunk_project(lambda s: xs[sl * tiles_per_slot + c, pl.ds(s, nrows, stride=STRIDE), :],
                                    w1_ref[sl])
                for hh in range(HEADS_PER_TILE):
                    pscr[sl, c * HEADS_PER_TILE + hh, pl.ds(row0, nrows), :] = (
                        p2[:, hh * 2 * CMP_HIDDEN:(hh + 1) * 2 * CMP_HIDDEN])

    for j, pg in enumerate(pages):
        stage(pg, j, True)
    project(m, pl.multiple_of(st * m, m))

    @pl.when(st == nsteps - 1)
    def _():
        stage(new_ref, 0, False)
        project(cps, ntok)
        for sl in range(2):
            toks = []
            for k in range(KV_HEADS):
                pre = (b1_ref[sl] + pscr[sl, k, 0:ntok, 0:CMP_HIDDEN]
                       + pscr[sl, k, 1:ntok + 1, CMP_HIDDEN:2 * CMP_HIDDEN])
                toks.append(_dot(jax.nn.gelu(pre).astype(BF16), w2_ref[sl]))
            o_ref[0, sl] = jnp.concatenate(toks, axis=1)


def _compress_sample(cache, page_table, new_rows, w1f, b1, w2, pps=16):
    db, npg = page_table.shape
    pps = min(pps, npg)
    nsteps = npg // pps
    cps = PAGE_SIZE // STRIDE
    ntok = npg * cps

    def page_spec(j):
        return pl.BlockSpec((1, 2 * KV_DIM, PAGE_SIZE), lambda bi, st, pt: (pt[bi, st * pps + j], 0, 0))

    const = lambda shape: pl.BlockSpec(shape, lambda bi, st, pt: (0,) * len(shape))
    grid_spec = pltpu.PrefetchScalarGridSpec(
        num_scalar_prefetch=1, grid=(db, nsteps),
        in_specs=[page_spec(j) for j in range(pps)] + [
            pl.BlockSpec((1, PAGE_SIZE, 2 * KV_DIM), lambda bi, st, pt: (bi, 0, 0)),
            const(w1f.shape), const((2, 1, CMP_HIDDEN)),
            const((2, CMP_HIDDEN, HEAD_DIM))],
        out_specs=pl.BlockSpec((1, 2, ntok, KV_DIM), lambda bi, st, pt: (bi, 0, 0, 0)),
        scratch_shapes=[pltpu.VMEM((2 * KV_DIM // LANES, pps * PAGE_SIZE, LANES), F32),
                        pltpu.VMEM((2, KV_HEADS, ntok + cps, 2 * CMP_HIDDEN), F32)])
    return pl.pallas_call(
        functools.partial(_compress_sample_kernel, pps=pps, nsteps=nsteps), grid_spec=grid_spec,
        out_shape=jax.ShapeDtypeStruct((db, 2, ntok, KV_DIM), F32),
        compiler_params=_cparams(("parallel", "arbitrary")), name="compress_sample",
    )(page_table, *([cache] * pps), new_rows, w1f, b1, w2)


def _topk_mask(score, axis):
    n = score.shape[axis]
    idx = lax.broadcasted_iota(jnp.int32, score.shape, axis).astype(F32)

    def body(_, carry):
        sc, sel = carry
        mx = jnp.max(sc, axis=axis, keepdims=True)
        first = jnp.min(jnp.where(sc == mx, idx, float(n)), axis=axis, keepdims=True)
        hit = idx == first
        return jnp.where(hit, KNOCKED, sc), jnp.where(hit, 1.0, sel)

    return lax.fori_loop(0, N_SEL, body, (score, jnp.zeros_like(score)))[1]


def _softmax2(s, axis):
    m = jnp.max(s, axis=axis, keepdims=True)
    e = jnp.where(s > 0.5 * NEG, jnp.exp2(s - m), 0.0)
    l = jnp.sum(e, axis=axis, keepdims=True)
    return e * (1.0 / jnp.maximum(l, 1e-30))


def _nsa_prompt_kernel(qT_ref, gT_ref, kc_ref, vc_ref, ksel_ref, vselT_ref, kwin_ref, vwinT_ref,
                       tc_ref, tiles_ref, near_ref, crow_ref, ov_ref, y_ref,
                       qk_s, madd_s, oc_s, m_s, acc_s, s_scr, p_scr, *, ncp, nsb):
    i = pl.program_id(1)
    qT = qT_ref[0]
    for k in range(KV_HEADS):
        qk_s[k] = (jnp.concatenate([qT[(GROUP * k + g) * HEAD_DIM:(GROUP * k + g + 1) * HEAD_DIM, :]
                                    for g in range(GROUP)], axis=1) * QK_SCALE).astype(BF16)
    m_s[...] = jnp.full(m_s.shape, NEG, F32)
    acc_s[...] = jnp.zeros_like(acc_s)

    start = pl.multiple_of(8 * i + 8, 8)
    tok_ok = lax.broadcasted_iota(jnp.int32, (ncp, 1), 0) >= ncp - 8 - 8 * i
    rblk = lax.broadcasted_iota(jnp.int32, (nsb, QT), 0)
    qhalf = jnp.where(lax.broadcasted_iota(jnp.int32, (nsb, QT), 1) >= L_SEL, 1, 0)
    r_qb = nsb - 2 + qhalf
    exists = rblk >= nsb - 2 - 2 * i
    forced = (rblk == nsb - 2 - 2 * i) | (rblk > r_qb - N_LOCAL)
    for k in range(KV_HEADS):
        kc = kc_ref[0, 0, k, pl.ds(start, ncp), :].astype(BF16)
        vc = vc_ref[0, 0, k, pl.ds(start, ncp), :].astype(BF16)
        s = jnp.where(tok_ok, _dot(kc, qk_s[k]) + tc_ref[k], NEG)
        pn = _softmax2(s, 0)
        oc_s[k] = _dot_tn(vc, pn.astype(BF16))
        psum = pn[:, 0:QT]
        for g in range(1, GROUP):
            psum = psum + pn[:, g * QT:(g + 1) * QT]
        imp = _dot_exact_rhs(ov_ref[...], psum)
        score = jnp.where(rblk > r_qb, -FORCED, jnp.where(forced, FORCED, imp))
        score = jnp.where(exists, score, ABSENT)
        madd_s[k] = (_topk_mask(score, 0) - 1.0) * (-NEG)

    bpt = QT // L_SEL

    def logits(t, n, k_ref, slot):
        kt = k_ref[0, t] if n == 1 else jnp.concatenate([k_ref[0, t + j] for j in range(n)], axis=0)
        for k in range(KV_HEADS):
            s_scr[slot, k, 0:n * QT, :] = _dot(kt[:, k * HEAD_DIM:(k + 1) * HEAD_DIM], qk_s[k])

    def softmax_pv(t, n, vT_ref, add_of, branch, slot):
        nk = n * QT
        vt = vT_ref[0, t] if n == 1 else jnp.concatenate([vT_ref[0, t + j] for j in range(n)], axis=1)
        ones = jnp.ones((ONES_ROWS, nk), BF16)
        alphas = []
        for k in range(KV_HEADS):
            idx = branch * KV_HEADS + k
            s = s_scr[slot, k, 0:nk, :] + add_of(k)
            m_old = m_s[idx]
            m_new = jnp.maximum(m_old, jnp.max(s, axis=0, keepdims=True))
            alphas.append(jnp.exp2(m_old - m_new))
            m_s[idx] = m_new
            p_scr[slot, k, 0:nk, :] = jnp.exp2((s - m_new).astype(BF16))
        for k in range(KV_HEADS):
            idx = branch * KV_HEADS + k
            vt_aug = jnp.concatenate([vt[k * HEAD_DIM:(k + 1) * HEAD_DIM, :], ones], axis=0)
            acc_s[idx] = alphas[k] * acc_s[idx] + _dot(vt_aug, p_scr[slot, k, 0:nk, :])

    def attend(t, n, k_ref, vT_ref, add_of, branch):
        logits(t, n, k_ref, 0)
        softmax_pv(t, n, vT_ref, add_of, branch, 0)

    def sel_rows(k, t, n):
        r0 = bpt * (t - i) + nsb - bpt
        return [madd_s[k, pl.ds(r0 + j, 1), :] for j in range(bpt * n)]

    def far_add(t, n, masked):
        def add_of(k):
            c = crow_ref[k]
            if not masked:
                return c
            return jnp.concatenate(
                [jnp.concatenate([jnp.broadcast_to(r + c[:, g * QT:(g + 1) * QT], (L_SEL, QT)) for g in range(GROUP)],
                                 axis=1) for r in sel_rows(k, t, n)], axis=0)
        return add_of

    def table_add(table_of, t, n, masked):
        def add_of(k):
            if not masked:
                return table_of(k)
            mt = jnp.concatenate([jnp.broadcast_to(r, (L_SEL, QT)) for r in sel_rows(k, t, n)], axis=0)
            return table_of(k) + jnp.concatenate([mt] * GROUP, axis=1)
        return add_of

    nfar = jnp.maximum(i - 1, 0)
    npair = nfar // 2

    nquad = npair // 2

    @pl.when(nquad > 0)
    def _():
        logits(0, 2, ksel_ref, 0)

    def far_body(u, c):
        t0 = 4 * u
        logits(t0 + 2, 2, ksel_ref, 1)
        softmax_pv(t0, 2, vselT_ref, far_add(t0, 2, True), 0, 0)
        logits(jnp.minimum(t0 + 4, 4 * (nquad - 1)), 2, ksel_ref, 0)
        softmax_pv(t0 + 2, 2, vselT_ref, far_add(t0 + 2, 2, True), 0, 1)
        return c

    lax.fori_loop(0, nquad, far_body, 0)

    @pl.when(npair % 2 == 1)
    def _():
        attend(2 * (npair - 1), 2, ksel_ref, vselT_ref, far_add(2 * (npair - 1), 2, True), 0)

    @pl.when(nfar % 2 == 1)
    def _():
        attend(nfar - 1, 1, ksel_ref, vselT_ref, far_add(nfar - 1, 1, True), 0)

    @pl.when(i >= 4)
    def _():
        attend(i - 4, 1, kwin_ref, vwinT_ref, table_add(lambda k: tiles_ref[1, k], i - 4, 1, False), 1)

    @pl.when(i >= 3)
    def _():
        attend(i - 3, 2, kwin_ref, vwinT_ref, far_add(i - 3, 2, False), 1)

    @pl.when(i == 2)
    def _():
        attend(0, 1, kwin_ref, vwinT_ref, far_add(0, 1, False), 1)

    @pl.when(i >= 1)
    def _():
        attend(i - 1, 2, ksel_ref, vselT_ref, table_add(lambda k: near_ref[k], i - 1, 2, True), 0)
        attend(i - 1, 2, kwin_ref, vwinT_ref, table_add(lambda k: near_ref[k], i - 1, 2, False), 1)

    @pl.when(i == 0)
    def _():
        attend(0, 1, ksel_ref, vselT_ref, table_add(lambda k: tiles_ref[0, k], 0, 1, True), 0)
        attend(0, 1, kwin_ref, vwinT_ref, table_add(lambda k: tiles_ref[0, k], 0, 1, False), 1)

    gate = jax.nn.sigmoid(gT_ref[0])
    for k in range(KV_HEADS):
        o_s = acc_s[k, 0:HEAD_DIM] * (1.0 / acc_s[k, HEAD_DIM:HEAD_DIM + 1])
        o_w = acc_s[KV_HEADS + k, 0:HEAD_DIM] * (1.0 / acc_s[KV_HEADS + k, HEAD_DIM:HEAD_DIM + 1])
        o_c = oc_s[k]
        for g in range(GROUP):
            h = GROUP * k + g
            cols = slice(g * QT, (g + 1) * QT)
            y_ref[0, h * HEAD_DIM:(h + 1) * HEAD_DIM, :] = (
                gate[3 * h:3 * h + 1] * o_c[:, cols] + gate[3 * h + 1:3 * h + 2] * o_s[:, cols]
                + gate[3 * h + 2:3 * h + 3] * o_w[:, cols])


def _nsa_prompt(qT, gT, kcvc, ksel, vselT, kwin, vwinT, tc, tiles, near, crow, ov):
    b, _, s = qT.shape
    nq = s // QT
    ncp = s // STRIDE
    nsb = s // L_SEL
    seq4 = lambda a: pl.BlockSpec((1,) + a.shape[1:], lambda bi, qi: (bi, 0, 0, 0), pipeline_mode=pl.Buffered(1))
    const = lambda a: pl.BlockSpec(a.shape, lambda bi, qi: (0,) * a.ndim, pipeline_mode=pl.Buffered(1))
    cmp_spec = lambda sl: pl.BlockSpec((1, 1, KV_HEADS, 2 * ncp, HEAD_DIM), lambda bi, qi: (sl, bi, 0, 0, 0),
                                       pipeline_mode=pl.Buffered(1))
    return pl.pallas_call(
        functools.partial(_nsa_prompt_kernel, ncp=ncp, nsb=nsb), grid=(b, nq),
        in_specs=[pl.BlockSpec((1, Q_DIM, QT), lambda bi, qi: (bi, 0, qi)),
                  pl.BlockSpec((1, 3 * N_HEADS, QT), lambda bi, qi: (bi, 0, qi)),
                  cmp_spec(0), cmp_spec(1), seq4(ksel), seq4(vselT), seq4(kwin), seq4(vwinT),
                  const(tc), const(tiles), const(near), const(crow), const(ov)],
        out_specs=pl.BlockSpec((1, Q_DIM, QT), lambda bi, qi: (bi, 0, qi)),
        out_shape=jax.ShapeDtypeStruct((b, Q_DIM, s), F32),
        scratch_shapes=[pltpu.VMEM((KV_HEADS, HEAD_DIM, GQ), BF16), pltpu.VMEM((KV_HEADS, nsb, QT), F32),
                        pltpu.VMEM((KV_HEADS, HEAD_DIM, GQ), F32), pltpu.VMEM((2 * KV_HEADS, 1, GQ), F32),
                        pltpu.VMEM((2 * KV_HEADS, HEAD_DIM + ONES_ROWS, GQ), F32),
                        pltpu.VMEM((2, KV_HEADS, 2 * QT, GQ), F32), pltpu.VMEM((2, KV_HEADS, 2 * QT, GQ), BF16)],
        compiler_params=_cparams(("parallel", "arbitrary")), name="nsa_prompt",
    )(qT, gT, kcvc, kcvc, ksel, vselT, kwin, vwinT, tc, tiles, near, crow, ov)


def _nsa_sample_kernel(pt_ref, *refs, pps, nsteps, nsb):
    pages = refs[:pps]
    (qbd_ref, gate_ref, kcvc_ref, win_ref, new_ref, bc_ref, bw_ref, bstep_ref, bnew_ref, ovt_ref, gg_ref, exp_ref,
     o_ref, madd_s, oc_s, ow_s, m_s, l_s, acc_s) = refs[pps:]
    st = pl.program_id(1)
    qbd = qbd_ref[0]
    ncol = qbd.shape[0]
    wb = win_ref.shape[2]
    bps = pps * (PAGE_SIZE // L_SEL)
    nlane = ovt_ref.shape[1]

    def online(s, pv_of):
        m_old = m_s[...]
        m_new = jnp.maximum(m_old, jnp.max(s, axis=1, keepdims=True))
        alpha = jnp.exp2(m_old - m_new)
        p = jnp.exp2(s - m_new)
        l_s[...] = alpha * l_s[...] + jnp.sum(p, axis=1, keepdims=True)
        acc_s[...] = alpha * acc_s[...] + pv_of(p.astype(BF16))
        m_s[...] = m_new

    @pl.when(st == 0)
    def _():
        m_s[...] = jnp.full(m_s.shape, NEG, F32)
        l_s[...] = jnp.zeros_like(l_s)
        acc_s[...] = jnp.zeros_like(acc_s)
        pn = _softmax2(_dot_nt(qbd, kcvc_ref[0, 0].astype(BF16)) + bc_ref[...], 1)
        oc_s[...] = _dot(pn.astype(BF16), kcvc_ref[0, 1].astype(BF16))
        imp = sum(_dot_tn(part, gg_ref[...]) for part in _split3(_dot_exact_lhs(pn, ovt_ref[...])))
        nrow = -(-nsb // 8) * 8
        blk = lax.broadcasted_iota(jnp.int32, (nrow, ncol), 0)
        qb = nsb - 1
        forced = (blk == 0) | (blk > qb - N_LOCAL)
        score = jnp.where(blk > qb, -FORCED, jnp.where(forced, FORCED, imp[0:nrow]))
        score = jnp.where(blk < nsb, score, ABSENT)
        madd = jnp.concatenate([jnp.where(_topk_mask(score, 0) > 0.0, 0.0, NEG),
                                jnp.full((nlane - nrow, ncol), NEG, F32)], axis=0).astype(BF16)
        for tl in range(nlane // LANES):
            madd_s[tl] = madd[tl * LANES:(tl + 1) * LANES, :]
        new = new_ref[0]
        s_w = jnp.concatenate([_dot(qbd, win_ref[0, 0:KV_DIM, :].astype(BF16)),
                               _dot_nt(qbd, new[:, 2 * KV_DIM:3 * KV_DIM].astype(BF16))], axis=1) + bw_ref[...]
        pw = _softmax2(s_w, 1).astype(BF16)
        ow_s[...] = (_dot_nt(pw[:, 0:wb], win_ref[0, KV_DIM:2 * KV_DIM, :].astype(BF16))
                     + _dot(pw[:, wb:], new[:, 3 * KV_DIM:4 * KV_DIM].astype(BF16)))

    kt = jnp.concatenate([pg[0, 0:KV_DIM, :] for pg in pages], axis=1).astype(BF16)
    vt = jnp.concatenate([pg[0, KV_DIM:2 * KV_DIM, :] for pg in pages], axis=1).astype(BF16)
    b0 = st * bps
    mask = _dot_tn(madd_s[b0 // LANES], exp_ref[(b0 % LANES) // bps])
    bias = jnp.where(st == nsteps - 1, bstep_ref[1], bstep_ref[0])
    online(_dot(qbd, kt) + mask + bias, lambda p: _dot_nt(p, vt))

    @pl.when(st == nsteps - 1)
    def _():
        new = new_ref[0]
        lb = nsb - 1
        spread = (lax.broadcasted_iota(jnp.int32, (LANES, PAGE_SIZE), 0) == lb % LANES).astype(BF16)
        online(_dot_nt(qbd, new[:, 0:KV_DIM].astype(BF16)) + bnew_ref[...] + _dot_tn(madd_s[lb // LANES], spread),
               lambda p: _dot(p, new[:, KV_DIM:2 * KV_DIM].astype(BF16)))
        gate = jax.nn.sigmoid(gate_ref[0])
        o_ref[0] = (gate[:, 0:1] * oc_s[...] + gate[:, 1:2] * (acc_s[...] * (1.0 / l_s[...]))
                    + gate[:, 2:3] * ow_s[...])


def _nsa_sample(cache, page_table, qbd, gates, kcvc, win, new_rows, bc, bw, bstep, bnew, ovt, gg, expand, pps):
    db, npg = page_table.shape
    nsteps = npg // pps
    nsb = npg * (PAGE_SIZE // L_SEL) + 1
    ncol = qbd.shape[1]
    nlane = ovt.shape[1]

    def page_spec(j):
        return pl.BlockSpec((1, 2 * KV_DIM, PAGE_SIZE), lambda bi, st, pt: (pt[bi, st * pps + j], 1, 0))

    const = lambda a: pl.BlockSpec(a.shape, lambda bi, st, pt: (0,) * a.ndim)
    seq = lambda a: pl.BlockSpec((1,) + a.shape[1:], lambda bi, st, pt: (bi,) + (0,) * (a.ndim - 1))
    grid_spec = pltpu.PrefetchScalarGridSpec(
        num_scalar_prefetch=1, grid=(db, nsteps),
        in_specs=[page_spec(j) for j in range(pps)] + [seq(qbd), seq(gates), seq(kcvc), seq(win), seq(new_rows)]
        + [const(a) for a in (bc, bw, bstep, bnew, ovt, gg, expand)],
        out_specs=pl.BlockSpec((1, ncol, KV_DIM), lambda bi, st, pt: (bi, 0, 0)),
        scratch_shapes=[pltpu.VMEM((nlane // LANES, LANES, ncol), BF16), pltpu.VMEM((ncol, KV_DIM), F32),
                        pltpu.VMEM((ncol, KV_DIM), F32), pltpu.VMEM((ncol, 1), F32), pltpu.VMEM((ncol, 1), F32),
                        pltpu.VMEM((ncol, KV_DIM), F32)])
    return pl.pallas_call(
        functools.partial(_nsa_sample_kernel, pps=pps, nsteps=nsteps, nsb=nsb), grid_spec=grid_spec,
        out_shape=jax.ShapeDtypeStruct((db, ncol, KV_DIM), F32),
        compiler_params=_cparams(("parallel", "arbitrary")), name="nsa_sample",
    )(page_table, *([cache] * pps), qbd, gates, kcvc, win, new_rows, bc, bw, bstep, bnew, ovt, gg, expand)


def _merge_kernel(x_ref, ylru_ref, ynsa_ref, mg_ref, wl_ref, wn_ref, wo_ref, gf_ref, wr_ref, br_ref,
                  x2_ref, hn_ref, wt_ref):
    gate = jax.nn.sigmoid(mg_ref[...])
    mixed = (gate[:, :D_MODEL] * _dot(ylru_ref[...].astype(BF16), wl_ref[...])
             + gate[:, D_MODEL:] * _dot(ynsa_ref[...].astype(BF16), wn_ref[...]))
    x2 = x_ref[...] + _dot(mixed.astype(BF16), wo_ref[...])
    x2_ref[...] = x2
    hn = x2 * lax.rsqrt(jnp.mean(x2 * x2, axis=-1, keepdims=True) + EPS) * gf_ref[...]
    hn_ref[...] = hn.astype(BF16)
    logits = jnp.dot(hn, wr_ref[...], precision=lax.Precision.HIGHEST, preferred_element_type=F32) + br_ref[...]
    lane = lax.broadcasted_iota(jnp.int32, logits.shape, 1)
    lanef = lane.astype(F32)
    big = float(LANES)
    gl = jnp.where(lane < N_GROUPS, logits, NEG)
    gmax = jnp.max(gl, axis=-1, keepdims=True)
    grp = jnp.min(jnp.where(gl == gmax, lanef, big), axis=-1, keepdims=True)
    p_grp = 1.0 / jnp.sum(jnp.where(lane < N_GROUPS, jnp.exp(gl - gmax), 0.0), axis=-1, keepdims=True)
    lo = N_GROUPS + grp * EXPERTS_PER_GROUP
    el = jnp.where((lanef >= lo) & (lanef < lo + EXPERTS_PER_GROUP), logits, NEG)
    v1 = jnp.max(el, axis=-1, keepdims=True)
    i1 = jnp.min(jnp.where(el == v1, lanef, big), axis=-1, keepdims=True)
    el2 = jnp.where(lanef == i1, NEG, el)
    v2 = jnp.max(el2, axis=-1, keepdims=True)
    i2 = jnp.min(jnp.where(el2 == v2, lanef, big), axis=-1, keepdims=True)
    e2 = jnp.exp(v2 - v1)
    den = 1.0 / (1.0 + e2)
    wt_ref[...] = jnp.where(lanef == i1, den * p_grp, jnp.where(lanef == i2, e2 * den * p_grp, 0.0))


def _merge(x2d, ylru, ynsa, mg, wl, wn, wo, gf, wr, br, tm=256):
    t = x2d.shape[0]
    tm = min(tm, t)
    tile = lambda a: pl.BlockSpec((tm, a.shape[1]), lambda i: (i, 0))
    const = lambda a: pl.BlockSpec(a.shape, lambda i: (0, 0), pipeline_mode=pl.Buffered(1))
    return pl.pallas_call(
        _merge_kernel, grid=(t // tm,),
        in_specs=[tile(x2d), tile(ylru), tile(ynsa), tile(mg)] + [const(a) for a in (wl, wn, wo, gf, wr, br)],
        out_specs=[pl.BlockSpec((tm, D_MODEL), lambda i: (i, 0)), pl.BlockSpec((tm, D_MODEL), lambda i: (i, 0)),
                   pl.BlockSpec((tm, LANES), lambda i: (i, 0))],
        out_shape=[jax.ShapeDtypeStruct((t, D_MODEL), F32), jax.ShapeDtypeStruct((t, D_MODEL), BF16),
                   jax.ShapeDtypeStruct((t, LANES), F32)],
        compiler_params=_cparams(("parallel",)), name="merge")(x2d, ylru, ynsa, mg, wl, wn, wo, gf, wr, br)


def _moe_kernel(hn_ref, wt_ref, x2_ref, wg_ref, wu_ref, wd_ref, gfin_ref, y_ref, acc):
    c = pl.program_id(1)

    @pl.when(c == 0)
    def _():
        acc[...] = jnp.zeros_like(acc)

    h = hn_ref[...]
    wt = wt_ref[...]
    lane = lax.broadcasted_iota(jnp.int32, wt.shape, 1)
    total = acc[...]
    for e in range(EXPERTS_PER_GROUP):
        act = jax.nn.silu(_dot(h, wg_ref[e])) * _dot(h, wu_ref[e])
        w_e = jnp.sum(jnp.where(lane == N_GROUPS + c * EXPERTS_PER_GROUP + e, wt, 0.0), axis=-1, keepdims=True)
        act = jnp.where(w_e != 0.0, act * w_e, 0.0)
        total = total + _dot(act.astype(BF16), wd_ref[e])
    acc[...] = total

    @pl.when(c == N_GROUPS - 1)
    def _():
        x = x2_ref[...] + total
        y_ref[...] = x * lax.rsqrt(jnp.mean(x * x, axis=-1, keepdims=True) + EPS) * gfin_ref[...]


def _moe(hn, wt, x2, wg, wu, wd, gfin, tm=512):
    t = hn.shape[0]
    tm = min(tm, t)
    return pl.pallas_call(
        _moe_kernel, grid=(t // tm, N_GROUPS),
        in_specs=[pl.BlockSpec((tm, D_MODEL), lambda i, c: (i, 0)), pl.BlockSpec((tm, LANES), lambda i, c: (i, 0)),
                  pl.BlockSpec((tm, D_MODEL), lambda i, c: (i, 0)),
                  pl.BlockSpec((EXPERTS_PER_GROUP, D_MODEL, D_EXPERT), lambda i, c: (c, 0, 0)),
                  pl.BlockSpec((EXPERTS_PER_GROUP, D_MODEL, D_EXPERT), lambda i, c: (c, 0, 0)),
                  pl.BlockSpec((EXPERTS_PER_GROUP, D_EXPERT, D_MODEL), lambda i, c: (c, 0, 0)),
                  pl.BlockSpec((1, D_MODEL), lambda i, c: (0, 0))],
        out_specs=pl.BlockSpec((tm, D_MODEL), lambda i, c: (i, 0)),
        out_shape=jax.ShapeDtypeStruct((t, D_MODEL), F32),
        scratch_shapes=[pltpu.VMEM((tm, D_MODEL), F32)],
        compiler_params=_cparams(("parallel", "arbitrary")), name="moe")(hn, wt, x2, wg, wu, wd, gfin)


def _bucket_table():
    n = np.arange(MAX_DISTANCE + 1)
    exact = N_BUCKETS // 2
    nf = np.maximum(n, exact).astype(np.float64)
    large = exact + (np.log(nf / exact) / math.log(MAX_DISTANCE / exact) * (N_BUCKETS - exact)).astype(np.int32)
    return np.where(n < exact, n, np.minimum(large, N_BUCKETS - 1))


def _bias_of(rel_b, rel):
    idx = jnp.asarray(np.clip(rel, 0, MAX_DISTANCE).reshape(-1, 1), jnp.int32)
    onehot = (idx == jnp.arange(MAX_DISTANCE + 1, dtype=jnp.int32)[None, :]).astype(F32)
    t = jnp.dot(onehot, rel_b * LOG2E, precision=lax.Precision.HIGHEST, preferred_element_type=F32)
    return t.reshape(rel.shape + (N_HEADS,))


def _prompt_tile(rel_b, rel, mask):
    r = rel.shape[0]
    t = _bias_of(rel_b, rel)
    t = jnp.where(jnp.asarray(mask)[..., None], t, NEG)
    return t.reshape(r, QT, KV_HEADS, GROUP).transpose(2, 0, 3, 1).reshape(KV_HEADS, r, GQ)


def _sample_tile(rel_b, rel, mask):
    r, ds = rel.shape
    t = _bias_of(rel_b, rel)
    t = jnp.where(jnp.asarray(mask)[..., None], t, NEG)
    return t.transpose(2, 1, 0).reshape(N_HEADS * ds, r)


def _block_diag(w):
    eye = jnp.eye(LRU_HEADS, dtype=w.dtype)
    return jnp.einsum('hij,hk->hikj', w, eye).reshape(D_LRU, D_LRU)


def kernel(x_prompt, x_sample, cache_kv, state_kv_win, state_conv, state_h, page_table, g_mix, w_in, conv_w, conv_b,
           w_gate_a, b_gate_a, w_gate_x, b_gate_x, lru_lambda, cmp_w1, cmp_b1, cmp_w2, w_lru_out, w_nsa_out, w_out,
           g_ffn, w_router_group, b_router_group, w_router_expert, b_router_expert, w_exp_gate, w_exp_up,
           w_exp_down, rel_bias, g_final):
    assert w_in.shape[0] == 1, "single layer"
    b, s, _ = x_prompt.shape
    db, ds, _ = x_sample.shape
    npg = page_table.shape[1]
    past = npg * PAGE_SIZE
    wb = state_kv_win.shape[2]
    assert s % 256 == 0 and s // L_SEL >= N_SEL and CONV_W - 1 <= ds <= STRIDE and wb == WINDOW and past >= WINDOW

    w = w_in[0].astype(BF16)
    o = 0
    ws = []
    for width in (D_LRU, D_LRU, Q_DIM, 4 * KV_DIM, 2 * KV_DIM, 3 * N_HEADS, 2 * D_MODEL):
        ws.append(w[:, o:o + width])
        o += width
    ws[5] = jnp.pad(ws[5], ((0, 0), (0, LANES - 3 * N_HEADS)))
    g_mix2 = g_mix[0][None]
    wg = jnp.concatenate([_block_diag(w_gate_x[0]), _block_diag(w_gate_a[0])], axis=1).astype(BF16)
    lru_args = (conv_w[0], conv_b[0][None], wg, b_gate_x[0][None], b_gate_a[0][None], lru_lambda[0][None])
    w1 = cmp_w1[0].reshape(2, 2, STRIDE, HEAD_DIM, CMP_HIDDEN)
    w1f = jnp.einsum('armdn,hk->amhdkrn', w1, jnp.eye(HEADS_PER_TILE, dtype=F32)).reshape(
        2, STRIDE * LANES, HEADS_PER_TILE * 2 * CMP_HIDDEN).astype(BF16)
    b1 = cmp_b1[0][:, None, :]
    w2 = cmp_w2[0].astype(BF16)
    wl, wn, wo = w_lru_out[0].astype(BF16), w_nsa_out[0].astype(BF16), w_out[0].astype(BF16)
    n_r = N_GROUPS + N_EXPERTS
    wr = jnp.pad(jnp.concatenate([w_router_group[0], w_router_expert[0]], axis=1), ((0, 0), (0, LANES - n_r)))
    br = jnp.pad(jnp.concatenate([b_router_group[0], b_router_expert[0]]), (0, LANES - n_r))[None]
    weg, weu, wed = w_exp_gate[0].astype(BF16), w_exp_up[0].astype(BF16), w_exp_down[0].astype(BF16)
    gf, gfin = g_ffn[0][None], g_final[None]
    rel_b = rel_bias.astype(F32)[_bucket_table()]

    def mix_ffn(x2d, ylru, ynsa, mg):
        x2, hn, wt = _merge(x2d, ylru, ynsa, mg, wl, wn, wo, gf, wr, br)
        return _moe(hn, wt, x2, weg, weu, wed, gfin)

    xp2 = x_prompt.reshape(b * s, D_MODEL)
    lx, lg, q, kv, kvw, ng, mg = _proj(xp2, g_mix2, ws)
    y_lru, conv_p, h_p = _rglru_prompt(lx.reshape(b, s, D_LRU), lg.reshape(b, s, D_LRU), *lru_args)
    kv3 = kv.reshape(b, s, 4 * KV_DIM)
    kvw3 = kvw.reshape(b, s, 2 * KV_DIM)
    kcvc = _compress_prompt(kv3, w1f, b1, w2)
    nt = s // QT
    ncp = s // STRIDE
    nsb = s // L_SEL

    def key_tiles(x):
        return x.astype(BF16).reshape(b, nt, QT, KV_DIM)

    def val_tiles(x):
        return x.astype(BF16).reshape(b, nt, QT, KV_DIM).swapaxes(2, 3)

    qi = np.arange(QT)[None, :]
    lrow = np.arange(ncp)[:, None]
    rel_c = qi - STRIDE * lrow + STRIDE * ncp - (STRIDE * 8 + L_CMP - 1)
    n_far = int(np.sum(np.all(rel_c >= MAX_DISTANCE, axis=1)))
    crow = _prompt_tile(rel_b, np.full((1, QT), MAX_DISTANCE), np.ones((1, QT), bool))
    tc = jnp.concatenate([jnp.broadcast_to(crow, (KV_HEADS, n_far, GQ)),
                          _prompt_tile(rel_b, rel_c[n_far:], rel_c[n_far:] >= 0)], axis=1)
    kj = np.arange(QT)[:, None]
    diag = _prompt_tile(rel_b, qi - kj, qi - kj >= 0)
    tiles = jnp.stack([diag, _prompt_tile(rel_b, WINDOW + qi - kj, qi - kj < 0)])
    near = jnp.concatenate([_prompt_tile(rel_b, QT + qi - kj, np.ones((QT, QT), bool)), diag], axis=1)
    rr = np.arange(nsb)[:, None]
    ll = np.arange(ncp)[None, :]
    ov = jnp.asarray((ll >= 4 * rr - 1) & (ll <= 4 * rr + 3), BF16)
    qT = q.reshape(b, s, Q_DIM).swapaxes(1, 2)
    gT = ng[:, :3 * N_HEADS].reshape(b, s, 3 * N_HEADS).swapaxes(1, 2)
    y_nsaT = _nsa_prompt(qT, gT, kcvc, key_tiles(kv3[..., 2 * KV_DIM:3 * KV_DIM]),
                         val_tiles(kv3[..., 3 * KV_DIM:]), key_tiles(kvw3[..., :KV_DIM]),
                         val_tiles(kvw3[..., KV_DIM:]), tc, tiles, near, crow, ov)
    y_nsa = y_nsaT.swapaxes(1, 2).reshape(b * s, Q_DIM)
    y_prompt = mix_ffn(xp2, y_lru.reshape(b * s, D_LRU), y_nsa, mg).reshape(b, s, D_MODEL)
    kv_rows_prompt = kv3.reshape(1, b, s, 4, KV_HEADS, HEAD_DIM)
    win_prompt = kvw3[:, s - min(WINDOW, s):].reshape(1, b, min(WINDOW, s), 2, KV_HEADS, HEAD_DIM)

    xs2 = x_sample.reshape(db * ds, D_MODEL)
    lx, lg, q, kv, kvw, ng, mg = _proj(xs2, g_mix2, ws)
    tmaj = lambda a: a.reshape(db, ds, D_LRU).swapaxes(0, 1)
    y_lru_t, conv_t, h_s = _rglru_sample(tmaj(lx), tmaj(lg), state_conv[0].swapaxes(0, 1), state_h[0], *lru_args)
    y_lru = y_lru_t.swapaxes(0, 1).reshape(db * ds, D_LRU)
    kv_s3 = kv.reshape(db, ds, 4 * KV_DIM)
    kvw_s3 = kvw.reshape(db, ds, 2 * KV_DIM)
    cache_t = cache_kv[0].transpose(0, 2, 3, 4, 1).reshape(cache_kv.shape[1], 4 * KV_DIM, PAGE_SIZE)
    win_t = state_kv_win[0].transpose(0, 2, 3, 4, 1).reshape(db, 2 * KV_DIM, wb)
    new_c = jnp.pad(kv_s3[..., :2 * KV_DIM], ((0, 0), (0, PAGE_SIZE - ds), (0, 0)))
    pps = 32
    assert npg % pps == 0 and LANES % (pps * PAGE_SIZE // L_SEL) == 0
    kcvc_s = _compress_sample(cache_t, page_table, new_c, w1f, b1, w2, pps)

    nsb_s = past // L_SEL + 1
    ntok = past // STRIDE
    jq = np.arange(ds)[None, :]
    nrow = np.arange(ntok)[:, None]
    rel = past + jq - STRIDE * nrow - (L_CMP - 1)
    bc = _sample_tile(rel_b, rel, rel >= 0)
    rn = np.arange(PAGE_SIZE)[:, None]
    new_rel, new_ok = jq - rn, (jq - rn >= 0) & (rn < ds)
    rw = np.arange(wb)[:, None]
    bw = _sample_tile(rel_b, np.concatenate([wb + jq - rw, new_rel], axis=0),
                      np.concatenate([(wb + jq - rw < WINDOW) & (past - wb + rw >= 0), new_ok], axis=0))
    bnew = _sample_tile(rel_b, new_rel, new_ok)
    step_keys = pps * PAGE_SIZE
    far_rel = np.full((step_keys, ds), MAX_DISTANCE)
    last_rel = far_rel.copy()
    last_rel[step_keys - PAGE_SIZE:] = PAGE_SIZE + jq - rn
    all_ok = np.ones((step_keys, ds), bool)
    bstep = jnp.stack([_sample_tile(rel_b, far_rel, all_ok), _sample_tile(rel_b, last_rel, all_ok)])
    nlane = -(-nsb_s // LANES) * LANES
    bb = np.arange(nlane)[None, :]
    tt = np.arange(ntok)[:, None]
    ovt = jnp.asarray((tt >= 4 * bb - 1) & (tt <= 4 * bb + 3) & (bb < nsb_s), BF16)
    ncol = N_HEADS * ds
    col = np.arange(ncol)
    kvh_c, j_c = col // (GROUP * ds), col % ds
    gg = jnp.asarray((kvh_c[:, None] == kvh_c[None, :]) & (j_c[:, None] == j_c[None, :]), BF16)
    bps = pps * PAGE_SIZE // L_SEL
    off = np.arange(LANES // bps)[:, None, None]
    expand = jnp.asarray(np.arange(LANES)[None, :, None] == bps * off + np.arange(step_keys)[None, None, :] // L_SEL,
                         BF16)
    q5 = q.reshape(db, ds, KV_HEADS, GROUP, HEAD_DIM) * QK_SCALE
    qbd = jnp.einsum('bjkgd,kc->bkgjcd', q5, jnp.eye(KV_HEADS, dtype=F32)).reshape(db, ncol, KV_DIM).astype(BF16)
    gates_s = ng[:, :3 * N_HEADS].reshape(db, ds, N_HEADS, 3).transpose(0, 2, 1, 3).reshape(db, ncol, 3)
    new_all = jnp.pad(jnp.concatenate([kv_s3[..., 2 * KV_DIM:], kvw_s3], axis=-1),
                      ((0, 0), (0, PAGE_SIZE - ds), (0, 0)))
    o_bd = _nsa_sample(cache_t, page_table, qbd, gates_s, kcvc_s, win_t, new_all, bc, bw, bstep, bnew, ovt, gg,
                       expand, pps)
    o6 = o_bd.reshape(db, KV_HEADS, GROUP, ds, KV_HEADS, HEAD_DIM)
    y_nsa = jnp.einsum('bkgjkd->bjkgd', o6).reshape(db * ds, Q_DIM)
    y_sample = mix_ffn(xs2, y_lru, y_nsa, mg).reshape(db, ds, D_MODEL)
    kv_rows_sample = kv_s3.reshape(1, db, ds, 4, KV_HEADS, HEAD_DIM)
    win_all = jnp.concatenate([state_kv_win[0].reshape(db, wb, 2 * KV_DIM), kvw_s3], axis=1)
    win_sample = win_all[:, win_all.shape[1] - WINDOW:].reshape(1, db, WINDOW, 2, KV_HEADS, HEAD_DIM)
    conv_s = conv_t.swapaxes(0, 1)

    return (y_prompt, y_sample, kv_rows_prompt, win_prompt, conv_p[None], h_p.reshape(1, b, D_LRU),
            kv_rows_sample, win_sample, conv_s[None], h_s[None])
```

```python
import functools
import math

import numpy as np
import jax
import jax.numpy as jnp
from jax import lax
from jax.experimental import pallas as pl
from jax.experimental.pallas import tpu as pltpu

F32 = jnp.float32
BF16 = jnp.bfloat16

D_MODEL = 1024
D_LRU = 1280
LRU_HEADS = 16
LRU_BLOCK = D_LRU // LRU_HEADS
CONV_W = 4
LRU_C = 8.0
N_HEADS = 16
HEAD_DIM = 64
KV_HEADS = 4
GROUP = N_HEADS // KV_HEADS
L_CMP = 32
STRIDE = 16
CMP_HIDDEN = 128
L_SEL = 64
N_SEL = 16
N_LOCAL = 2
WINDOW = 512
PAGE_SIZE = 128
N_BUCKETS = 32
MAX_DISTANCE = 128
N_GROUPS = 4
EXPERTS_PER_GROUP = 8
N_EXPERTS = N_GROUPS * EXPERTS_PER_GROUP
D_EXPERT = 256
EPS = 1e-6
NEG = -1e30
FORCED = 1e9
Q_DIM = N_HEADS * HEAD_DIM
KV_DIM = KV_HEADS * HEAD_DIM
QT = 128
GQ = GROUP * QT
LANES = 128
SUB = 8
VMEM_LIMIT = 56 * 1024 * 1024
KNOCKED = -3e38
ABSENT = -2e38
LOG2E = math.log2(math.e)
QK_SCALE = HEAD_DIM ** -0.5 * LOG2E
MASK_ROWS = 16
ONES_ROWS = 16


def _cparams(sem):
    return pltpu.CompilerParams(dimension_semantics=sem, vmem_limit_bytes=VMEM_LIMIT)


def _dot(a, b):
    return jnp.dot(a, b, preferred_element_type=F32)


def _dot_tn(a, b):
    return lax.dot_general(a, b, (((0,), (0,)), ((), ())), preferred_element_type=F32)


def _dot_nt(a, b):
    return lax.dot_general(a, b, (((1,), (1,)), ((), ())), preferred_element_type=F32)


def _split3(x):
    hi = x.astype(BF16)
    r1 = x - hi.astype(F32)
    mid = r1.astype(BF16)
    lo = (r1 - mid.astype(F32)).astype(BF16)
    return hi, mid, lo


def _dot_exact_rhs(a_bf16, x):
    hi, mid, lo = _split3(x)
    return _dot(a_bf16, hi) + _dot(a_bf16, mid) + _dot(a_bf16, lo)


def _dot_exact_lhs(x, b_bf16):
    hi, mid, lo = _split3(x)
    return _dot(hi, b_bf16) + _dot(mid, b_bf16) + _dot(lo, b_bf16)


def _proj_kernel(x_ref, g_ref, *refs):
    n = len(refs) // 2
    x = x_ref[...]
    xn = x * lax.rsqrt(jnp.mean(x * x, axis=-1, keepdims=True) + EPS) * g_ref[...]
    xb = xn.astype(BF16)
    for w_ref, o_ref in zip(refs[:n], refs[n:]):
        o_ref[...] = _dot(xb, w_ref[...])


def _proj(x2d, g, ws, tm=256):
    t = x2d.shape[0]
    tm = min(tm, t)
    in_specs = [pl.BlockSpec((tm, D_MODEL), lambda i: (i, 0)), pl.BlockSpec((1, D_MODEL), lambda i: (0, 0))]
    in_specs += [pl.BlockSpec(w.shape, lambda i: (0, 0), pipeline_mode=pl.Buffered(1)) for w in ws]
    out_specs = [pl.BlockSpec((tm, w.shape[1]), lambda i: (i, 0)) for w in ws]
    out_shape = [jax.ShapeDtypeStruct((t, w.shape[1]), F32) for w in ws]
    return pl.pallas_call(_proj_kernel, grid=(t // tm,), in_specs=in_specs, out_specs=out_specs,
                          out_shape=out_shape, compiler_params=_cparams(("parallel",)), name="proj")(x2d, g, *ws)


def _softplus(x):
    return jnp.maximum(x, 0.0) + jnp.log1p(jnp.exp(-jnp.abs(x)))


def _lru_gates(xc, wg_ref, bgx_ref, bga_ref, lam_ref):
    gates = _dot(xc.astype(BF16), wg_ref[...])
    gx = jax.nn.sigmoid(gates[:, :D_LRU] + bgx_ref[...])
    ga = jax.nn.sigmoid(gates[:, D_LRU:] + bga_ref[...])
    log_a = -LRU_C * ga * _softplus(-lam_ref[...])
    a = jnp.exp(log_a)
    th = jnp.tanh(log_a)
    u = jnp.sqrt(-2.0 * th / (1.0 - th)) * (gx * xc)
    return a, u


def _rglru_prompt_kernel(lx_ref, lg_ref, cw_ref, cb_ref, wg_ref, bgx_ref, bga_ref, lam_ref,
                         y_ref, conv_ref, h_ref, xext, hc, *, tc):
    t = pl.program_id(1)

    @pl.when(t == 0)
    def _():
        xext[0:8, :] = jnp.zeros((8, D_LRU), F32)
        hc[...] = jnp.zeros_like(hc)

    x = lx_ref[0]
    xext[8:8 + tc, :] = x
    cw = cw_ref[...]
    xc = cb_ref[...] + cw[3:4] * x
    for j in range(CONV_W - 1):
        xc = xc + cw[j:j + 1] * xext[5 + j:5 + j + tc, :]
    a, u = _lru_gates(xc, wg_ref, bgx_ref, bga_ref, lam_ref)
    sub = lax.broadcasted_iota(jnp.int32, (tc, D_LRU), 0) % SUB
    s = 1
    while s < SUB:
        a_sh = pltpu.roll(a, s, 0)
        u_sh = pltpu.roll(u, s, 0)
        m = sub >= s
        u = jnp.where(m, a * u_sh + u, u)
        a = jnp.where(m, a * a_sh, a)
        s *= 2
    carry = hc[0:1, :]
    lg = lg_ref[0]
    for g in range(tc // SUB):
        rows = slice(g * SUB, (g + 1) * SUB)
        h = a[rows] * carry + u[rows]
        carry = h[SUB - 1:SUB, :]
        y_ref[0, rows, :] = h * jax.nn.gelu(lg[rows])
    hc[0:1, :] = carry
    xext[0:8, :] = x[tc - 8:tc, :]
    conv_ref[0] = x[tc - (CONV_W - 1):tc, :]
    h_ref[0] = carry


def _rglru_prompt(lx, lg, cw, cb, wg, bgx, bga, lam, tc=256):
    b, s, _ = lx.shape
    row = lambda shape: pl.BlockSpec(shape, lambda bi, ti: (0, 0))
    return pl.pallas_call(
        functools.partial(_rglru_prompt_kernel, tc=tc), grid=(b, s // tc),
        in_specs=[pl.BlockSpec((1, tc, D_LRU), lambda bi, ti: (bi, ti, 0)),
                  pl.BlockSpec((1, tc, D_LRU), lambda bi, ti: (bi, ti, 0)),
                  row((CONV_W, D_LRU)), row((1, D_LRU)), row((D_LRU, 2 * D_LRU)),
                  row((1, D_LRU)), row((1, D_LRU)), row((1, D_LRU))],
        out_specs=[pl.BlockSpec((1, tc, D_LRU), lambda bi, ti: (bi, ti, 0)),
                   pl.BlockSpec((1, CONV_W - 1, D_LRU), lambda bi, ti: (bi, 0, 0)),
                   pl.BlockSpec((1, 1, D_LRU), lambda bi, ti: (bi, 0, 0))],
        out_shape=[jax.ShapeDtypeStruct((b, s, D_LRU), F32),
                   jax.ShapeDtypeStruct((b, CONV_W - 1, D_LRU), F32),
                   jax.ShapeDtypeStruct((b, 1, D_LRU), F32)],
        scratch_shapes=[pltpu.VMEM((tc + 8, D_LRU), F32), pltpu.VMEM((8, D_LRU), F32)],
        compiler_params=_cparams(("parallel", "arbitrary")), name="rglru_prompt",
    )(lx, lg, cw, cb, wg, bgx, bga, lam)


def _rglru_sample_kernel(lx_ref, lg_ref, cbuf_ref, h0_ref, cw_ref, cb_ref, wg_ref, bgx_ref, bga_ref, lam_ref,
                         y_ref, conv_ref, h_ref, *, ds):
    cw = cw_ref[...]
    xp = [cbuf_ref[j] for j in range(CONV_W - 1)] + [lx_ref[j] for j in range(ds)]
    h = h0_ref[...]
    for t in range(ds):
        xc = cb_ref[...]
        for j in range(CONV_W):
            xc = xc + cw[j:j + 1] * xp[t + j]
        a, u = _lru_gates(xc, wg_ref, bgx_ref, bga_ref, lam_ref)
        h = a * h + u
        y_ref[t] = h * jax.nn.gelu(lg_ref[t])
    for j in range(CONV_W - 1):
        conv_ref[j] = xp[ds + j]
    h_ref[...] = h


def _rglru_sample(lx_t, lg_t, cbuf_t, h0, cw, cb, wg, bgx, bga, lam):
    ds, n, _ = lx_t.shape
    full = lambda a: pl.BlockSpec(a.shape, lambda i: (0,) * a.ndim)
    args = (lx_t, lg_t, cbuf_t, h0, cw, cb, wg, bgx, bga, lam)
    out_shape = [jax.ShapeDtypeStruct((ds, n, D_LRU), F32), jax.ShapeDtypeStruct((CONV_W - 1, n, D_LRU), F32),
                 jax.ShapeDtypeStruct((n, D_LRU), F32)]
    return pl.pallas_call(
        functools.partial(_rglru_sample_kernel, ds=ds), grid=(1,),
        in_specs=[full(a) for a in args], out_specs=[full(o) for o in out_shape], out_shape=out_shape,
        compiler_params=_cparams(("arbitrary",)), name="rglru_sample")(*args)


HEADS_PER_TILE = LANES // HEAD_DIM


def _chunk_project(load, w):
    feat = jnp.concatenate([load(s) for s in range(STRIDE)], axis=1).astype(BF16)
    return _dot(feat, w)


def _compress_prompt_kernel(xa_ref, xb_ref, w1_ref, b1_ref, w2_ref, o_ref, *, ncp):
    rows = lax.broadcasted_iota(jnp.int32, (ncp, HEAD_DIM), 0)
    for c, x_ref in enumerate((xa_ref, xb_ref)):
        p2 = _chunk_project(lambda s: x_ref[0, pl.ds(s, ncp, stride=STRIDE), :], w1_ref[0])
        for hh in range(HEADS_PER_TILE):
            k = c * HEADS_PER_TILE + hh
            p = p2[:, hh * 2 * CMP_HIDDEN:(hh + 1) * 2 * CMP_HIDDEN]
            pre = b1_ref[0] + p[:, :CMP_HIDDEN] + pltpu.roll(p[:, CMP_HIDDEN:], ncp - 1, 0)
            phi = _dot(jax.nn.gelu(pre).astype(BF16), w2_ref[0])
            o_ref[0, 0, k, 0:ncp, :] = jnp.zeros((ncp, HEAD_DIM), F32)
            o_ref[0, 0, k, ncp:2 * ncp, :] = jnp.where(rows < ncp - 1, phi, 0.0)


def _compress_prompt(kv, w1f, b1, w2):
    b, s, _ = kv.shape
    ncp = s // STRIDE
    return pl.pallas_call(
        functools.partial(_compress_prompt_kernel, ncp=ncp), grid=(b, 2),
        in_specs=[pl.BlockSpec((1, s, LANES), lambda bi, sl: (bi, 0, 2 * sl)),
                  pl.BlockSpec((1, s, LANES), lambda bi, sl: (bi, 0, 2 * sl + 1)),
                  pl.BlockSpec((1,) + w1f.shape[1:], lambda bi, sl: (sl, 0, 0)),
                  pl.BlockSpec((1, 1, CMP_HIDDEN), lambda bi, sl: (sl, 0, 0)),
                  pl.BlockSpec((1, CMP_HIDDEN, HEAD_DIM), lambda bi, sl: (sl, 0, 0))],
        out_specs=pl.BlockSpec((1, 1, KV_HEADS, 2 * ncp, HEAD_DIM), lambda bi, sl: (sl, bi, 0, 0, 0)),
        out_shape=jax.ShapeDtypeStruct((2, b, KV_HEADS, 2 * ncp, HEAD_DIM), F32),
        compiler_params=_cparams(("parallel", "parallel")), name="compress_prompt")(kv, kv, w1f, b1, w2)


def _compress_sample_kernel(pt_ref, *refs, pps, nsteps):
    pages = refs[:pps]
    new_ref, w1_ref, b1_ref, w2_ref, o_ref, xs, pscr = refs[pps:]
    st = pl.program_id(1)
    cps = PAGE_SIZE // STRIDE
    m = pps * cps
    ntok = nsteps * m
    tiles_per_slot = KV_DIM // LANES

    def stage(ref, j, transposed):
        for c in range(2 * tiles_per_slot):
            if transposed:
                blk = ref[0, c * LANES:(c + 1) * LANES, :].T
            else:
                blk = ref[0, :, c * LANES:(c + 1) * LANES]
            xs[c, j * PAGE_SIZE:(j + 1) * PAGE_SIZE, :] = blk

    def project(nrows, row0):
        for sl in range(2):
            for c in range(tiles_per_slot):
                p2 = _chunk_project(lambda s: xs[sl * tiles_per_slot + c, pl.ds(s, nrows, stride=STRIDE), :],
                                    w1_ref[sl])
                for hh in range(HEADS_PER_TILE):
                    pscr[sl, c * HEADS_PER_TILE + hh, pl.ds(row0, nrows), :] = (
                        p2[:, hh * 2 * CMP_HIDDEN:(hh + 1) * 2 * CMP_HIDDEN])

    for j, pg in enumerate(pages):
        stage(pg, j, True)
    project(m, pl.multiple_of(st * m, m))

    @pl.when(st == nsteps - 1)
    def _():
        stage(new_ref, 0, False)
        project(cps, ntok)
        for sl in range(2):
            toks = []
            for k in range(KV_HEADS):
                pre = (b1_ref[sl] + pscr[sl, k, 0:ntok, 0:CMP_HIDDEN]
                       + pscr[sl, k, 1:ntok + 1, CMP_HIDDEN:2 * CMP_HIDDEN])
                toks.append(_dot(jax.nn.gelu(pre).astype(BF16), w2_ref[sl]))
            o_ref[0, sl] = jnp.concatenate(toks, axis=1)


def _compress_sample(cache, page_table, new_rows, w1f, b1, w2, pps=16):
    db, npg = page_table.shape
    pps = min(pps, npg)
    nsteps = npg // pps
    cps = PAGE_SIZE // STRIDE
    ntok = npg * cps

    def page_spec(j):
        return pl.BlockSpec((1, 2 * KV_DIM, PAGE_SIZE), lambda bi, st, pt: (pt[bi, st * pps + j], 0, 0))

    const = lambda shape: pl.BlockSpec(shape, lambda bi, st, pt: (0,) * len(shape))
    grid_spec = pltpu.PrefetchScalarGridSpec(
        num_scalar_prefetch=1, grid=(db, nsteps),
        in_specs=[page_spec(j) for j in range(pps)] + [
            pl.BlockSpec((1, PAGE_SIZE, 2 * KV_DIM), lambda bi, st, pt: (bi, 0, 0)),
            const(w1f.shape), const((2, 1, CMP_HIDDEN)),
            const((2, CMP_HIDDEN, HEAD_DIM))],
        out_specs=pl.BlockSpec((1, 2, ntok, KV_DIM), lambda bi, st, pt: (bi, 0, 0, 0)),
        scratch_shapes=[pltpu.VMEM((2 * KV_DIM // LANES, pps * PAGE_SIZE, LANES), F32),
                        pltpu.VMEM((2, KV_HEADS, ntok + cps, 2 * CMP_HIDDEN), F32)])
    return pl.pallas_call(
        functools.partial(_compress_sample_kernel, pps=pps, nsteps=nsteps), grid_spec=grid_spec,
        out_shape=jax.ShapeDtypeStruct((db, 2, ntok, KV_DIM), F32),
        compiler_params=_cparams(("parallel", "arbitrary")), name="compress_sample",
    )(page_table, *([cache] * pps), new_rows, w1f, b1, w2)


def _topk_mask(score, axis):
    n = score.shape[axis]
    idx = lax.broadcasted_iota(jnp.int32, score.shape, axis).astype(F32)

    def body(_, carry):
        sc, sel = carry
        mx = jnp.max(sc, axis=axis, keepdims=True)
        first = jnp.min(jnp.where(sc == mx, idx, float(n)), axis=axis, keepdims=True)
        hit = idx == first
        return jnp.where(hit, KNOCKED, sc), jnp.where(hit, 1.0, sel)

    return lax.fori_loop(0, N_SEL, body, (score, jnp.zeros_like(score)))[1]


def _softmax2(s, axis):
    m = jnp.max(s, axis=axis, keepdims=True)
    e = jnp.where(s > 0.5 * NEG, jnp.exp2(s - m), 0.0)
    l = jnp.sum(e, axis=axis, keepdims=True)
    return e * (1.0 / jnp.maximum(l, 1e-30))


def _nsa_prompt_kernel(qT_ref, gT_ref, kc_ref, vc_ref, ksel_ref, vselT_ref, kwin_ref, vwinT_ref,
                       tc_ref, tiles_ref, near_ref, ov_ref, y_ref,
                       qk_s, madd_s, oc_s, m_s, acc_s, s_scr, p_scr, alpha_s, *, ncp, nsb):
    i = pl.program_id(1)
    qT = qT_ref[0]
    for k in range(KV_HEADS):
        qk_s[k, 0:HEAD_DIM, :] = (jnp.concatenate([qT[(GROUP * k + g) * HEAD_DIM:(GROUP * k + g + 1) * HEAD_DIM, :]
                                                   for g in range(GROUP)], axis=1) * QK_SCALE).astype(BF16)
        qk_s[k, HEAD_DIM:LANES, :] = jnp.zeros((LANES - HEAD_DIM, GQ), BF16)
    m_s[...] = jnp.full(m_s.shape, NEG, F32)
    acc_s[...] = jnp.zeros_like(acc_s)

    start = pl.multiple_of(8 * i + 8, 8)
    tok_ok = lax.broadcasted_iota(jnp.int32, (ncp, 1), 0) >= ncp - 8 - 8 * i
    rblk = lax.broadcasted_iota(jnp.int32, (nsb, QT), 0)
    qhalf = jnp.where(lax.broadcasted_iota(jnp.int32, (nsb, QT), 1) >= L_SEL, 1, 0)
    r_qb = nsb - 2 + qhalf
    exists = rblk >= nsb - 2 - 2 * i
    forced = (rblk == nsb - 2 - 2 * i) | (rblk > r_qb - N_LOCAL)
    for k in range(KV_HEADS):
        kc = kc_ref[0, 0, k, pl.ds(start, ncp), :].astype(BF16)
        vc = vc_ref[0, 0, k, pl.ds(start, ncp), :].astype(BF16)
        s = jnp.where(tok_ok, _dot(kc, qk_s[k, 0:HEAD_DIM, :]) + tc_ref[k], NEG)
        pn = _softmax2(s, 0)
        oc_s[k] = _dot_tn(vc, pn.astype(BF16))
        psum = pn[:, 0:QT]
        for g in range(1, GROUP):
            psum = psum + pn[:, g * QT:(g + 1) * QT]
        imp = _dot_exact_rhs(ov_ref[...], psum)
        score = jnp.where(rblk > r_qb, -FORCED, jnp.where(forced, FORCED, imp))
        score = jnp.where(exists, score, ABSENT)
        madd_s[k] = (_topk_mask(score, 0) - 1.0) * (-NEG)

    bpt = QT // L_SEL

    def logits(t, n, k_ref, slot, masked):
        kt = k_ref[0, t] if n == 1 else jnp.concatenate([k_ref[0, t + j] for j in range(n)], axis=0)
        for k in range(KV_HEADS):
            if masked:
                r0 = bpt * (t - i) + nsb - bpt
                par = t % 2
                first = [r0 + (bpt * par if n == 2 else 0) + h for h in range(bpt)]
                second = [r0 + (bpt * (1 - par) if n == 2 else 0) + h for h in range(bpt)]
                rows = [jnp.concatenate([madd_s[k, pl.ds(r, 1), :]] * GROUP, axis=1) for r in first + second]
                blk = jnp.concatenate(rows + [jnp.zeros((MASK_ROWS - 2 * bpt, GQ), F32)], axis=0)
                qk_s[k, HEAD_DIM:HEAD_DIM + MASK_ROWS, :] = blk.astype(BF16)
                s_scr[slot, k, 0:n * QT, :] = _dot(kt[:, k * LANES:(k + 1) * LANES], qk_s[k])
            else:
                s_scr[slot, k, 0:n * QT, :] = _dot(kt[:, k * HEAD_DIM:(k + 1) * HEAD_DIM], qk_s[k, 0:HEAD_DIM, :])

    def softmax(n, add_of, branch, slot):
        nk = n * QT
        for k in range(KV_HEADS):
            idx = branch * KV_HEADS + k
            s = s_scr[slot, k, 0:nk, :]
            if add_of is not None:
                s = s + add_of(k)
            m_old = m_s[idx]
            m_new = jnp.maximum(m_old, jnp.max(s, axis=0, keepdims=True))
            alpha_s[slot, k] = jnp.exp2(m_old - m_new)
            m_s[idx] = m_new
            p_scr[slot, k, 0:nk, :] = jnp.exp2((s - m_new).astype(BF16))

    def values(t, n, vT_ref, branch, slot):
        nk = n * QT
        vt = vT_ref[0, t] if n == 1 else jnp.concatenate([vT_ref[0, t + j] for j in range(n)], axis=1)
        ones = jnp.ones((ONES_ROWS, nk), BF16)
        for k in range(KV_HEADS):
            idx = branch * KV_HEADS + k
            vt_aug = jnp.concatenate([vt[k * HEAD_DIM:(k + 1) * HEAD_DIM, :], ones], axis=0)
            acc_s[idx] = alpha_s[slot, k] * acc_s[idx] + _dot(vt_aug, p_scr[slot, k, 0:nk, :])

    def attend(t, n, k_ref, vT_ref, add_of, branch):
        logits(t, n, k_ref, 0, branch == 0)
        softmax(n, add_of, branch, 0)
        values(t, n, vT_ref, branch, 0)

    nfar = jnp.maximum(i - 1, 0)
    npair = nfar // 2

    last_pair = 2 * (npair - 1)

    @pl.when(npair >= 2)
    def _():
        logits(0, 2, ksel_ref, 0, True)
        p_scr[1] = jnp.zeros(p_scr.shape[1:], BF16)
        alpha_s[1] = jnp.ones(alpha_s.shape[1:], F32)

    def far_body(u, c):
        ta = 4 * u
        logits(ta + 2, 2, ksel_ref, 1, True)
        values(jnp.maximum(ta - 2, 0), 2, vselT_ref, 0, 1)
        softmax(2, None, 0, 0)
        logits(jnp.minimum(ta + 4, last_pair), 2, ksel_ref, 0, True)
        values(ta, 2, vselT_ref, 0, 0)
        softmax(2, None, 0, 1)
        return c

    ntrip = npair // 2
    lax.fori_loop(0, ntrip, far_body, 0)

    @pl.when(ntrip > 0)
    def _():
        values(4 * ntrip - 2, 2, vselT_ref, 0, 1)

    @pl.when(npair % 2 == 1)
    def _():
        attend(last_pair, 2, ksel_ref, vselT_ref, None, 0)

    @pl.when(nfar % 2 == 1)
    def _():
        attend(nfar - 1, 1, ksel_ref, vselT_ref, None, 0)

    @pl.when(i >= 4)
    def _():
        attend(i - 4, 1, kwin_ref, vwinT_ref, lambda k: tiles_ref[1, k], 1)

    @pl.when(i >= 3)
    def _():
        attend(i - 3, 2, kwin_ref, vwinT_ref, None, 1)

    @pl.when(i == 2)
    def _():
        attend(0, 1, kwin_ref, vwinT_ref, None, 1)

    @pl.when(i >= 1)
    def _():
        attend(i - 1, 2, ksel_ref, vselT_ref, lambda k: near_ref[k], 0)
        attend(i - 1, 2, kwin_ref, vwinT_ref, lambda k: near_ref[k], 1)

    @pl.when(i == 0)
    def _():
        attend(0, 1, ksel_ref, vselT_ref, lambda k: tiles_ref[0, k], 0)
        attend(0, 1, kwin_ref, vwinT_ref, lambda k: tiles_ref[0, k], 1)

    gate = jax.nn.sigmoid(gT_ref[0])
    for k in range(KV_HEADS):
        o_s = acc_s[k, 0:HEAD_DIM] * (1.0 / acc_s[k, HEAD_DIM:HEAD_DIM + 1])
        o_w = acc_s[KV_HEADS + k, 0:HEAD_DIM] * (1.0 / acc_s[KV_HEADS + k, HEAD_DIM:HEAD_DIM + 1])
        o_c = oc_s[k]
        for g in range(GROUP):
            h = GROUP * k + g
            cols = slice(g * QT, (g + 1) * QT)
            y_ref[0, h * HEAD_DIM:(h + 1) * HEAD_DIM, :] = (
                gate[3 * h:3 * h + 1] * o_c[:, cols] + gate[3 * h + 1:3 * h + 2] * o_s[:, cols]
                + gate[3 * h + 2:3 * h + 3] * o_w[:, cols])


def _nsa_prompt(qT, gT, kcvc, ksel, vselT, kwin, vwinT, tc, tiles, near, ov):
    b, _, s = qT.shape
    nq = s // QT
    ncp = s // STRIDE
    nsb = s // L_SEL
    seq4 = lambda a: pl.BlockSpec((1,) + a.shape[1:], lambda bi, qi: (bi, 0, 0, 0), pipeline_mode=pl.Buffered(1))
    const = lambda a: pl.BlockSpec(a.shape, lambda bi, qi: (0,) * a.ndim, pipeline_mode=pl.Buffered(1))
    cmp_spec = lambda sl: pl.BlockSpec((1, 1, KV_HEADS, 2 * ncp, HEAD_DIM), lambda bi, qi: (sl, bi, 0, 0, 0),
                                       pipeline_mode=pl.Buffered(1))
    return pl.pallas_call(
        functools.partial(_nsa_prompt_kernel, ncp=ncp, nsb=nsb), grid=(b, nq),
        in_specs=[pl.BlockSpec((1, Q_DIM, QT), lambda bi, qi: (bi, 0, qi)),
                  pl.BlockSpec((1, 3 * N_HEADS, QT), lambda bi, qi: (bi, 0, qi)),
                  cmp_spec(0), cmp_spec(1), seq4(ksel), seq4(vselT), seq4(kwin), seq4(vwinT),
                  const(tc), const(tiles), const(near), const(ov)],
        out_specs=pl.BlockSpec((1, Q_DIM, QT), lambda bi, qi: (bi, 0, qi)),
        out_shape=jax.ShapeDtypeStruct((b, Q_DIM, s), F32),
        scratch_shapes=[pltpu.VMEM((KV_HEADS, LANES, GQ), BF16), pltpu.VMEM((KV_HEADS, nsb, QT), F32),
                        pltpu.VMEM((KV_HEADS, HEAD_DIM, GQ), F32), pltpu.VMEM((2 * KV_HEADS, 1, GQ), F32),
                        pltpu.VMEM((2 * KV_HEADS, HEAD_DIM + ONES_ROWS, GQ), F32),
                        pltpu.VMEM((2, KV_HEADS, 2 * QT, GQ), F32), pltpu.VMEM((2, KV_HEADS, 2 * QT, GQ), BF16),
                        pltpu.VMEM((2, KV_HEADS, 1, GQ), F32)],
        compiler_params=_cparams(("parallel", "arbitrary")), name="nsa_prompt",
    )(qT, gT, kcvc, kcvc, ksel, vselT, kwin, vwinT, tc, tiles, near, ov)


def _nsa_sample_kernel(pt_ref, *refs, pps, nsteps, nsb):
    pages = refs[:pps]
    (qbd_ref, gate_ref, kcvc_ref, win_ref, new_ref, bc_ref, bw_ref, bstep_ref, bnew_ref, ovt_ref, gg_ref, exp_ref,
     o_ref, madd_s, oc_s, ow_s, m_s, l_s, acc_s) = refs[pps:]
    st = pl.program_id(1)
    qbd = qbd_ref[0]
    ncol = qbd.shape[0]
    wb = win_ref.shape[2]
    bps = pps * (PAGE_SIZE // L_SEL)
    nlane = ovt_ref.shape[1]

    def online(s, pv_of):
        m_old = m_s[...]
        m_new = jnp.maximum(m_old, jnp.max(s, axis=1, keepdims=True))
        alpha = jnp.exp2(m_old - m_new)
        p = jnp.exp2(s - m_new)
        l_s[...] = alpha * l_s[...] + jnp.sum(p, axis=1, keepdims=True)
        acc_s[...] = alpha * acc_s[...] + pv_of(p.astype(BF16))
        m_s[...] = m_new

    @pl.when(st == 0)
    def _():
        m_s[...] = jnp.full(m_s.shape, NEG, F32)
        l_s[...] = jnp.zeros_like(l_s)
        acc_s[...] = jnp.zeros_like(acc_s)
        pn = _softmax2(_dot_nt(qbd, kcvc_ref[0, 0].astype(BF16)) + bc_ref[...], 1)
        oc_s[...] = _dot(pn.astype(BF16), kcvc_ref[0, 1].astype(BF16))
        imp = sum(_dot_tn(part, gg_ref[...]) for part in _split3(_dot_exact_lhs(pn, ovt_ref[...])))
        nrow = -(-nsb // 8) * 8
        blk = lax.broadcasted_iota(jnp.int32, (nrow, ncol), 0)
        qb = nsb - 1
        forced = (blk == 0) | (blk > qb - N_LOCAL)
        score = jnp.where(blk > qb, -FORCED, jnp.where(forced, FORCED, imp[0:nrow]))
        score = jnp.where(blk < nsb, score, ABSENT)
        madd = jnp.concatenate([jnp.where(_topk_mask(score, 0) > 0.0, 0.0, NEG),
                                jnp.full((nlane - nrow, ncol), NEG, F32)], axis=0).astype(BF16)
        for tl in range(nlane // LANES):
            madd_s[tl] = madd[tl * LANES:(tl + 1) * LANES, :]
        new = new_ref[0]
        s_w = jnp.concatenate([_dot(qbd, win_ref[0, 0:KV_DIM, :].astype(BF16)),
                               _dot_nt(qbd, new[:, 2 * KV_DIM:3 * KV_DIM].astype(BF16))], axis=1) + bw_ref[...]
        pw = _softmax2(s_w, 1).astype(BF16)
        ow_s[...] = (_dot_nt(pw[:, 0:wb], win_ref[0, KV_DIM:2 * KV_DIM, :].astype(BF16))
                     + _dot(pw[:, wb:], new[:, 3 * KV_DIM:4 * KV_DIM].astype(BF16)))

    kt = jnp.concatenate([pg[0, 0:KV_DIM, :] for pg in pages], axis=1).astype(BF16)
    vt = jnp.concatenate([pg[0, KV_DIM:2 * KV_DIM, :] for pg in pages], axis=1).astype(BF16)
    b0 = st * bps
    mask = _dot_tn(madd_s[b0 // LANES], exp_ref[(b0 % LANES) // bps])
    bias = jnp.where(st == nsteps - 1, bstep_ref[1], bstep_ref[0])
    online(_dot(qbd, kt) + mask + bias, lambda p: _dot_nt(p, vt))

    @pl.when(st == nsteps - 1)
    def _():
        new = new_ref[0]
        lb = nsb - 1
        spread = (lax.broadcasted_iota(jnp.int32, (LANES, PAGE_SIZE), 0) == lb % LANES).astype(BF16)
        online(_dot_nt(qbd, new[:, 0:KV_DIM].astype(BF16)) + bnew_ref[...] + _dot_tn(madd_s[lb // LANES], spread),
               lambda p: _dot(p, new[:, KV_DIM:2 * KV_DIM].astype(BF16)))
        gate = jax.nn.sigmoid(gate_ref[0])
        o_ref[0] = (gate[:, 0:1] * oc_s[...] + gate[:, 1:2] * (acc_s[...] * (1.0 / l_s[...]))
                    + gate[:, 2:3] * ow_s[...])


def _nsa_sample(cache, page_table, qbd, gates, kcvc, win, new_rows, bc, bw, bstep, bnew, ovt, gg, expand, pps):
    db, npg = page_table.shape
    nsteps = npg // pps
    nsb = npg * (PAGE_SIZE // L_SEL) + 1
    ncol = qbd.shape[1]
    nlane = ovt.shape[1]

    def page_spec(j):
        return pl.BlockSpec((1, 2 * KV_DIM, PAGE_SIZE), lambda bi, st, pt: (pt[bi, st * pps + j], 1, 0))

    const = lambda a: pl.BlockSpec(a.shape, lambda bi, st, pt: (0,) * a.ndim)
    seq = lambda a: pl.BlockSpec((1,) + a.shape[1:], lambda bi, st, pt: (bi,) + (0,) * (a.ndim - 1))
    grid_spec = pltpu.PrefetchScalarGridSpec(
        num_scalar_prefetch=1, grid=(db, nsteps),
        in_specs=[page_spec(j) for j in range(pps)] + [seq(qbd), seq(gates), seq(kcvc), seq(win), seq(new_rows)]
        + [const(a) for a in (bc, bw, bstep, bnew, ovt, gg, expand)],
        out_specs=pl.BlockSpec((1, ncol, KV_DIM), lambda bi, st, pt: (bi, 0, 0)),
        scratch_shapes=[pltpu.VMEM((nlane // LANES, LANES, ncol), BF16), pltpu.VMEM((ncol, KV_DIM), F32),
                        pltpu.VMEM((ncol, KV_DIM), F32), pltpu.VMEM((ncol, 1), F32), pltpu.VMEM((ncol, 1), F32),
                        pltpu.VMEM((ncol, KV_DIM), F32)])
    return pl.pallas_call(
        functools.partial(_nsa_sample_kernel, pps=pps, nsteps=nsteps, nsb=nsb), grid_spec=grid_spec,
        out_shape=jax.ShapeDtypeStruct((db, ncol, KV_DIM), F32),
        compiler_params=_cparams(("parallel", "arbitrary")), name="nsa_sample",
    )(page_table, *([cache] * pps), qbd, gates, kcvc, win, new_rows, bc, bw, bstep, bnew, ovt, gg, expand)


def _merge_kernel(x_ref, ylru_ref, ynsa_ref, mg_ref, wl_ref, wn_ref, wo_ref, gf_ref, wr_ref, br_ref,
                  x2_ref, hn_ref, wt_ref):
    gate = jax.nn.sigmoid(mg_ref[...])
    mixed = (gate[:, :D_MODEL] * _dot(ylru_ref[...].astype(BF16), wl_ref[...])
             + gate[:, D_MODEL:] * _dot(ynsa_ref[...].astype(BF16), wn_ref[...]))
    x2 = x_ref[...] + _dot(mixed.astype(BF16), wo_ref[...])
    x2_ref[...] = x2
    hn = x2 * lax.rsqrt(jnp.mean(x2 * x2, axis=-1, keepdims=True) + EPS) * gf_ref[...]
    hn_ref[...] = hn.astype(BF16)
    logits = jnp.dot(hn, wr_ref[...], precision=lax.Precision.HIGHEST, preferred_element_type=F32) + br_ref[...]
    lane = lax.broadcasted_iota(jnp.int32, logits.shape, 1)
    lanef = lane.astype(F32)
    big = float(LANES)
    gl = jnp.where(lane < N_GROUPS, logits, NEG)
    gmax = jnp.max(gl, axis=-1, keepdims=True)
    grp = jnp.min(jnp.where(gl == gmax, lanef, big), axis=-1, keepdims=True)
    p_grp = 1.0 / jnp.sum(jnp.where(lane < N_GROUPS, jnp.exp(gl - gmax), 0.0), axis=-1, keepdims=True)
    lo = N_GROUPS + grp * EXPERTS_PER_GROUP
    el = jnp.where((lanef >= lo) & (lanef < lo + EXPERTS_PER_GROUP), logits, NEG)
    v1 = jnp.max(el, axis=-1, keepdims=True)
    i1 = jnp.min(jnp.where(el == v1, lanef, big), axis=-1, keepdims=True)
    el2 = jnp.where(lanef == i1, NEG, el)
    v2 = jnp.max(el2, axis=-1, keepdims=True)
    i2 = jnp.min(jnp.where(el2 == v2, lanef, big), axis=-1, keepdims=True)
    e2 = jnp.exp(v2 - v1)
    den = 1.0 / (1.0 + e2)
    wt_ref[...] = jnp.where(lanef == i1, den * p_grp, jnp.where(lanef == i2, e2 * den * p_grp, 0.0))


def _merge(x2d, ylru, ynsa, mg, wl, wn, wo, gf, wr, br, tm=256):
    t = x2d.shape[0]
    tm = min(tm, t)
    tile = lambda a: pl.BlockSpec((tm, a.shape[1]), lambda i: (i, 0))
    const = lambda a: pl.BlockSpec(a.shape, lambda i: (0, 0), pipeline_mode=pl.Buffered(1))
    return pl.pallas_call(
        _merge_kernel, grid=(t // tm,),
        in_specs=[tile(x2d), tile(ylru), tile(ynsa), tile(mg)] + [const(a) for a in (wl, wn, wo, gf, wr, br)],
        out_specs=[pl.BlockSpec((tm, D_MODEL), lambda i: (i, 0)), pl.BlockSpec((tm, D_MODEL), lambda i: (i, 0)),
                   pl.BlockSpec((tm, LANES), lambda i: (i, 0))],
        out_shape=[jax.ShapeDtypeStruct((t, D_MODEL), F32), jax.ShapeDtypeStruct((t, D_MODEL), BF16),
                   jax.ShapeDtypeStruct((t, LANES), F32)],
        compiler_params=_cparams(("parallel",)), name="merge")(x2d, ylru, ynsa, mg, wl, wn, wo, gf, wr, br)


def _moe_kernel(hn_ref, wt_ref, x2_ref, wg_ref, wu_ref, wd_ref, gfin_ref, y_ref, acc):
    c = pl.program_id(1)

    @pl.when(c == 0)
    def _():
        acc[...] = jnp.zeros_like(acc)

    h = hn_ref[...]
    wt = wt_ref[...]
    lane = lax.broadcasted_iota(jnp.int32, wt.shape, 1)
    total = acc[...]
    for e in range(EXPERTS_PER_GROUP):
        act = jax.nn.silu(_dot(h, wg_ref[e])) * _dot(h, wu_ref[e])
        w_e = jnp.sum(jnp.where(lane == N_GROUPS + c * EXPERTS_PER_GROUP + e, wt, 0.0), axis=-1, keepdims=True)
        act = jnp.where(w_e != 0.0, act * w_e, 0.0)
        total = total + _dot(act.astype(BF16), wd_ref[e])
    acc[...] = total

    @pl.when(c == N_GROUPS - 1)
    def _():
        x = x2_ref[...] + total
        y_ref[...] = x * lax.rsqrt(jnp.mean(x * x, axis=-1, keepdims=True) + EPS) * gfin_ref[...]


def _moe(hn, wt, x2, wg, wu, wd, gfin, tm=512):
    t = hn.shape[0]
    tm = min(tm, t)
    return pl.pallas_call(
        _moe_kernel, grid=(t // tm, N_GROUPS),
        in_specs=[pl.BlockSpec((tm, D_MODEL), lambda i, c: (i, 0)), pl.BlockSpec((tm, LANES), lambda i, c: (i, 0)),
                  pl.BlockSpec((tm, D_MODEL), lambda i, c: (i, 0)),
                  pl.BlockSpec((EXPERTS_PER_GROUP, D_MODEL, D_EXPERT), lambda i, c: (c, 0, 0)),
                  pl.BlockSpec((EXPERTS_PER_GROUP, D_MODEL, D_EXPERT), lambda i, c: (c, 0, 0)),
                  pl.BlockSpec((EXPERTS_PER_GROUP, D_EXPERT, D_MODEL), lambda i, c: (c, 0, 0)),
                  pl.BlockSpec((1, D_MODEL), lambda i, c: (0, 0))],
        out_specs=pl.BlockSpec((tm, D_MODEL), lambda i, c: (i, 0)),
        out_shape=jax.ShapeDtypeStruct((t, D_MODEL), F32),
        scratch_shapes=[pltpu.VMEM((tm, D_MODEL), F32)],
        compiler_params=_cparams(("parallel", "arbitrary")), name="moe")(hn, wt, x2, wg, wu, wd, gfin)


def _bucket_table():
    n = np.arange(MAX_DISTANCE + 1)
    exact = N_BUCKETS // 2
    nf = np.maximum(n, exact).astype(np.float64)
    large = exact + (np.log(nf / exact) / math.log(MAX_DISTANCE / exact) * (N_BUCKETS - exact)).astype(np.int32)
    return np.where(n < exact, n, np.minimum(large, N_BUCKETS - 1))


def _bias_of(rel_b, rel):
    idx = jnp.asarray(np.clip(rel, 0, MAX_DISTANCE).reshape(-1, 1), jnp.int32)
    onehot = (idx == jnp.arange(MAX_DISTANCE + 1, dtype=jnp.int32)[None, :]).astype(F32)
    t = jnp.dot(onehot, rel_b * LOG2E, precision=lax.Precision.HIGHEST, preferred_element_type=F32)
    return t.reshape(rel.shape + (N_HEADS,))


def _prompt_tile(rel_b, rel, mask, minus=None):
    r = rel.shape[0]
    t = _bias_of(rel_b, rel).reshape(r, QT, KV_HEADS, GROUP).transpose(2, 0, 3, 1).reshape(KV_HEADS, r, GQ)
    if minus is not None:
        t = t - minus
    m = np.broadcast_to(np.asarray(mask)[None, :, None, :], (KV_HEADS, r, GROUP, QT)).reshape(KV_HEADS, r, GQ)
    return jnp.where(jnp.asarray(m), t, NEG)


def _sample_tile(rel_b, rel, mask):
    r, ds = rel.shape
    t = _bias_of(rel_b, rel)
    t = jnp.where(jnp.asarray(mask)[..., None], t, NEG)
    return t.transpose(2, 1, 0).reshape(N_HEADS * ds, r)


def _block_diag(w):
    eye = jnp.eye(LRU_HEADS, dtype=w.dtype)
    return jnp.einsum('hij,hk->hikj', w, eye).reshape(D_LRU, D_LRU)


def kernel(x_prompt, x_sample, cache_kv, state_kv_win, state_conv, state_h, page_table, g_mix, w_in, conv_w, conv_b,
           w_gate_a, b_gate_a, w_gate_x, b_gate_x, lru_lambda, cmp_w1, cmp_b1, cmp_w2, w_lru_out, w_nsa_out, w_out,
           g_ffn, w_router_group, b_router_group, w_router_expert, b_router_expert, w_exp_gate, w_exp_up,
           w_exp_down, rel_bias, g_final):
    assert w_in.shape[0] == 1, "single layer"
    b, s, _ = x_prompt.shape
    db, ds, _ = x_sample.shape
    npg = page_table.shape[1]
    past = npg * PAGE_SIZE
    wb = state_kv_win.shape[2]
    assert s % 256 == 0 and s // L_SEL >= N_SEL and CONV_W - 1 <= ds <= STRIDE and wb == WINDOW and past >= WINDOW

    w = w_in[0].astype(BF16)
    o = 0
    ws = []
    for width in (D_LRU, D_LRU, Q_DIM, 4 * KV_DIM, 2 * KV_DIM, 3 * N_HEADS, 2 * D_MODEL):
        ws.append(w[:, o:o + width])
        o += width
    ws[5] = jnp.pad(ws[5], ((0, 0), (0, LANES - 3 * N_HEADS)))
    g_mix2 = g_mix[0][None]
    wg = jnp.concatenate([_block_diag(w_gate_x[0]), _block_diag(w_gate_a[0])], axis=1).astype(BF16)
    lru_args = (conv_w[0], conv_b[0][None], wg, b_gate_x[0][None], b_gate_a[0][None], lru_lambda[0][None])
    w1 = cmp_w1[0].reshape(2, 2, STRIDE, HEAD_DIM, CMP_HIDDEN)
    w1f = jnp.einsum('armdn,hk->amhdkrn', w1, jnp.eye(HEADS_PER_TILE, dtype=F32)).reshape(
        2, STRIDE * LANES, HEADS_PER_TILE * 2 * CMP_HIDDEN).astype(BF16)
    b1 = cmp_b1[0][:, None, :]
    w2 = cmp_w2[0].astype(BF16)
    wl, wn, wo = w_lru_out[0].astype(BF16), w_nsa_out[0].astype(BF16), w_out[0].astype(BF16)
    n_r = N_GROUPS + N_EXPERTS
    wr = jnp.pad(jnp.concatenate([w_router_group[0], w_router_expert[0]], axis=1), ((0, 0), (0, LANES - n_r)))
    br = jnp.pad(jnp.concatenate([b_router_group[0], b_router_expert[0]]), (0, LANES - n_r))[None]
    weg, weu, wed = w_exp_gate[0].astype(BF16), w_exp_up[0].astype(BF16), w_exp_down[0].astype(BF16)
    gf, gfin = g_ffn[0][None], g_final[None]
    rel_b = rel_bias.astype(F32)[_bucket_table()]

    def mix_ffn(x2d, ylru, ynsa, mg):
        x2, hn, wt = _merge(x2d, ylru, ynsa, mg, wl, wn, wo, gf, wr, br)
        return _moe(hn, wt, x2, weg, weu, wed, gfin)

    xp2 = x_prompt.reshape(b * s, D_MODEL)
    lx, lg, q, kv, kvw, ng, mg = _proj(xp2, g_mix2, ws)
    y_lru, conv_p, h_p = _rglru_prompt(lx.reshape(b, s, D_LRU), lg.reshape(b, s, D_LRU), *lru_args)
    kv3 = kv.reshape(b, s, 4 * KV_DIM)
    kvw3 = kvw.reshape(b, s, 2 * KV_DIM)
    kcvc = _compress_prompt(kv3, w1f, b1, w2)
    nt = s // QT
    ncp = s // STRIDE
    nsb = s // L_SEL

    def key_tiles(x):
        return x.astype(BF16).reshape(b, nt, QT, KV_DIM)

    def val_tiles(x):
        return x.astype(BF16).reshape(b, nt, QT, KV_DIM).swapaxes(2, 3)

    qi = np.arange(QT)[None, :]
    lrow = np.arange(ncp)[:, None]
    rel_c = qi - STRIDE * lrow + STRIDE * ncp - (STRIDE * 8 + L_CMP - 1)
    n_far = int(np.sum(np.all(rel_c >= MAX_DISTANCE, axis=1)))
    crow = _prompt_tile(rel_b, np.full((1, QT), MAX_DISTANCE), np.ones((1, QT), bool))
    tc = jnp.concatenate([jnp.broadcast_to(crow, (KV_HEADS, n_far, GQ)),
                          _prompt_tile(rel_b, rel_c[n_far:], rel_c[n_far:] >= 0)], axis=1)
    kj = np.arange(QT)[:, None]
    diag = _prompt_tile(rel_b, qi - kj, qi - kj >= 0, crow)
    tiles = jnp.stack([diag, _prompt_tile(rel_b, WINDOW + qi - kj, qi - kj < 0, crow)])
    near = jnp.concatenate([_prompt_tile(rel_b, QT + qi - kj, np.ones((QT, QT), bool), crow), diag], axis=1)
    rr = np.arange(nsb)[:, None]
    ll = np.arange(ncp)[None, :]
    ov = jnp.asarray((ll >= 4 * rr - 1) & (ll <= 4 * rr + 3), BF16)
    qT = q.reshape(b, s, Q_DIM).swapaxes(1, 2)
    gT = ng[:, :3 * N_HEADS].reshape(b, s, 3 * N_HEADS).swapaxes(1, 2)
    bpt = QT // L_SEL
    code = (np.arange(nt)[:, None] % 2) * bpt + np.arange(QT)[None, :] // L_SEL
    mark = jnp.asarray(code[..., None] == np.arange(LANES - HEAD_DIM)[None, None, :], BF16)
    ks = kv3[..., 2 * KV_DIM:3 * KV_DIM].astype(BF16).reshape(b, nt, QT, KV_HEADS, HEAD_DIM)
    ksel_aug = jnp.concatenate([ks, jnp.broadcast_to(mark[None, :, :, None, :], ks.shape[:4] + (LANES - HEAD_DIM,))],
                               axis=-1).reshape(b, nt, QT, KV_HEADS * LANES)
    y_nsaT = _nsa_prompt(qT, gT, kcvc, ksel_aug, val_tiles(kv3[..., 3 * KV_DIM:]), key_tiles(kvw3[..., :KV_DIM]),
                         val_tiles(kvw3[..., KV_DIM:]), tc, tiles, near, ov)
    y_nsa = y_nsaT.swapaxes(1, 2).reshape(b * s, Q_DIM)
    y_prompt = mix_ffn(xp2, y_lru.reshape(b * s, D_LRU), y_nsa, mg).reshape(b, s, D_MODEL)
    kv_rows_prompt = kv3.reshape(1, b, s, 4, KV_HEADS, HEAD_DIM)
    win_prompt = kvw3[:, s - min(WINDOW, s):].reshape(1, b, min(WINDOW, s), 2, KV_HEADS, HEAD_DIM)

    xs2 = x_sample.reshape(db * ds, D_MODEL)
    lx, lg, q, kv, kvw, ng, mg = _proj(xs2, g_mix2, ws)
    tmaj = lambda a: a.reshape(db, ds, D_LRU).swapaxes(0, 1)
    y_lru_t, conv_t, h_s = _rglru_sample(tmaj(lx), tmaj(lg), state_conv[0].swapaxes(0, 1), state_h[0], *lru_args)
    y_lru = y_lru_t.swapaxes(0, 1).reshape(db * ds, D_LRU)
    kv_s3 = kv.reshape(db, ds, 4 * KV_DIM)
    kvw_s3 = kvw.reshape(db, ds, 2 * KV_DIM)
    cache_t = cache_kv[0].transpose(0, 2, 3, 4, 1).reshape(cache_kv.shape[1], 4 * KV_DIM, PAGE_SIZE)
    win_t = state_kv_win[0].transpose(0, 2, 3, 4, 1).reshape(db, 2 * KV_DIM, wb)
    new_c = jnp.pad(kv_s3[..., :2 * KV_DIM], ((0, 0), (0, PAGE_SIZE - ds), (0, 0)))
    pps = 32
    assert npg % pps == 0 and LANES % (pps * PAGE_SIZE // L_SEL) == 0
    kcvc_s = _compress_sample(cache_t, page_table, new_c, w1f, b1, w2, pps)

    nsb_s = past // L_SEL + 1
    ntok = past // STRIDE
    jq = np.arange(ds)[None, :]
    nrow = np.arange(ntok)[:, None]
    rel = past + jq - STRIDE * nrow - (L_CMP - 1)
    bc = _sample_tile(rel_b, rel, rel >= 0)
    rn = np.arange(PAGE_SIZE)[:, None]
    new_rel, new_ok = jq - rn, (jq - rn >= 0) & (rn < ds)
    rw = np.arange(wb)[:, None]
    bw = _sample_tile(rel_b, np.concatenate([wb + jq - rw, new_rel], axis=0),
                      np.concatenate([(wb + jq - rw < WINDOW) & (past - wb + rw >= 0), new_ok], axis=0))
    bnew = _sample_tile(rel_b, new_rel, new_ok)
    step_keys = pps * PAGE_SIZE
    far_rel = np.full((step_keys, ds), MAX_DISTANCE)
    last_rel = far_rel.copy()
    last_rel[step_keys - PAGE_SIZE:] = PAGE_SIZE + jq - rn
    all_ok = np.ones((step_keys, ds), bool)
    bstep = jnp.stack([_sample_tile(rel_b, far_rel, all_ok), _sample_tile(rel_b, last_rel, all_ok)])
    nlane = -(-nsb_s // LANES) * LANES
    bb = np.arange(nlane)[None, :]
    tt = np.arange(ntok)[:, None]
    ovt = jnp.asarray((tt >= 4 * bb - 1) & (tt <= 4 * bb + 3) & (bb < nsb_s), BF16)
    ncol = N_HEADS * ds
    col = np.arange(ncol)
    kvh_c, j_c = col // (GROUP * ds), col % ds
    gg = jnp.asarray((kvh_c[:, None] == kvh_c[None, :]) & (j_c[:, None] == j_c[None, :]), BF16)
    bps = pps * PAGE_SIZE // L_SEL
    off = np.arange(LANES // bps)[:, None, None]
    expand = jnp.asarray(np.arange(LANES)[None, :, None] == bps * off + np.arange(step_keys)[None, None, :] // L_SEL,
                         BF16)
    q5 = q.reshape(db, ds, KV_HEADS, GROUP, HEAD_DIM) * QK_SCALE
    qbd = jnp.einsum('bjkgd,kc->bkgjcd', q5, jnp.eye(KV_HEADS, dtype=F32)).reshape(db, ncol, KV_DIM).astype(BF16)
    gates_s = ng[:, :3 * N_HEADS].reshape(db, ds, N_HEADS, 3).transpose(0, 2, 1, 3).reshape(db, ncol, 3)
    new_all = jnp.pad(jnp.concatenate([kv_s3[..., 2 * KV_DIM:], kvw_s3], axis=-1),
                      ((0, 0), (0, PAGE_SIZE - ds), (0, 0)))
    o_bd = _nsa_sample(cache_t, page_table, qbd, gates_s, kcvc_s, win_t, new_all, bc, bw, bstep, bnew, ovt, gg,
                       expand, pps)
    o6 = o_bd.reshape(db, KV_HEADS, GROUP, ds, KV_HEADS, HEAD_DIM)
    y_nsa = jnp.einsum('bkgjkd->bjkgd', o6).reshape(db * ds, Q_DIM)
    y_sample = mix_ffn(xs2, y_lru, y_nsa, mg).reshape(db, ds, D_MODEL)
    kv_rows_sample = kv_s3.reshape(1, db, ds, 4, KV_HEADS, HEAD_DIM)
    win_all = jnp.concatenate([state_kv_win[0].reshape(db, wb, 2 * KV_DIM), kvw_s3], axis=1)
    win_sample = win_all[:, win_all.shape[1] - WINDOW:].reshape(1, db, WINDOW, 2, KV_HEADS, HEAD_DIM)
    conv_s = conv_t.swapaxes(0, 1)

    return (y_prompt, y_sample, kv_rows_prompt, win_prompt, conv_p[None], h_p.reshape(1, b, D_LRU),
            kv_rows_sample, win_sample, conv_s[None], h_s[None])
```

```python
import functools
import math

import numpy as np
import jax
import jax.numpy as jnp
from jax import lax
from jax.experimental import pallas as pl
from jax.experimental.pallas import tpu as pltpu

F32 = jnp.float32
BF16 = jnp.bfloat16

D_MODEL = 1024
D_LRU = 1280
LRU_HEADS = 16
LRU_BLOCK = D_LRU // LRU_HEADS
CONV_W = 4
LRU_C = 8.0
N_HEADS = 16
HEAD_DIM = 64
KV_HEADS = 4
GROUP = N_HEADS // KV_HEADS
L_CMP = 32
STRIDE = 16
CMP_HIDDEN = 128
L_SEL = 64
N_SEL = 16
N_LOCAL = 2
WINDOW = 512
PAGE_SIZE = 128
N_BUCKETS = 32
MAX_DISTANCE = 128
N_GROUPS = 4
EXPERTS_PER_GROUP = 8
N_EXPERTS = N_GROUPS * EXPERTS_PER_GROUP
D_EXPERT = 256
EPS = 1e-6
NEG = -1e30
FORCED = 1e9
Q_DIM = N_HEADS * HEAD_DIM
KV_DIM = KV_HEADS * HEAD_DIM
QT = 128
GQ = GROUP * QT
LANES = 128
SUB = 8
VMEM_LIMIT = 56 * 1024 * 1024
KNOCKED = -3e38
ABSENT = -2e38
LOG2E = math.log2(math.e)
QK_SCALE = HEAD_DIM ** -0.5 * LOG2E
MASK_ROWS = 16
ONES_ROWS = 16


def _cparams(sem):
    return pltpu.CompilerParams(dimension_semantics=sem, vmem_limit_bytes=VMEM_LIMIT)


def _dot(a, b):
    return jnp.dot(a, b, preferred_element_type=F32)


def _dot_tn(a, b):
    return lax.dot_general(a, b, (((0,), (0,)), ((), ())), preferred_element_type=F32)


def _dot_nt(a, b):
    return lax.dot_general(a, b, (((1,), (1,)), ((), ())), preferred_element_type=F32)


def _split3(x):
    hi = x.astype(BF16)
    r1 = x - hi.astype(F32)
    mid = r1.astype(BF16)
    lo = (r1 - mid.astype(F32)).astype(BF16)
    return hi, mid, lo


def _dot_exact_rhs(a_bf16, x):
    hi, mid, lo = _split3(x)
    return _dot(a_bf16, hi) + _dot(a_bf16, mid) + _dot(a_bf16, lo)


def _dot_exact_lhs(x, b_bf16):
    hi, mid, lo = _split3(x)
    return _dot(hi, b_bf16) + _dot(mid, b_bf16) + _dot(lo, b_bf16)


def _proj_kernel(x_ref, g_ref, *refs, n, feature_major):
    x = x_ref[...]
    xn = x * lax.rsqrt(jnp.mean(x * x, axis=-1, keepdims=True) + EPS) * g_ref[...]
    xb = xn.astype(BF16)
    outs = {}
    for j, (w_ref, o_ref) in enumerate(zip(refs[:n], refs[n:2 * n])):
        outs[j] = _dot(xb, w_ref[...])
        o_ref[...] = outs[j]
    for j, o_ref in zip(feature_major, refs[2 * n:]):
        o_ref[0] = outs[j].T


def _proj(x2d, g, ws, rows_per_seq, feature_major=(), tm=256):
    t = x2d.shape[0]
    tm = min(tm, t, rows_per_seq)
    tps = rows_per_seq // tm
    in_specs = [pl.BlockSpec((tm, D_MODEL), lambda i: (i, 0)), pl.BlockSpec((1, D_MODEL), lambda i: (0, 0))]
    in_specs += [pl.BlockSpec(w.shape, lambda i: (0, 0), pipeline_mode=pl.Buffered(1)) for w in ws]
    out_specs = [pl.BlockSpec((tm, w.shape[1]), lambda i: (i, 0)) for w in ws]
    out_shape = [jax.ShapeDtypeStruct((t, w.shape[1]), F32) for w in ws]
    for j in feature_major:
        out_specs.append(pl.BlockSpec((1, ws[j].shape[1], tm), lambda i: (i // tps, 0, i % tps)))
        out_shape.append(jax.ShapeDtypeStruct((t // rows_per_seq, ws[j].shape[1], rows_per_seq), F32))
    return pl.pallas_call(functools.partial(_proj_kernel, n=len(ws), feature_major=tuple(feature_major)),
                          grid=(t // tm,), in_specs=in_specs, out_specs=out_specs, out_shape=out_shape,
                          compiler_params=_cparams(("parallel",)), name="proj")(x2d, g, *ws)


def _softplus(x):
    return jnp.maximum(x, 0.0) + jnp.log1p(jnp.exp(-jnp.abs(x)))


def _lru_gates(xc, wg_ref, bgx_ref, bga_ref, lam_ref):
    gates = _dot(xc.astype(BF16), wg_ref[...])
    gx = jax.nn.sigmoid(gates[:, :D_LRU] + bgx_ref[...])
    ga = jax.nn.sigmoid(gates[:, D_LRU:] + bga_ref[...])
    log_a = -LRU_C * ga * _softplus(-lam_ref[...])
    a = jnp.exp(log_a)
    th = jnp.tanh(log_a)
    u = jnp.sqrt(-2.0 * th / (1.0 - th)) * (gx * xc)
    return a, u


def _rglru_prompt_kernel(lx_ref, lg_ref, cw_ref, cb_ref, wg_ref, bgx_ref, bga_ref, lam_ref,
                         y_ref, conv_ref, h_ref, xext, hc, *, tc):
    t = pl.program_id(1)

    @pl.when(t == 0)
    def _():
        xext[0:8, :] = jnp.zeros((8, D_LRU), F32)
        hc[...] = jnp.zeros_like(hc)

    x = lx_ref[0]
    xext[8:8 + tc, :] = x
    cw = cw_ref[...]
    xc = cb_ref[...] + cw[3:4] * x
    for j in range(CONV_W - 1):
        xc = xc + cw[j:j + 1] * xext[5 + j:5 + j + tc, :]
    a, u = _lru_gates(xc, wg_ref, bgx_ref, bga_ref, lam_ref)
    sub = lax.broadcasted_iota(jnp.int32, (tc, D_LRU), 0) % SUB
    s = 1
    while s < SUB:
        a_sh = pltpu.roll(a, s, 0)
        u_sh = pltpu.roll(u, s, 0)
        m = sub >= s
        u = jnp.where(m, a * u_sh + u, u)
        a = jnp.where(m, a * a_sh, a)
        s *= 2
    carry = hc[0:1, :]
    lg = lg_ref[0]
    for g in range(tc // SUB):
        rows = slice(g * SUB, (g + 1) * SUB)
        h = a[rows] * carry + u[rows]
        carry = h[SUB - 1:SUB, :]
        y_ref[0, rows, :] = h * jax.nn.gelu(lg[rows])
    hc[0:1, :] = carry
    xext[0:8, :] = x[tc - 8:tc, :]
    conv_ref[0] = x[tc - (CONV_W - 1):tc, :]
    h_ref[0] = carry


def _rglru_prompt(lx, lg, cw, cb, wg, bgx, bga, lam, tc=256):
    b, s, _ = lx.shape
    row = lambda shape: pl.BlockSpec(shape, lambda bi, ti: (0, 0))
    return pl.pallas_call(
        functools.partial(_rglru_prompt_kernel, tc=tc), grid=(b, s // tc),
        in_specs=[pl.BlockSpec((1, tc, D_LRU), lambda bi, ti: (bi, ti, 0)),
                  pl.BlockSpec((1, tc, D_LRU), lambda bi, ti: (bi, ti, 0)),
                  row((CONV_W, D_LRU)), row((1, D_LRU)), row((D_LRU, 2 * D_LRU)),
                  row((1, D_LRU)), row((1, D_LRU)), row((1, D_LRU))],
        out_specs=[pl.BlockSpec((1, tc, D_LRU), lambda bi, ti: (bi, ti, 0)),
                   pl.BlockSpec((1, CONV_W - 1, D_LRU), lambda bi, ti: (bi, 0, 0)),
                   pl.BlockSpec((1, 1, D_LRU), lambda bi, ti: (bi, 0, 0))],
        out_shape=[jax.ShapeDtypeStruct((b, s, D_LRU), F32),
                   jax.ShapeDtypeStruct((b, CONV_W - 1, D_LRU), F32),
                   jax.ShapeDtypeStruct((b, 1, D_LRU), F32)],
        scratch_shapes=[pltpu.VMEM((tc + 8, D_LRU), F32), pltpu.VMEM((8, D_LRU), F32)],
        compiler_params=_cparams(("parallel", "arbitrary")), name="rglru_prompt",
    )(lx, lg, cw, cb, wg, bgx, bga, lam)


def _rglru_sample_kernel(lx_ref, lg_ref, cbuf_ref, h0_ref, cw_ref, cb_ref, wg_ref, bgx_ref, bga_ref, lam_ref,
                         y_ref, conv_ref, h_ref, *, ds):
    cw = cw_ref[...]
    xp = [cbuf_ref[j] for j in range(CONV_W - 1)] + [lx_ref[j] for j in range(ds)]
    h = h0_ref[...]
    for t in range(ds):
        xc = cb_ref[...]
        for j in range(CONV_W):
            xc = xc + cw[j:j + 1] * xp[t + j]
        a, u = _lru_gates(xc, wg_ref, bgx_ref, bga_ref, lam_ref)
        h = a * h + u
        y_ref[t] = h * jax.nn.gelu(lg_ref[t])
    for j in range(CONV_W - 1):
        conv_ref[j] = xp[ds + j]
    h_ref[...] = h


def _rglru_sample(lx_t, lg_t, cbuf_t, h0, cw, cb, wg, bgx, bga, lam):
    ds, n, _ = lx_t.shape
    full = lambda a: pl.BlockSpec(a.shape, lambda i: (0,) * a.ndim)
    args = (lx_t, lg_t, cbuf_t, h0, cw, cb, wg, bgx, bga, lam)
    out_shape = [jax.ShapeDtypeStruct((ds, n, D_LRU), F32), jax.ShapeDtypeStruct((CONV_W - 1, n, D_LRU), F32),
                 jax.ShapeDtypeStruct((n, D_LRU), F32)]
    return pl.pallas_call(
        functools.partial(_rglru_sample_kernel, ds=ds), grid=(1,),
        in_specs=[full(a) for a in args], out_specs=[full(o) for o in out_shape], out_shape=out_shape,
        compiler_params=_cparams(("arbitrary",)), name="rglru_sample")(*args)


HEADS_PER_TILE = LANES // HEAD_DIM


def _chunk_project(load, w):
    feat = jnp.concatenate([load(s) for s in range(STRIDE)], axis=1).astype(BF16)
    return _dot(feat, w)


def _compress_prompt_kernel(xa_ref, xb_ref, w1_ref, b1_ref, w2_ref, o_ref, *, ncp):
    rows = lax.broadcasted_iota(jnp.int32, (ncp, HEAD_DIM), 0)
    for c, x_ref in enumerate((xa_ref, xb_ref)):
        p2 = _chunk_project(lambda s: x_ref[0, pl.ds(s, ncp, stride=STRIDE), :], w1_ref[0])
        for hh in range(HEADS_PER_TILE):
            k = c * HEADS_PER_TILE + hh
            p = p2[:, hh * 2 * CMP_HIDDEN:(hh + 1) * 2 * CMP_HIDDEN]
            pre = b1_ref[0] + p[:, :CMP_HIDDEN] + pltpu.roll(p[:, CMP_HIDDEN:], ncp - 1, 0)
            phi = _dot(jax.nn.gelu(pre).astype(BF16), w2_ref[0])
            o_ref[0, 0, k, 0:ncp, :] = jnp.zeros((ncp, HEAD_DIM), F32)
            o_ref[0, 0, k, ncp:2 * ncp, :] = jnp.where(rows < ncp - 1, phi, 0.0)


def _compress_prompt(kv, w1f, b1, w2):
    b, s, _ = kv.shape
    ncp = s // STRIDE
    return pl.pallas_call(
        functools.partial(_compress_prompt_kernel, ncp=ncp), grid=(b, 2),
        in_specs=[pl.BlockSpec((1, s, LANES), lambda bi, sl: (bi, 0, 2 * sl)),
                  pl.BlockSpec((1, s, LANES), lambda bi, sl: (bi, 0, 2 * sl + 1)),
                  pl.BlockSpec((1,) + w1f.shape[1:], lambda bi, sl: (sl, 0, 0)),
                  pl.BlockSpec((1, 1, CMP_HIDDEN), lambda bi, sl: (sl, 0, 0)),
                  pl.BlockSpec((1, CMP_HIDDEN, HEAD_DIM), lambda bi, sl: (sl, 0, 0))],
        out_specs=pl.BlockSpec((1, 1, KV_HEADS, 2 * ncp, HEAD_DIM), lambda bi, sl: (sl, bi, 0, 0, 0)),
        out_shape=jax.ShapeDtypeStruct((2, b, KV_HEADS, 2 * ncp, HEAD_DIM), F32),
        compiler_params=_cparams(("parallel", "parallel")), name="compress_prompt")(kv, kv, w1f, b1, w2)


def _compress_sample_kernel(pt_ref, *refs, pps, nsteps):
    pages = refs[:pps]
    new_ref, w1_ref, b1_ref, w2_ref, o_ref, xs, pscr = refs[pps:]
    st = pl.program_id(1)
    cps = PAGE_SIZE // STRIDE
    m = pps * cps
    ntok = nsteps * m
    tiles_per_slot = KV_DIM // LANES

    def stage(ref, j, transposed):
        for c in range(2 * tiles_per_slot):
            if transposed:
                blk = ref[0, c * LANES:(c + 1) * LANES, :].T
            else:
                blk = ref[0, :, c * LANES:(c + 1) * LANES]
            xs[c, j * PAGE_SIZE:(j + 1) * PAGE_SIZE, :] = blk

    def project(nrows, row0):
        for sl in range(2):
            for c in range(tiles_per_slot):
                p2 = _chunk_project(lambda s: xs[sl * tiles_per_slot + c, pl.ds(s, nrows, stride=STRIDE), :],
                                    w1_ref[sl])
                for hh in range(HEADS_PER_TILE):
                    pscr[sl, c * HEADS_PER_TILE + hh, pl.ds(row0, nrows), :] = (
                        p2[:, hh * 2 * CMP_HIDDEN:(hh + 1) * 2 * CMP_HIDDEN])

    for j, pg in enumerate(pages):
        stage(pg, j, True)
    project(m, pl.multiple_of(st * m, m))

    @pl.when(st == nsteps - 1)
    def _():
        stage(new_ref, 0, False)
        project(cps, ntok)
        for sl in range(2):
            toks = []
            for k in range(KV_HEADS):
                pre = (b1_ref[sl] + pscr[sl, k, 0:ntok, 0:CMP_HIDDEN]
                       + pscr[sl, k, 1:ntok + 1, CMP_HIDDEN:2 * CMP_HIDDEN])
                toks.append(_dot(jax.nn.gelu(pre).astype(BF16), w2_ref[sl]))
            o_ref[0, sl] = jnp.concatenate(toks, axis=1)


def _compress_sample(cache, page_table, new_rows, w1f, b1, w2, pps=16):
    db, npg = page_table.shape
    pps = min(pps, npg)
    nsteps = npg // pps
    cps = PAGE_SIZE // STRIDE
    ntok = npg * cps

    def page_spec(j):
        return pl.BlockSpec((1, 2 * KV_DIM, PAGE_SIZE), lambda bi, st, pt: (pt[bi, st * pps + j], 0, 0))

    const = lambda shape: pl.BlockSpec(shape, lambda bi, st, pt: (0,) * len(shape))
    grid_spec = pltpu.PrefetchScalarGridSpec(
        num_scalar_prefetch=1, grid=(db, nsteps),
        in_specs=[page_spec(j) for j in range(pps)] + [
            pl.BlockSpec((1, PAGE_SIZE, 2 * KV_DIM), lambda bi, st, pt: (bi, 0, 0)),
            const(w1f.shape), const((2, 1, CMP_HIDDEN)),
            const((2, CMP_HIDDEN, HEAD_DIM))],
        out_specs=pl.BlockSpec((1, 2, ntok, KV_DIM), lambda bi, st, pt: (bi, 0, 0, 0)),
        scratch_shapes=[pltpu.VMEM((2 * KV_DIM // LANES, pps * PAGE_SIZE, LANES), F32),
                        pltpu.VMEM((2, KV_HEADS, ntok + cps, 2 * CMP_HIDDEN), F32)])
    return pl.pallas_call(
        functools.partial(_compress_sample_kernel, pps=pps, nsteps=nsteps), grid_spec=grid_spec,
        out_shape=jax.ShapeDtypeStruct((db, 2, ntok, KV_DIM), F32),
        compiler_params=_cparams(("parallel", "arbitrary")), name="compress_sample",
    )(page_table, *([cache] * pps), new_rows, w1f, b1, w2)


def _topk_masks(scores, axis):
    n = scores[0].shape[axis]
    idx = lax.broadcasted_iota(jnp.int32, scores[0].shape, axis).astype(F32)

    def body(_, scs):
        out = []
        for sc in scs:
            mx = jnp.max(sc, axis=axis, keepdims=True)
            first = jnp.min(jnp.where(sc == mx, idx, float(n)), axis=axis, keepdims=True)
            out.append(jnp.where(idx == first, KNOCKED, sc))
        return tuple(out)

    return [jnp.where(sc == KNOCKED, 1.0, 0.0) for sc in lax.fori_loop(0, N_SEL, body, tuple(scores))]


def _softmax2(s, axis):
    m = jnp.max(s, axis=axis, keepdims=True)
    e = jnp.where(s > 0.5 * NEG, jnp.exp2(s - m), 0.0)
    l = jnp.sum(e, axis=axis, keepdims=True)
    return e * (1.0 / jnp.maximum(l, 1e-30))


def _nsa_prompt_kernel(qT_ref, gT_ref, kc_ref, vc_ref, ksel_ref, vselT_ref, kwin_ref, vwinT_ref,
                       tc_ref, tiles_ref, near_ref, ov_ref, y_ref,
                       qk_s, madd_s, oc_s, m_s, acc_s, s_scr, p_scr, alpha_s, *, ncp, nsb):
    i = pl.program_id(1)
    qT = qT_ref[0]
    for k in range(KV_HEADS):
        qk_s[k, 0:HEAD_DIM, :] = (jnp.concatenate([qT[(GROUP * k + g) * HEAD_DIM:(GROUP * k + g + 1) * HEAD_DIM, :]
                                                   for g in range(GROUP)], axis=1) * QK_SCALE).astype(BF16)
        qk_s[k, HEAD_DIM:LANES, :] = jnp.zeros((LANES - HEAD_DIM, GQ), BF16)
    m_s[...] = jnp.full(m_s.shape, NEG, F32)
    acc_s[...] = jnp.zeros_like(acc_s)

    start = pl.multiple_of(8 * i + 8, 8)
    tok_ok = lax.broadcasted_iota(jnp.int32, (ncp, 1), 0) >= ncp - 8 - 8 * i
    rblk = lax.broadcasted_iota(jnp.int32, (nsb, QT), 0)
    qhalf = jnp.where(lax.broadcasted_iota(jnp.int32, (nsb, QT), 1) >= L_SEL, 1, 0)
    r_qb = nsb - 2 + qhalf
    exists = rblk >= nsb - 2 - 2 * i
    forced = (rblk == nsb - 2 - 2 * i) | (rblk > r_qb - N_LOCAL)
    def block_scores(k):
        kc = kc_ref[0, 0, k, pl.ds(start, ncp), :].astype(BF16)
        vc = vc_ref[0, 0, k, pl.ds(start, ncp), :].astype(BF16)
        s = jnp.where(tok_ok, _dot(kc, qk_s[k, 0:HEAD_DIM, :]) + tc_ref[k], NEG)
        pn = _softmax2(s, 0)
        oc_s[k] = _dot_tn(vc, pn.astype(BF16))
        psum = pn[:, 0:QT]
        for g in range(1, GROUP):
            psum = psum + pn[:, g * QT:(g + 1) * QT]
        imp = _dot_exact_rhs(ov_ref[...], psum)
        score = jnp.where(rblk > r_qb, -FORCED, jnp.where(forced, FORCED, imp))
        return jnp.where(exists, score, ABSENT)

    for k0 in range(0, KV_HEADS, 2):
        for j, sel in enumerate(_topk_masks([block_scores(k0), block_scores(k0 + 1)], 0)):
            madd_s[k0 + j] = (sel - 1.0) * (-NEG)

    bpt = QT // L_SEL

    def logits(t, n, k_ref, slot, masked):
        kt = k_ref[0, t] if n == 1 else jnp.concatenate([k_ref[0, t + j] for j in range(n)], axis=0)
        for k in range(KV_HEADS):
            if masked:
                r0 = bpt * (t - i) + nsb - bpt
                par = t % 2
                first = [r0 + (bpt * par if n == 2 else 0) + h for h in range(bpt)]
                second = [r0 + (bpt * (1 - par) if n == 2 else 0) + h for h in range(bpt)]
                rows = [jnp.concatenate([madd_s[k, pl.ds(r, 1), :]] * GROUP, axis=1) for r in first + second]
                blk = jnp.concatenate(rows + [jnp.zeros((MASK_ROWS - 2 * bpt, GQ), F32)], axis=0)
                qk_s[k, HEAD_DIM:HEAD_DIM + MASK_ROWS, :] = blk.astype(BF16)
                s_scr[slot, k, 0:n * QT, :] = _dot(kt[:, k * LANES:(k + 1) * LANES], qk_s[k])
            else:
                s_scr[slot, k, 0:n * QT, :] = _dot(kt[:, k * HEAD_DIM:(k + 1) * HEAD_DIM], qk_s[k, 0:HEAD_DIM, :])

    def softmax(n, add_of, branch, slot):
        nk = n * QT
        for k in range(KV_HEADS):
            idx = branch * KV_HEADS + k
            s = s_scr[slot, k, 0:nk, :]
            if add_of is not None:
                s = s + add_of(k)
            m_old = m_s[idx]
            m_new = jnp.maximum(m_old, jnp.max(s, axis=0, keepdims=True))
            alpha_s[slot, k] = jnp.exp2(m_old - m_new)
            m_s[idx] = m_new
            p_scr[slot, k, 0:nk, :] = jnp.exp2((s - m_new).astype(BF16))

    def values(t, n, vT_ref, branch, slot):
        nk = n * QT
        vt = vT_ref[0, t] if n == 1 else jnp.concatenate([vT_ref[0, t + j] for j in range(n)], axis=1)
        ones = jnp.ones((ONES_ROWS, nk), BF16)
        for k in range(KV_HEADS):
            idx = branch * KV_HEADS + k
            vt_aug = jnp.concatenate([vt[k * HEAD_DIM:(k + 1) * HEAD_DIM, :], ones], axis=0)
            acc_s[idx] = alpha_s[slot, k] * acc_s[idx] + _dot(vt_aug, p_scr[slot, k, 0:nk, :])

    def attend(t, n, k_ref, vT_ref, add_of, branch):
        logits(t, n, k_ref, 0, branch == 0)
        softmax(n, add_of, branch, 0)
        values(t, n, vT_ref, branch, 0)

    nfar = jnp.maximum(i - 1, 0)
    npair = nfar // 2

    last_pair = 2 * (npair - 1)

    @pl.when(npair >= 2)
    def _():
        logits(0, 2, ksel_ref, 0, True)
        p_scr[1] = jnp.zeros(p_scr.shape[1:], BF16)
        alpha_s[1] = jnp.ones(alpha_s.shape[1:], F32)

    def far_body(u, c):
        ta = 4 * u
        logits(ta + 2, 2, ksel_ref, 1, True)
        values(jnp.maximum(ta - 2, 0), 2, vselT_ref, 0, 1)
        softmax(2, None, 0, 0)
        logits(jnp.minimum(ta + 4, last_pair), 2, ksel_ref, 0, True)
        values(ta, 2, vselT_ref, 0, 0)
        softmax(2, None, 0, 1)
        return c

    ntrip = npair // 2
    lax.fori_loop(0, ntrip, far_body, 0)

    @pl.when(ntrip > 0)
    def _():
        values(4 * ntrip - 2, 2, vselT_ref, 0, 1)

    @pl.when(npair % 2 == 1)
    def _():
        attend(last_pair, 2, ksel_ref, vselT_ref, None, 0)

    @pl.when(nfar % 2 == 1)
    def _():
        attend(nfar - 1, 1, ksel_ref, vselT_ref, None, 0)

    @pl.when(i >= 4)
    def _():
        attend(i - 4, 1, kwin_ref, vwinT_ref, lambda k: tiles_ref[1, k], 1)

    @pl.when(i >= 3)
    def _():
        attend(i - 3, 2, kwin_ref, vwinT_ref, None, 1)

    @pl.when(i == 2)
    def _():
        attend(0, 1, kwin_ref, vwinT_ref, None, 1)

    @pl.when(i >= 1)
    def _():
        attend(i - 1, 2, ksel_ref, vselT_ref, lambda k: near_ref[k], 0)
        attend(i - 1, 2, kwin_ref, vwinT_ref, lambda k: near_ref[k], 1)

    @pl.when(i == 0)
    def _():
        attend(0, 1, ksel_ref, vselT_ref, lambda k: tiles_ref[0, k], 0)
        attend(0, 1, kwin_ref, vwinT_ref, lambda k: tiles_ref[0, k], 1)

    gate = jax.nn.sigmoid(gT_ref[0])
    for k in range(KV_HEADS):
        o_s = acc_s[k, 0:HEAD_DIM] * (1.0 / acc_s[k, HEAD_DIM:HEAD_DIM + 1])
        o_w = acc_s[KV_HEADS + k, 0:HEAD_DIM] * (1.0 / acc_s[KV_HEADS + k, HEAD_DIM:HEAD_DIM + 1])
        o_c = oc_s[k]
        for g in range(GROUP):
            h = GROUP * k + g
            cols = slice(g * QT, (g + 1) * QT)
            y_ref[0, h * HEAD_DIM:(h + 1) * HEAD_DIM, :] = (
                gate[3 * h:3 * h + 1] * o_c[:, cols] + gate[3 * h + 1:3 * h + 2] * o_s[:, cols]
                + gate[3 * h + 2:3 * h + 3] * o_w[:, cols])


def _nsa_prompt(qT, gT, kcvc, ksel, vselT, kwin, vwinT, tc, tiles, near, ov):
    b, _, s = qT.shape
    nq = s // QT
    ncp = s // STRIDE
    nsb = s // L_SEL
    seq4 = lambda a: pl.BlockSpec((1,) + a.shape[1:], lambda bi, qi: (bi, 0, 0, 0), pipeline_mode=pl.Buffered(1))
    const = lambda a: pl.BlockSpec(a.shape, lambda bi, qi: (0,) * a.ndim, pipeline_mode=pl.Buffered(1))
    cmp_spec = lambda sl: pl.BlockSpec((1, 1, KV_HEADS, 2 * ncp, HEAD_DIM), lambda bi, qi: (sl, bi, 0, 0, 0),
                                       pipeline_mode=pl.Buffered(1))
    return pl.pallas_call(
        functools.partial(_nsa_prompt_kernel, ncp=ncp, nsb=nsb), grid=(b, nq),
        in_specs=[pl.BlockSpec((1, Q_DIM, QT), lambda bi, qi: (bi, 0, qi)),
                  pl.BlockSpec((1, 3 * N_HEADS, QT), lambda bi, qi: (bi, 0, qi)),
                  cmp_spec(0), cmp_spec(1), seq4(ksel), seq4(vselT), seq4(kwin), seq4(vwinT),
                  const(tc), const(tiles), const(near), const(ov)],
        out_specs=pl.BlockSpec((1, Q_DIM, QT), lambda bi, qi: (bi, 0, qi)),
        out_shape=jax.ShapeDtypeStruct((b, Q_DIM, s), F32),
        scratch_shapes=[pltpu.VMEM((KV_HEADS, LANES, GQ), BF16), pltpu.VMEM((KV_HEADS, nsb, QT), F32),
                        pltpu.VMEM((KV_HEADS, HEAD_DIM, GQ), F32), pltpu.VMEM((2 * KV_HEADS, 1, GQ), F32),
                        pltpu.VMEM((2 * KV_HEADS, HEAD_DIM + ONES_ROWS, GQ), F32),
                        pltpu.VMEM((2, KV_HEADS, 2 * QT, GQ), F32), pltpu.VMEM((2, KV_HEADS, 2 * QT, GQ), BF16),
                        pltpu.VMEM((2, KV_HEADS, 1, GQ), F32)],
        compiler_params=_cparams(("parallel", "arbitrary")), name="nsa_prompt",
    )(qT, gT, kcvc, kcvc, ksel, vselT, kwin, vwinT, tc, tiles, near, ov)


def _nsa_sample_kernel(pt_ref, *refs, pps, nsteps, nsb):
    pages = refs[:pps]
    (qbd_ref, gate_ref, kcvc_ref, win_ref, new_ref, bc_ref, bw_ref, bstep_ref, bnew_ref, ovt_ref, gg_ref, exp_ref,
     o_ref, madd_s, oc_s, ow_s, m_s, l_s, acc_s) = refs[pps:]
    st = pl.program_id(1)
    qbd = qbd_ref[0]
    ncol = qbd.shape[0]
    wb = win_ref.shape[2]
    bps = pps * (PAGE_SIZE // L_SEL)
    nlane = ovt_ref.shape[1]

    def online(s, pv_of):
        m_old = m_s[...]
        m_new = jnp.maximum(m_old, jnp.max(s, axis=1, keepdims=True))
        alpha = jnp.exp2(m_old - m_new)
        p = jnp.exp2(s - m_new)
        l_s[...] = alpha * l_s[...] + jnp.sum(p, axis=1, keepdims=True)
        acc_s[...] = alpha * acc_s[...] + pv_of(p.astype(BF16))
        m_s[...] = m_new

    @pl.when(st == 0)
    def _():
        m_s[...] = jnp.full(m_s.shape, NEG, F32)
        l_s[...] = jnp.zeros_like(l_s)
        acc_s[...] = jnp.zeros_like(acc_s)
        pn = _softmax2(_dot_nt(qbd, kcvc_ref[0, 0].astype(BF16)) + bc_ref[...], 1)
        oc_s[...] = _dot(pn.astype(BF16), kcvc_ref[0, 1].astype(BF16))
        imp = sum(_dot_tn(part, gg_ref[...]) for part in _split3(_dot_exact_lhs(pn, ovt_ref[...])))
        nrow = -(-nsb // 8) * 8
        blk = lax.broadcasted_iota(jnp.int32, (nrow, ncol), 0)
        qb = nsb - 1
        forced = (blk == 0) | (blk > qb - N_LOCAL)
        score = jnp.where(blk > qb, -FORCED, jnp.where(forced, FORCED, imp[0:nrow]))
        score = jnp.where(blk < nsb, score, ABSENT)
        madd = jnp.concatenate([jnp.where(_topk_masks([score], 0)[0] > 0.0, 0.0, NEG),
                                jnp.full((nlane - nrow, ncol), NEG, F32)], axis=0).astype(BF16)
        for tl in range(nlane // LANES):
            madd_s[tl] = madd[tl * LANES:(tl + 1) * LANES, :]
        new = new_ref[0]
        s_w = jnp.concatenate([_dot(qbd, win_ref[0, 0:KV_DIM, :].astype(BF16)),
                               _dot_nt(qbd, new[:, 2 * KV_DIM:3 * KV_DIM].astype(BF16))], axis=1) + bw_ref[...]
        pw = _softmax2(s_w, 1).astype(BF16)
        ow_s[...] = (_dot_nt(pw[:, 0:wb], win_ref[0, KV_DIM:2 * KV_DIM, :].astype(BF16))
                     + _dot(pw[:, wb:], new[:, 3 * KV_DIM:4 * KV_DIM].astype(BF16)))

    kt = jnp.concatenate([pg[0, 0:KV_DIM, :] for pg in pages], axis=1).astype(BF16)
    vt = jnp.concatenate([pg[0, KV_DIM:2 * KV_DIM, :] for pg in pages], axis=1).astype(BF16)
    b0 = st * bps
    mask = _dot_tn(madd_s[b0 // LANES], exp_ref[(b0 % LANES) // bps])
    bias = jnp.where(st == nsteps - 1, bstep_ref[1], bstep_ref[0])
    online(_dot(qbd, kt) + mask + bias, lambda p: _dot_nt(p, vt))

    @pl.when(st == nsteps - 1)
    def _():
        new = new_ref[0]
        lb = nsb - 1
        spread = (lax.broadcasted_iota(jnp.int32, (LANES, PAGE_SIZE), 0) == lb % LANES).astype(BF16)
        online(_dot_nt(qbd, new[:, 0:KV_DIM].astype(BF16)) + bnew_ref[...] + _dot_tn(madd_s[lb // LANES], spread),
               lambda p: _dot(p, new[:, KV_DIM:2 * KV_DIM].astype(BF16)))
        gate = jax.nn.sigmoid(gate_ref[0])
        o_ref[0] = (gate[:, 0:1] * oc_s[...] + gate[:, 1:2] * (acc_s[...] * (1.0 / l_s[...]))
                    + gate[:, 2:3] * ow_s[...])


def _nsa_sample(cache, page_table, qbd, gates, kcvc, win, new_rows, bc, bw, bstep, bnew, ovt, gg, expand, pps):
    db, npg = page_table.shape
    nsteps = npg // pps
    nsb = npg * (PAGE_SIZE // L_SEL) + 1
    ncol = qbd.shape[1]
    nlane = ovt.shape[1]

    def page_spec(j):
        return pl.BlockSpec((1, 2 * KV_DIM, PAGE_SIZE), lambda bi, st, pt: (pt[bi, st * pps + j], 1, 0))

    const = lambda a: pl.BlockSpec(a.shape, lambda bi, st, pt: (0,) * a.ndim)
    seq = lambda a: pl.BlockSpec((1,) + a.shape[1:], lambda bi, st, pt: (bi,) + (0,) * (a.ndim - 1))
    grid_spec = pltpu.PrefetchScalarGridSpec(
        num_scalar_prefetch=1, grid=(db, nsteps),
        in_specs=[page_spec(j) for j in range(pps)] + [seq(qbd), seq(gates), seq(kcvc), seq(win), seq(new_rows)]
        + [const(a) for a in (bc, bw, bstep, bnew, ovt, gg, expand)],
        out_specs=pl.BlockSpec((1, ncol, KV_DIM), lambda bi, st, pt: (bi, 0, 0)),
        scratch_shapes=[pltpu.VMEM((nlane // LANES, LANES, ncol), BF16), pltpu.VMEM((ncol, KV_DIM), F32),
                        pltpu.VMEM((ncol, KV_DIM), F32), pltpu.VMEM((ncol, 1), F32), pltpu.VMEM((ncol, 1), F32),
                        pltpu.VMEM((ncol, KV_DIM), F32)])
    return pl.pallas_call(
        functools.partial(_nsa_sample_kernel, pps=pps, nsteps=nsteps, nsb=nsb), grid_spec=grid_spec,
        out_shape=jax.ShapeDtypeStruct((db, ncol, KV_DIM), F32),
        compiler_params=_cparams(("parallel", "arbitrary")), name="nsa_sample",
    )(page_table, *([cache] * pps), qbd, gates, kcvc, win, new_rows, bc, bw, bstep, bnew, ovt, gg, expand)


def _merge_kernel(x_ref, ylru_ref, ynsa_ref, mg_ref, wl_ref, wn_ref, wo_ref, gf_ref, wr_ref, br_ref,
                  x2_ref, hn_ref, wt_ref):
    gate = jax.nn.sigmoid(mg_ref[...])
    mixed = (gate[:, :D_MODEL] * _dot(ylru_ref[...].astype(BF16), wl_ref[...])
             + gate[:, D_MODEL:] * _dot(ynsa_ref[...].astype(BF16), wn_ref[...]))
    x2 = x_ref[...] + _dot(mixed.astype(BF16), wo_ref[...])
    x2_ref[...] = x2
    hn = x2 * lax.rsqrt(jnp.mean(x2 * x2, axis=-1, keepdims=True) + EPS) * gf_ref[...]
    hn_ref[...] = hn.astype(BF16)
    logits = jnp.dot(hn, wr_ref[...], precision=lax.Precision.HIGHEST, preferred_element_type=F32) + br_ref[...]
    lane = lax.broadcasted_iota(jnp.int32, logits.shape, 1)
    lanef = lane.astype(F32)
    big = float(LANES)
    gl = jnp.where(lane < N_GROUPS, logits, NEG)
    gmax = jnp.max(gl, axis=-1, keepdims=True)
    grp = jnp.min(jnp.where(gl == gmax, lanef, big), axis=-1, keepdims=True)
    p_grp = 1.0 / jnp.sum(jnp.where(lane < N_GROUPS, jnp.exp(gl - gmax), 0.0), axis=-1, keepdims=True)
    lo = N_GROUPS + grp * EXPERTS_PER_GROUP
    el = jnp.where((lanef >= lo) & (lanef < lo + EXPERTS_PER_GROUP), logits, NEG)
    v1 = jnp.max(el, axis=-1, keepdims=True)
    i1 = jnp.min(jnp.where(el == v1, lanef, big), axis=-1, keepdims=True)
    el2 = jnp.where(lanef == i1, NEG, el)
    v2 = jnp.max(el2, axis=-1, keepdims=True)
    i2 = jnp.min(jnp.where(el2 == v2, lanef, big), axis=-1, keepdims=True)
    e2 = jnp.exp(v2 - v1)
    den = 1.0 / (1.0 + e2)
    wt_ref[...] = jnp.where(lanef == i1, den * p_grp, jnp.where(lanef == i2, e2 * den * p_grp, 0.0))


def _merge(x2d, ylru, ynsa, mg, wl, wn, wo, gf, wr, br, tm=256):
    t = x2d.shape[0]
    tm = min(tm, t)
    tile = lambda a: pl.BlockSpec((tm, a.shape[1]), lambda i: (i, 0))
    const = lambda a: pl.BlockSpec(a.shape, lambda i: (0, 0), pipeline_mode=pl.Buffered(1))
    return pl.pallas_call(
        _merge_kernel, grid=(t // tm,),
        in_specs=[tile(x2d), tile(ylru), tile(ynsa), tile(mg)] + [const(a) for a in (wl, wn, wo, gf, wr, br)],
        out_specs=[pl.BlockSpec((tm, D_MODEL), lambda i: (i, 0)), pl.BlockSpec((tm, D_MODEL), lambda i: (i, 0)),
                   pl.BlockSpec((tm, LANES), lambda i: (i, 0))],
        out_shape=[jax.ShapeDtypeStruct((t, D_MODEL), F32), jax.ShapeDtypeStruct((t, D_MODEL), BF16),
                   jax.ShapeDtypeStruct((t, LANES), F32)],
        compiler_params=_cparams(("parallel",)), name="merge")(x2d, ylru, ynsa, mg, wl, wn, wo, gf, wr, br)


def _moe_kernel(hn_ref, wt_ref, x2_ref, wg_ref, wu_ref, wd_ref, gfin_ref, y_ref, acc):
    c = pl.program_id(1)

    @pl.when(c == 0)
    def _():
        acc[...] = jnp.zeros_like(acc)

    h = hn_ref[...]
    wt = wt_ref[...]
    lane = lax.broadcasted_iota(jnp.int32, wt.shape, 1)
    total = acc[...]
    for e in range(EXPERTS_PER_GROUP):
        act = jax.nn.silu(_dot(h, wg_ref[e])) * _dot(h, wu_ref[e])
        w_e = jnp.sum(jnp.where(lane == N_GROUPS + c * EXPERTS_PER_GROUP + e, wt, 0.0), axis=-1, keepdims=True)
        act = jnp.where(w_e != 0.0, act * w_e, 0.0)
        total = total + _dot(act.astype(BF16), wd_ref[e])
    acc[...] = total

    @pl.when(c == N_GROUPS - 1)
    def _():
        x = x2_ref[...] + total
        y_ref[...] = x * lax.rsqrt(jnp.mean(x * x, axis=-1, keepdims=True) + EPS) * gfin_ref[...]


def _moe(hn, wt, x2, wg, wu, wd, gfin, tm=512):
    t = hn.shape[0]
    tm = min(tm, t)
    return pl.pallas_call(
        _moe_kernel, grid=(t // tm, N_GROUPS),
        in_specs=[pl.BlockSpec((tm, D_MODEL), lambda i, c: (i, 0)), pl.BlockSpec((tm, LANES), lambda i, c: (i, 0)),
                  pl.BlockSpec((tm, D_MODEL), lambda i, c: (i, 0)),
                  pl.BlockSpec((EXPERTS_PER_GROUP, D_MODEL, D_EXPERT), lambda i, c: (c, 0, 0)),
                  pl.BlockSpec((EXPERTS_PER_GROUP, D_MODEL, D_EXPERT), lambda i, c: (c, 0, 0)),
                  pl.BlockSpec((EXPERTS_PER_GROUP, D_EXPERT, D_MODEL), lambda i, c: (c, 0, 0)),
                  pl.BlockSpec((1, D_MODEL), lambda i, c: (0, 0))],
        out_specs=pl.BlockSpec((tm, D_MODEL), lambda i, c: (i, 0)),
        out_shape=jax.ShapeDtypeStruct((t, D_MODEL), F32),
        scratch_shapes=[pltpu.VMEM((tm, D_MODEL), F32)],
        compiler_params=_cparams(("parallel", "arbitrary")), name="moe")(hn, wt, x2, wg, wu, wd, gfin)


def _bucket_table():
    n = np.arange(MAX_DISTANCE + 1)
    exact = N_BUCKETS // 2
    nf = np.maximum(n, exact).astype(np.float64)
    large = exact + (np.log(nf / exact) / math.log(MAX_DISTANCE / exact) * (N_BUCKETS - exact)).astype(np.int32)
    return np.where(n < exact, n, np.minimum(large, N_BUCKETS - 1))


def _bias_of(rel_b, rel):
    idx = jnp.asarray(np.clip(rel, 0, MAX_DISTANCE).reshape(-1, 1), jnp.int32)
    onehot = (idx == jnp.arange(MAX_DISTANCE + 1, dtype=jnp.int32)[None, :]).astype(F32)
    t = jnp.dot(onehot, rel_b * LOG2E, precision=lax.Precision.HIGHEST, preferred_element_type=F32)
    return t.reshape(rel.shape + (N_HEADS,))


def _prompt_tile(rel_b, rel, mask, minus=None):
    r = rel.shape[0]
    t = _bias_of(rel_b, rel).reshape(r, QT, KV_HEADS, GROUP).transpose(2, 0, 3, 1).reshape(KV_HEADS, r, GQ)
    if minus is not None:
        t = t - minus
    m = np.broadcast_to(np.asarray(mask)[None, :, None, :], (KV_HEADS, r, GROUP, QT)).reshape(KV_HEADS, r, GQ)
    return jnp.where(jnp.asarray(m), t, NEG)


def _sample_tile(rel_b, rel, mask):
    r, ds = rel.shape
    t = _bias_of(rel_b, rel)
    t = jnp.where(jnp.asarray(mask)[..., None], t, NEG)
    return t.transpose(2, 1, 0).reshape(N_HEADS * ds, r)


def _block_diag(w):
    eye = jnp.eye(LRU_HEADS, dtype=w.dtype)
    return jnp.einsum('hij,hk->hikj', w, eye).reshape(D_LRU, D_LRU)


def kernel(x_prompt, x_sample, cache_kv, state_kv_win, state_conv, state_h, page_table, g_mix, w_in, conv_w, conv_b,
           w_gate_a, b_gate_a, w_gate_x, b_gate_x, lru_lambda, cmp_w1, cmp_b1, cmp_w2, w_lru_out, w_nsa_out, w_out,
           g_ffn, w_router_group, b_router_group, w_router_expert, b_router_expert, w_exp_gate, w_exp_up,
           w_exp_down, rel_bias, g_final):
    assert w_in.shape[0] == 1, "single layer"
    b, s, _ = x_prompt.shape
    db, ds, _ = x_sample.shape
    npg = page_table.shape[1]
    past = npg * PAGE_SIZE
    wb = state_kv_win.shape[2]
    assert s % 256 == 0 and s // L_SEL >= N_SEL and CONV_W - 1 <= ds <= STRIDE and wb == WINDOW and past >= WINDOW

    w = w_in[0].astype(BF16)
    o = 0
    ws = []
    for width in (D_LRU, D_LRU, Q_DIM, 4 * KV_DIM, 2 * KV_DIM, 3 * N_HEADS, 2 * D_MODEL):
        ws.append(w[:, o:o + width])
        o += width
    ws[5] = jnp.pad(ws[5], ((0, 0), (0, LANES - 3 * N_HEADS)))
    g_mix2 = g_mix[0][None]
    wg = jnp.concatenate([_block_diag(w_gate_x[0]), _block_diag(w_gate_a[0])], axis=1).astype(BF16)
    lru_args = (conv_w[0], conv_b[0][None], wg, b_gate_x[0][None], b_gate_a[0][None], lru_lambda[0][None])
    w1 = cmp_w1[0].reshape(2, 2, STRIDE, HEAD_DIM, CMP_HIDDEN)
    w1f = jnp.einsum('armdn,hk->amhdkrn', w1, jnp.eye(HEADS_PER_TILE, dtype=F32)).reshape(
        2, STRIDE * LANES, HEADS_PER_TILE * 2 * CMP_HIDDEN).astype(BF16)
    b1 = cmp_b1[0][:, None, :]
    w2 = cmp_w2[0].astype(BF16)
    wl, wn, wo = w_lru_out[0].astype(BF16), w_nsa_out[0].astype(BF16), w_out[0].astype(BF16)
    n_r = N_GROUPS + N_EXPERTS
    wr = jnp.pad(jnp.concatenate([w_router_group[0], w_router_expert[0]], axis=1), ((0, 0), (0, LANES - n_r)))
    br = jnp.pad(jnp.concatenate([b_router_group[0], b_router_expert[0]]), (0, LANES - n_r))[None]
    weg, weu, wed = w_exp_gate[0].astype(BF16), w_exp_up[0].astype(BF16), w_exp_down[0].astype(BF16)
    gf, gfin = g_ffn[0][None], g_final[None]
    rel_b = rel_bias.astype(F32)[_bucket_table()]

    def mix_ffn(x2d, ylru, ynsa, mg):
        x2, hn, wt = _merge(x2d, ylru, ynsa, mg, wl, wn, wo, gf, wr, br)
        return _moe(hn, wt, x2, weg, weu, wed, gfin)

    xp2 = x_prompt.reshape(b * s, D_MODEL)
    lx, lg, q, kv, kvw, ng, mg, kv_t = _proj(xp2, g_mix2, ws, s, feature_major=(3,))
    y_lru, conv_p, h_p = _rglru_prompt(lx.reshape(b, s, D_LRU), lg.reshape(b, s, D_LRU), *lru_args)
    kv3 = kv.reshape(b, s, 4 * KV_DIM)
    kvw3 = kvw.reshape(b, s, 2 * KV_DIM)
    kcvc = _compress_prompt(kv3, w1f, b1, w2)
    nt = s // QT
    ncp = s // STRIDE
    nsb = s // L_SEL

    def key_tiles(x):
        return x.astype(BF16).reshape(b, nt, QT, KV_DIM)

    def val_tiles(x):
        return x.astype(BF16).reshape(b, nt, QT, KV_DIM).swapaxes(2, 3)

    qi = np.arange(QT)[None, :]
    lrow = np.arange(ncp)[:, None]
    rel_c = qi - STRIDE * lrow + STRIDE * ncp - (STRIDE * 8 + L_CMP - 1)
    n_far = int(np.sum(np.all(rel_c >= MAX_DISTANCE, axis=1)))
    crow = _prompt_tile(rel_b, np.full((1, QT), MAX_DISTANCE), np.ones((1, QT), bool))
    tc = jnp.concatenate([jnp.broadcast_to(crow, (KV_HEADS, n_far, GQ)),
                          _prompt_tile(rel_b, rel_c[n_far:], rel_c[n_far:] >= 0)], axis=1)
    kj = np.arange(QT)[:, None]
    diag = _prompt_tile(rel_b, qi - kj, qi - kj >= 0, crow)
    tiles = jnp.stack([diag, _prompt_tile(rel_b, WINDOW + qi - kj, qi - kj < 0, crow)])
    near = jnp.concatenate([_prompt_tile(rel_b, QT + qi - kj, np.ones((QT, QT), bool), crow), diag], axis=1)
    rr = np.arange(nsb)[:, None]
    ll = np.arange(ncp)[None, :]
    ov = jnp.asarray((ll >= 4 * rr - 1) & (ll <= 4 * rr + 3), BF16)
    qT = q.reshape(b, s, Q_DIM).swapaxes(1, 2)
    gT = ng[:, :3 * N_HEADS].reshape(b, s, 3 * N_HEADS).swapaxes(1, 2)
    bpt = QT // L_SEL
    code = (np.arange(nt)[:, None] % 2) * bpt + np.arange(QT)[None, :] // L_SEL
    mark = jnp.asarray(code[..., None] == np.arange(LANES - HEAD_DIM)[None, None, :], BF16)
    ks = kv3[..., 2 * KV_DIM:3 * KV_DIM].astype(BF16).reshape(b, nt, QT, KV_HEADS, HEAD_DIM)
    ksel_aug = jnp.concatenate([ks, jnp.broadcast_to(mark[None, :, :, None, :], ks.shape[:4] + (LANES - HEAD_DIM,))],
                               axis=-1).reshape(b, nt, QT, KV_HEADS * LANES)
    y_nsaT = _nsa_prompt(qT, gT, kcvc, ksel_aug, val_tiles(kv3[..., 3 * KV_DIM:]), key_tiles(kvw3[..., :KV_DIM]),
                         val_tiles(kvw3[..., KV_DIM:]), tc, tiles, near, ov)
    y_nsa = y_nsaT.swapaxes(1, 2).reshape(b * s, Q_DIM)
    y_prompt = mix_ffn(xp2, y_lru.reshape(b * s, D_LRU), y_nsa, mg).reshape(b, s, D_MODEL)
    kv_rows_prompt = kv_t.reshape(1, b, 4, KV_HEADS, HEAD_DIM, s).transpose(0, 1, 5, 2, 3, 4)
    win_prompt = kvw3[:, s - min(WINDOW, s):].reshape(1, b, min(WINDOW, s), 2, KV_HEADS, HEAD_DIM)

    xs2 = x_sample.reshape(db * ds, D_MODEL)
    lx, lg, q, kv, kvw, ng, mg = _proj(xs2, g_mix2, ws, db * ds)
    tmaj =lambda a: a.reshape(db, ds, D_LRU).swapaxes(0, 1)
    y_lru_t, conv_t, h_s = _rglru_sample(tmaj(lx), tmaj(lg), state_conv[0].swapaxes(0, 1), state_h[0], *lru_args)
    y_lru = y_lru_t.swapaxes(0, 1).reshape(db * ds, D_LRU)
    kv_s3 = kv.reshape(db, ds, 4 * KV_DIM)
    kvw_s3 = kvw.reshape(db, ds, 2 * KV_DIM)
    cache_t = cache_kv[0].transpose(0, 2, 3, 4, 1).reshape(cache_kv.shape[1], 4 * KV_DIM, PAGE_SIZE)
    win_t = state_kv_win[0].transpose(0, 2, 3, 4, 1).reshape(db, 2 * KV_DIM, wb)
    new_c = jnp.pad(kv_s3[..., :2 * KV_DIM], ((0, 0), (0, PAGE_SIZE - ds), (0, 0)))
    pps = 32
    assert npg % pps == 0 and LANES % (pps * PAGE_SIZE // L_SEL) == 0
    kcvc_s = _compress_sample(cache_t, page_table, new_c, w1f, b1, w2, pps)

    nsb_s = past // L_SEL + 1
    ntok = past // STRIDE
    jq = np.arange(ds)[None, :]
    nrow = np.arange(ntok)[:, None]
    rel = past + jq - STRIDE * nrow - (L_CMP - 1)
    bc = _sample_tile(rel_b, rel, rel >= 0)
    rn = np.arange(PAGE_SIZE)[:, None]
    new_rel, new_ok = jq - rn, (jq - rn >= 0) & (rn < ds)
    rw = np.arange(wb)[:, None]
    bw = _sample_tile(rel_b, np.concatenate([wb + jq - rw, new_rel], axis=0),
                      np.concatenate([(wb + jq - rw < WINDOW) & (past - wb + rw >= 0), new_ok], axis=0))
    bnew = _sample_tile(rel_b, new_rel, new_ok)
    step_keys = pps * PAGE_SIZE
    far_rel = np.full((step_keys, ds), MAX_DISTANCE)
    last_rel = far_rel.copy()
    last_rel[step_keys - PAGE_SIZE:] = PAGE_SIZE + jq - rn
    all_ok = np.ones((step_keys, ds), bool)
    bstep = jnp.stack([_sample_tile(rel_b, far_rel, all_ok), _sample_tile(rel_b, last_rel, all_ok)])
    nlane = -(-nsb_s // LANES) * LANES
    bb = np.arange(nlane)[None, :]
    tt = np.arange(ntok)[:, None]
    ovt = jnp.asarray((tt >= 4 * bb - 1) & (tt <= 4 * bb + 3) & (bb < nsb_s), BF16)
    ncol = N_HEADS * ds
    col = np.arange(ncol)
    kvh_c, j_c = col // (GROUP * ds), col % ds
    gg = jnp.asarray((kvh_c[:, None] == kvh_c[None, :]) & (j_c[:, None] == j_c[None, :]), BF16)
    bps = pps * PAGE_SIZE // L_SEL
    off = np.arange(LANES // bps)[:, None, None]
    expand = jnp.asarray(np.arange(LANES)[None, :, None] == bps * off + np.arange(step_keys)[None, None, :] // L_SEL,
                         BF16)
    q5 = q.reshape(db, ds, KV_HEADS, GROUP, HEAD_DIM) * QK_SCALE
    qbd = jnp.einsum('bjkgd,kc->bkgjcd', q5, jnp.eye(KV_HEADS, dtype=F32)).reshape(db, ncol, KV_DIM).astype(BF16)
    gates_s = ng[:, :3 * N_HEADS].reshape(db, ds, N_HEADS, 3).transpose(0, 2, 1, 3).reshape(db, ncol, 3)
    new_all = jnp.pad(jnp.concatenate([kv_s3[..., 2 * KV_DIM:], kvw_s3], axis=-1),
                      ((0, 0), (0, PAGE_SIZE - ds), (0, 0)))
    o_bd = _nsa_sample(cache_t, page_table, qbd, gates_s, kcvc_s, win_t, new_all, bc, bw, bstep, bnew, ovt, gg,
                       expand, pps)
    o6 = o_bd.reshape(db, KV_HEADS, GROUP, ds, KV_HEADS, HEAD_DIM)
    y_nsa = jnp.einsum('bkgjkd->bjkgd', o6).reshape(db * ds, Q_DIM)
    y_sample = mix_ffn(xs2, y_lru, y_nsa, mg).reshape(db, ds, D_MODEL)
    kv_rows_sample = kv_s3.reshape(1, db, ds, 4, KV_HEADS, HEAD_DIM)
    win_all = jnp.concatenate([state_kv_win[0].reshape(db, wb, 2 * KV_DIM), kvw_s3], axis=1)
    win_sample = win_all[:, win_all.shape[1] - WINDOW:].reshape(1, db, WINDOW, 2, KV_HEADS, HEAD_DIM)
    conv_s = conv_t.swapaxes(0, 1)

    return (y_prompt, y_sample, kv_rows_prompt, win_prompt, conv_p[None], h_p.reshape(1, b, D_LRU),
            kv_rows_sample, win_sample, conv_s[None], h_s[None])
```

```python
import functools
import math

import numpy as np
import jax
import jax.numpy as jnp
from jax import lax
from jax.experimental import pallas as pl
from jax.experimental.pallas import tpu as pltpu

F32 = jnp.float32
BF16 = jnp.bfloat16

D_MODEL = 1024
D_LRU = 1280
LRU_HEADS = 16
LRU_BLOCK = D_LRU // LRU_HEADS
CONV_W = 4
LRU_C = 8.0
N_HEADS = 16
HEAD_DIM = 64
KV_HEADS = 4
GROUP = N_HEADS // KV_HEADS
L_CMP = 32
STRIDE = 16
CMP_HIDDEN = 128
L_SEL = 64
N_SEL = 16
N_LOCAL = 2
WINDOW = 512
PAGE_SIZE = 128
N_BUCKETS = 32
MAX_DISTANCE = 128
N_GROUPS = 4
EXPERTS_PER_GROUP = 8
N_EXPERTS = N_GROUPS * EXPERTS_PER_GROUP
D_EXPERT = 256
EPS = 1e-6
NEG = -1e30
FORCED = 1e9
Q_DIM = N_HEADS * HEAD_DIM
KV_DIM = KV_HEADS * HEAD_DIM
QT = 128
GQ = GROUP * QT
LANES = 128
SUB = 8
VMEM_LIMIT = 56 * 1024 * 1024
KNOCKED = -3e38
ABSENT = -2e38
LOG2E = math.log2(math.e)
QK_SCALE = HEAD_DIM ** -0.5 * LOG2E
MASK_ROWS = 16
ONES_ROWS = 16


def _cparams(sem):
    return pltpu.CompilerParams(dimension_semantics=sem, vmem_limit_bytes=VMEM_LIMIT)


def _dot(a, b):
    return jnp.dot(a, b, preferred_element_type=F32)


def _dot_tn(a, b):
    return lax.dot_general(a, b, (((0,), (0,)), ((), ())), preferred_element_type=F32)


def _dot_nt(a, b):
    return lax.dot_general(a, b, (((1,), (1,)), ((), ())), preferred_element_type=F32)


def _split3(x):
    hi = x.astype(BF16)
    r1 = x - hi.astype(F32)
    mid = r1.astype(BF16)
    lo = (r1 - mid.astype(F32)).astype(BF16)
    return hi, mid, lo


def _dot_exact_rhs(a_bf16, x):
    hi, mid, lo = _split3(x)
    return _dot(a_bf16, hi) + _dot(a_bf16, mid) + _dot(a_bf16, lo)


def _dot_exact_lhs(x, b_bf16):
    hi, mid, lo = _split3(x)
    return _dot(hi, b_bf16) + _dot(mid, b_bf16) + _dot(lo, b_bf16)


def _proj_kernel(x_ref, g_ref, *refs, n, feature_major):
    x = x_ref[...]
    xn = x * lax.rsqrt(jnp.mean(x * x, axis=-1, keepdims=True) + EPS) * g_ref[...]
    xb = xn.astype(BF16)
    outs = {}
    for j, (w_ref, o_ref) in enumerate(zip(refs[:n], refs[n:2 * n])):
        outs[j] = _dot(xb, w_ref[...])
        o_ref[...] = outs[j]
    for j, o_ref in zip(feature_major, refs[2 * n:]):
        o_ref[0] = outs[j].T


def _proj(x2d, g, ws, rows_per_seq, feature_major=(), tm=256):
    t = x2d.shape[0]
    tm = min(tm, t, rows_per_seq)
    tps = rows_per_seq // tm
    in_specs = [pl.BlockSpec((tm, D_MODEL), lambda i: (i, 0)), pl.BlockSpec((1, D_MODEL), lambda i: (0, 0))]
    in_specs += [pl.BlockSpec(w.shape, lambda i: (0, 0), pipeline_mode=pl.Buffered(1)) for w in ws]
    out_specs = [pl.BlockSpec((tm, w.shape[1]), lambda i: (i, 0)) for w in ws]
    out_shape = [jax.ShapeDtypeStruct((t, w.shape[1]), F32) for w in ws]
    for j in feature_major:
        out_specs.append(pl.BlockSpec((1, ws[j].shape[1], tm), lambda i: (i // tps, 0, i % tps)))
        out_shape.append(jax.ShapeDtypeStruct((t // rows_per_seq, ws[j].shape[1], rows_per_seq), F32))
    return pl.pallas_call(functools.partial(_proj_kernel, n=len(ws), feature_major=tuple(feature_major)),
                          grid=(t // tm,), in_specs=in_specs, out_specs=out_specs, out_shape=out_shape,
                          compiler_params=_cparams(("parallel",)), name="proj")(x2d, g, *ws)


def _softplus(x):
    return jnp.maximum(x, 0.0) + jnp.log1p(jnp.exp(-jnp.abs(x)))


def _lru_gates(xc, wg_ref, bgx_ref, bga_ref, lam_ref):
    gates = _dot(xc.astype(BF16), wg_ref[...])
    gx = jax.nn.sigmoid(gates[:, :D_LRU] + bgx_ref[...])
    ga = jax.nn.sigmoid(gates[:, D_LRU:] + bga_ref[...])
    log_a = -LRU_C * ga * _softplus(-lam_ref[...])
    a = jnp.exp(log_a)
    th = jnp.tanh(log_a)
    u = jnp.sqrt(-2.0 * th / (1.0 - th)) * (gx * xc)
    return a, u


def _rglru_prompt_kernel(lx_ref, lg_ref, cw_ref, cb_ref, wg_ref, bgx_ref, bga_ref, lam_ref,
                         y_ref, conv_ref, h_ref, xext, hc, *, tc):
    t = pl.program_id(1)

    @pl.when(t == 0)
    def _():
        xext[0:8, :] = jnp.zeros((8, D_LRU), F32)
        hc[...] = jnp.zeros_like(hc)

    x = lx_ref[0]
    xext[8:8 + tc, :] = x
    cw = cw_ref[...]
    xc = cb_ref[...] + cw[3:4] * x
    for j in range(CONV_W - 1):
        xc = xc + cw[j:j + 1] * xext[5 + j:5 + j + tc, :]
    a, u = _lru_gates(xc, wg_ref, bgx_ref, bga_ref, lam_ref)
    sub = lax.broadcasted_iota(jnp.int32, (tc, D_LRU), 0) % SUB
    s = 1
    while s < SUB:
        a_sh = pltpu.roll(a, s, 0)
        u_sh = pltpu.roll(u, s, 0)
        m = sub >= s
        u = jnp.where(m, a * u_sh + u, u)
        a = jnp.where(m, a * a_sh, a)
        s *= 2
    carry = hc[0:1, :]
    lg = lg_ref[0]
    for g in range(tc // SUB):
        rows = slice(g * SUB, (g + 1) * SUB)
        h = a[rows] * carry + u[rows]
        carry = h[SUB - 1:SUB, :]
        y_ref[0, rows, :] = h * jax.nn.gelu(lg[rows])
    hc[0:1, :] = carry
    xext[0:8, :] = x[tc - 8:tc, :]
    conv_ref[0] = x[tc - (CONV_W - 1):tc, :]
    h_ref[0] = carry


def _rglru_prompt(lx, lg, cw, cb, wg, bgx, bga, lam, tc=256):
    b, s, _ = lx.shape
    row = lambda shape: pl.BlockSpec(shape, lambda bi, ti: (0, 0))
    return pl.pallas_call(
        functools.partial(_rglru_prompt_kernel, tc=tc), grid=(b, s // tc),
        in_specs=[pl.BlockSpec((1, tc, D_LRU), lambda bi, ti: (bi, ti, 0)),
                  pl.BlockSpec((1, tc, D_LRU), lambda bi, ti: (bi, ti, 0)),
                  row((CONV_W, D_LRU)), row((1, D_LRU)), row((D_LRU, 2 * D_LRU)),
                  row((1, D_LRU)), row((1, D_LRU)), row((1, D_LRU))],
        out_specs=[pl.BlockSpec((1, tc, D_LRU), lambda bi, ti: (bi, ti, 0)),
                   pl.BlockSpec((1, CONV_W - 1, D_LRU), lambda bi, ti: (bi, 0, 0)),
                   pl.BlockSpec((1, 1, D_LRU), lambda bi, ti: (bi, 0, 0))],
        out_shape=[jax.ShapeDtypeStruct((b, s, D_LRU), F32),
                   jax.ShapeDtypeStruct((b, CONV_W - 1, D_LRU), F32),
                   jax.ShapeDtypeStruct((b, 1, D_LRU), F32)],
        scratch_shapes=[pltpu.VMEM((tc + 8, D_LRU), F32), pltpu.VMEM((8, D_LRU), F32)],
        compiler_params=_cparams(("parallel", "arbitrary")), name="rglru_prompt",
    )(lx, lg, cw, cb, wg, bgx, bga, lam)


def _rglru_sample_kernel(lx_ref, lg_ref, cbuf_ref, h0_ref, cw_ref, cb_ref, wg_ref, bgx_ref, bga_ref, lam_ref,
                         y_ref, conv_ref, h_ref, *, ds):
    cw = cw_ref[...]
    xp = [cbuf_ref[j] for j in range(CONV_W - 1)] + [lx_ref[j] for j in range(ds)]
    h = h0_ref[...]
    for t in range(ds):
        xc = cb_ref[...]
        for j in range(CONV_W):
            xc = xc + cw[j:j + 1] * xp[t + j]
        a, u = _lru_gates(xc, wg_ref, bgx_ref, bga_ref, lam_ref)
        h = a * h + u
        y_ref[t] = h * jax.nn.gelu(lg_ref[t])
    for j in range(CONV_W - 1):
        conv_ref[j] = xp[ds + j]
    h_ref[...] = h


def _rglru_sample(lx_t, lg_t, cbuf_t, h0, cw, cb, wg, bgx, bga, lam):
    ds, n, _ = lx_t.shape
    full = lambda a: pl.BlockSpec(a.shape, lambda i: (0,) * a.ndim)
    args = (lx_t, lg_t, cbuf_t, h0, cw, cb, wg, bgx, bga, lam)
    out_shape = [jax.ShapeDtypeStruct((ds, n, D_LRU), F32), jax.ShapeDtypeStruct((CONV_W - 1, n, D_LRU), F32),
                 jax.ShapeDtypeStruct((n, D_LRU), F32)]
    return pl.pallas_call(
        functools.partial(_rglru_sample_kernel, ds=ds), grid=(1,),
        in_specs=[full(a) for a in args], out_specs=[full(o) for o in out_shape], out_shape=out_shape,
        compiler_params=_cparams(("arbitrary",)), name="rglru_sample")(*args)


HEADS_PER_TILE = LANES // HEAD_DIM


def _chunk_project(load, w):
    feat = jnp.concatenate([load(s) for s in range(STRIDE)], axis=1).astype(BF16)
    return _dot(feat, w)


def _compress_prompt_kernel(xa_ref, xb_ref, w1_ref, b1_ref, w2_ref, o_ref, *, ncp):
    rows = lax.broadcasted_iota(jnp.int32, (ncp, HEAD_DIM), 0)
    for c, x_ref in enumerate((xa_ref, xb_ref)):
        p2 = _chunk_project(lambda s: x_ref[0, pl.ds(s, ncp, stride=STRIDE), :], w1_ref[0])
        for hh in range(HEADS_PER_TILE):
            k = c * HEADS_PER_TILE + hh
            p = p2[:, hh * 2 * CMP_HIDDEN:(hh + 1) * 2 * CMP_HIDDEN]
            pre = b1_ref[0] + p[:, :CMP_HIDDEN] + pltpu.roll(p[:, CMP_HIDDEN:], ncp - 1, 0)
            phi = _dot(jax.nn.gelu(pre).astype(BF16), w2_ref[0])
            o_ref[0, 0, k, 0:ncp, :] = jnp.zeros((ncp, HEAD_DIM), F32)
            o_ref[0, 0, k, ncp:2 * ncp, :] = jnp.where(rows < ncp - 1, phi, 0.0)


def _compress_prompt(kv, w1f, b1, w2):
    b, s, _ = kv.shape
    ncp = s // STRIDE
    return pl.pallas_call(
        functools.partial(_compress_prompt_kernel, ncp=ncp), grid=(b, 2),
        in_specs=[pl.BlockSpec((1, s, LANES), lambda bi, sl: (bi, 0, 2 * sl)),
                  pl.BlockSpec((1, s, LANES), lambda bi, sl: (bi, 0, 2 * sl + 1)),
                  pl.BlockSpec((1,) + w1f.shape[1:], lambda bi, sl: (sl, 0, 0)),
                  pl.BlockSpec((1, 1, CMP_HIDDEN), lambda bi, sl: (sl, 0, 0)),
                  pl.BlockSpec((1, CMP_HIDDEN, HEAD_DIM), lambda bi, sl: (sl, 0, 0))],
        out_specs=pl.BlockSpec((1, 1, KV_HEADS, 2 * ncp, HEAD_DIM), lambda bi, sl: (sl, bi, 0, 0, 0)),
        out_shape=jax.ShapeDtypeStruct((2, b, KV_HEADS, 2 * ncp, HEAD_DIM), F32),
        compiler_params=_cparams(("parallel", "parallel")), name="compress_prompt")(kv, kv, w1f, b1, w2)


def _compress_sample_kernel(pt_ref, *refs, pps, nsteps):
    pages = refs[:pps]
    new_ref, w1_ref, b1_ref, w2_ref, o_ref, xs, pscr = refs[pps:]
    st = pl.program_id(1)
    cps = PAGE_SIZE // STRIDE
    m = pps * cps
    ntok = nsteps * m
    tiles_per_slot = KV_DIM // LANES

    def stage(ref, j, transposed):
        for c in range(2 * tiles_per_slot):
            if transposed:
                blk = ref[0, c * LANES:(c + 1) * LANES, :].T
            else:
                blk = ref[0, :, c * LANES:(c + 1) * LANES]
            xs[c, j * PAGE_SIZE:(j + 1) * PAGE_SIZE, :] = blk

    def project(nrows, row0):
        for sl in range(2):
            for c in range(tiles_per_slot):
                p2 = _chunk_project(lambda s: xs[sl * tiles_per_slot + c, pl.ds(s, nrows, stride=STRIDE), :],
                                    w1_ref[sl])
                for hh in range(HEADS_PER_TILE):
                    pscr[sl, c * HEADS_PER_TILE + hh, pl.ds(row0, nrows), :] = (
                        p2[:, hh * 2 * CMP_HIDDEN:(hh + 1) * 2 * CMP_HIDDEN])

    for j, pg in enumerate(pages):
        stage(pg, j, True)
    project(m, pl.multiple_of(st * m, m))

    @pl.when(st == nsteps - 1)
    def _():
        stage(new_ref, 0, False)
        project(cps, ntok)
        for sl in range(2):
            toks = []
            for k in range(KV_HEADS):
                pre = (b1_ref[sl] + pscr[sl, k, 0:ntok, 0:CMP_HIDDEN]
                       + pscr[sl, k, 1:ntok + 1, CMP_HIDDEN:2 * CMP_HIDDEN])
                toks.append(_dot(jax.nn.gelu(pre).astype(BF16), w2_ref[sl]))
            o_ref[0, sl] = jnp.concatenate(toks, axis=1)


def _compress_sample(cache, page_table, new_rows, w1f, b1, w2, pps=16):
    db, npg = page_table.shape
    pps = min(pps, npg)
    nsteps = npg // pps
    cps = PAGE_SIZE // STRIDE
    ntok = npg * cps

    def page_spec(j):
        return pl.BlockSpec((1, 2 * KV_DIM, PAGE_SIZE), lambda bi, st, pt: (pt[bi, st * pps + j], 0, 0))

    const = lambda shape: pl.BlockSpec(shape, lambda bi, st, pt: (0,) * len(shape))
    grid_spec = pltpu.PrefetchScalarGridSpec(
        num_scalar_prefetch=1, grid=(db, nsteps),
        in_specs=[page_spec(j) for j in range(pps)] + [
            pl.BlockSpec((1, PAGE_SIZE, 2 * KV_DIM), lambda bi, st, pt: (bi, 0, 0)),
            const(w1f.shape), const((2, 1, CMP_HIDDEN)),
            const((2, CMP_HIDDEN, HEAD_DIM))],
        out_specs=pl.BlockSpec((1, 2, ntok, KV_DIM), lambda bi, st, pt: (bi, 0, 0, 0)),
        scratch_shapes=[pltpu.VMEM((2 * KV_DIM // LANES, pps * PAGE_SIZE, LANES), F32),
                        pltpu.VMEM((2, KV_HEADS, ntok + cps, 2 * CMP_HIDDEN), F32)])
    return pl.pallas_call(
        functools.partial(_compress_sample_kernel, pps=pps, nsteps=nsteps), grid_spec=grid_spec,
        out_shape=jax.ShapeDtypeStruct((db, 2, ntok, KV_DIM), F32),
        compiler_params=_cparams(("parallel", "arbitrary")), name="compress_sample",
    )(page_table, *([cache] * pps), new_rows, w1f, b1, w2)


def _topk_masks(scores, axis):
    n = scores[0].shape[axis]
    idx = lax.broadcasted_iota(jnp.int32, scores[0].shape, axis).astype(F32)

    def body(_, scs):
        out = []
        for sc in scs:
            mx = jnp.max(sc, axis=axis, keepdims=True)
            first = jnp.min(jnp.where(sc == mx, idx, float(n)), axis=axis, keepdims=True)
            out.append(jnp.where(idx == first, KNOCKED, sc))
        return tuple(out)

    return [jnp.where(sc == KNOCKED, 1.0, 0.0) for sc in lax.fori_loop(0, N_SEL, body, tuple(scores))]


def _softmax2(s, axis):
    m = jnp.max(s, axis=axis, keepdims=True)
    e = jnp.where(s > 0.5 * NEG, jnp.exp2(s - m), 0.0)
    l = jnp.sum(e, axis=axis, keepdims=True)
    return e * (1.0 / jnp.maximum(l, 1e-30))


def _nsa_prompt_kernel(qT_ref, gT_ref, kc_ref, vc_ref, ksel_ref, vselT_ref, kwin_ref, vwinT_ref,
                       tc_ref, tiles_ref, near_ref, ov_ref, y_ref,
                       qk_s, madd_s, oc_s, m_s, acc_s, s_scr, p_scr, alpha_s, *, ncp, nsb):
    i = pl.program_id(1)
    qT = qT_ref[0]
    for k in range(KV_HEADS):
        qk_s[k, 0:HEAD_DIM, :] = (jnp.concatenate([qT[(GROUP * k + g) * HEAD_DIM:(GROUP * k + g + 1) * HEAD_DIM, :]
                                                   for g in range(GROUP)], axis=1) * QK_SCALE).astype(BF16)
        qk_s[k, HEAD_DIM:LANES, :] = jnp.zeros((LANES - HEAD_DIM, GQ), BF16)
    m_s[...] = jnp.full(m_s.shape, NEG, F32)
    acc_s[...] = jnp.zeros_like(acc_s)

    start = pl.multiple_of(8 * i + 8, 8)
    tok_ok = lax.broadcasted_iota(jnp.int32, (ncp, 1), 0) >= ncp - 8 - 8 * i
    rblk = lax.broadcasted_iota(jnp.int32, (nsb, QT), 0)
    qhalf = jnp.where(lax.broadcasted_iota(jnp.int32, (nsb, QT), 1) >= L_SEL, 1, 0)
    r_qb = nsb - 2 + qhalf
    exists = rblk >= nsb - 2 - 2 * i
    forced = (rblk == nsb - 2 - 2 * i) | (rblk > r_qb - N_LOCAL)
    def block_scores(k):
        kc = kc_ref[0, 0, k, pl.ds(start, ncp), :].astype(BF16)
        vc = vc_ref[0, 0, k, pl.ds(start, ncp), :].astype(BF16)
        s = jnp.where(tok_ok, _dot(kc, qk_s[k, 0:HEAD_DIM, :]) + tc_ref[k], NEG)
        pn = _softmax2(s, 0)
        oc_s[k] = _dot_tn(vc, pn.astype(BF16))
        psum = pn[:, 0:QT]
        for g in range(1, GROUP):
            psum = psum + pn[:, g * QT:(g + 1) * QT]
        imp = _dot_exact_rhs(ov_ref[...], psum)
        score = jnp.where(rblk > r_qb, -FORCED, jnp.where(forced, FORCED, imp))
        return jnp.where(exists, score, ABSENT)

    for k0 in range(0, KV_HEADS, 2):
        for j, sel in enumerate(_topk_masks([block_scores(k0), block_scores(k0 + 1)], 0)):
            madd_s[k0 + j] = (sel - 1.0) * (-NEG)

    bpt = QT // L_SEL

    def logits(t, n, k_ref, slot, masked):
        kt = k_ref[0, t] if n == 1 else jnp.concatenate([k_ref[0, t + j] for j in range(n)], axis=0)
        for k in range(KV_HEADS):
            if masked:
                r0 = bpt * (t - i) + nsb - bpt
                par = t % 2
                first = [r0 + (bpt * par if n == 2 else 0) + h for h in range(bpt)]
                second = [r0 + (bpt * (1 - par) if n == 2 else 0) + h for h in range(bpt)]
                rows = [jnp.concatenate([madd_s[k, pl.ds(r, 1), :]] * GROUP, axis=1) for r in first + second]
                blk = jnp.concatenate(rows + [jnp.zeros((MASK_ROWS - 2 * bpt, GQ), F32)], axis=0)
                qk_s[k, HEAD_DIM:HEAD_DIM + MASK_ROWS, :] = blk.astype(BF16)
                s_scr[slot, k, 0:n * QT, :] = _dot(kt[:, k * LANES:(k + 1) * LANES], qk_s[k])
            else:
                s_scr[slot, k, 0:n * QT, :] = _dot(kt[:, k * HEAD_DIM:(k + 1) * HEAD_DIM], qk_s[k, 0:HEAD_DIM, :])

    def softmax(n, add_of, branch, slot):
        nk = n * QT
        for k in range(KV_HEADS):
            idx = branch * KV_HEADS + k
            s = s_scr[slot, k, 0:nk, :]
            if add_of is not None:
                s = s + add_of(k)
            m_old = m_s[idx]
            m_new = jnp.maximum(m_old, jnp.max(s, axis=0, keepdims=True))
            alpha_s[slot, k] = jnp.exp2(m_old - m_new)
            m_s[idx] = m_new
            p_scr[slot, k, 0:nk, :] = jnp.exp2((s - m_new).astype(BF16))

    def values(t, n, vT_ref, branch, slot):
        nk = n * QT
        vt = vT_ref[0, t] if n == 1 else jnp.concatenate([vT_ref[0, t + j] for j in range(n)], axis=1)
        ones = jnp.ones((ONES_ROWS, nk), BF16)
        for k in range(KV_HEADS):
            idx = branch * KV_HEADS + k
            vt_aug = jnp.concatenate([vt[k * HEAD_DIM:(k + 1) * HEAD_DIM, :], ones], axis=0)
            acc_s[idx] = alpha_s[slot, k] * acc_s[idx] + _dot(vt_aug, p_scr[slot, k, 0:nk, :])

    def attend(t, n, k_ref, vT_ref, add_of, branch):
        logits(t, n, k_ref, 0, branch == 0)
        softmax(n, add_of, branch, 0)
        values(t, n, vT_ref, branch, 0)

    nfar = jnp.maximum(i - 1, 0)
    npair = nfar // 2

    last_pair = 2 * (npair - 1)

    @pl.when(npair >= 2)
    def _():
        logits(0, 2, ksel_ref, 0, True)
        p_scr[1] = jnp.zeros(p_scr.shape[1:], BF16)
        alpha_s[1] = jnp.ones(alpha_s.shape[1:], F32)

    def far_body(u, c):
        ta = 4 * u
        logits(ta + 2, 2, ksel_ref, 1, True)
        values(jnp.maximum(ta - 2, 0), 2, vselT_ref, 0, 1)
        softmax(2, None, 0, 0)
        logits(jnp.minimum(ta + 4, last_pair), 2, ksel_ref, 0, True)
        values(ta, 2, vselT_ref, 0, 0)
        softmax(2, None, 0, 1)
        return c

    ntrip = npair // 2
    lax.fori_loop(0, ntrip, far_body, 0)

    @pl.when(ntrip > 0)
    def _():
        values(4 * ntrip - 2, 2, vselT_ref, 0, 1)

    @pl.when(npair % 2 == 1)
    def _():
        attend(last_pair, 2, ksel_ref, vselT_ref, None, 0)

    @pl.when(nfar % 2 == 1)
    def _():
        attend(nfar - 1, 1, ksel_ref, vselT_ref, None, 0)

    near = lambda k: near_ref[k]
    win_edge = (i - 4, 1, kwin_ref, vwinT_ref, lambda k: tiles_ref[1, k], 1)
    win_far = (i - 3, 2, kwin_ref, vwinT_ref, None, 1)
    sel_near = (i - 1, 2, ksel_ref, vselT_ref, near, 0)
    win_near = (i - 1, 2, kwin_ref, vwinT_ref, near, 1)

    @pl.when(i >= 4)
    def _():
        items = (win_edge, win_far, sel_near, win_near)
        logits(items[0][0], items[0][1], items[0][2], 0, items[0][5] == 0)
        for j, (t, n, k_ref, vT_ref, add_of, branch) in enumerate(items):
            slot = j % 2
            if j + 1 < len(items):
                tn, nn, kn = items[j + 1][:3]
                logits(tn, nn, kn, 1 - slot, items[j + 1][5] == 0)
            if j >= 1:
                tp, np_, _, vp, _, bp = items[j - 1]
                values(tp, np_, vp, bp, 1 - slot)
            softmax(n, add_of, branch, slot)
        tp, np_, _, vp, _, bp = items[-1]
        values(tp, np_, vp, bp, (len(items) - 1) % 2)

    @pl.when(i == 3)
    def _():
        attend(*win_far)

    @pl.when(i == 2)
    def _():
        attend(0, 1, kwin_ref, vwinT_ref, None, 1)

    @pl.when((i >= 1) & (i < 4))
    def _():
        attend(*sel_near)
        attend(*win_near)

    @pl.when(i == 0)
    def _():
        attend(0, 1, ksel_ref, vselT_ref, lambda k: tiles_ref[0, k], 0)
        attend(0, 1, kwin_ref, vwinT_ref, lambda k: tiles_ref[0, k], 1)

    gate = jax.nn.sigmoid(gT_ref[0])
    for k in range(KV_HEADS):
        o_s = acc_s[k, 0:HEAD_DIM] * (1.0 / acc_s[k, HEAD_DIM:HEAD_DIM + 1])
        o_w = acc_s[KV_HEADS + k, 0:HEAD_DIM] * (1.0 / acc_s[KV_HEADS + k, HEAD_DIM:HEAD_DIM + 1])
        o_c = oc_s[k]
        for g in range(GROUP):
            h = GROUP * k + g
            cols = slice(g * QT, (g + 1) * QT)
            y_ref[0, h * HEAD_DIM:(h + 1) * HEAD_DIM, :] = (
                gate[3 * h:3 * h + 1] * o_c[:, cols] + gate[3 * h + 1:3 * h + 2] * o_s[:, cols]
                + gate[3 * h + 2:3 * h + 3] * o_w[:, cols])


def _nsa_prompt(qT, gT, kcvc, ksel, vselT, kwin, vwinT, tc, tiles, near, ov):
    b, _, s = qT.shape
    nq = s // QT
    ncp = s // STRIDE
    nsb = s // L_SEL
    seq4 = lambda a: pl.BlockSpec((1,) + a.shape[1:], lambda bi, qi: (bi, 0, 0, 0), pipeline_mode=pl.Buffered(1))
    const = lambda a: pl.BlockSpec(a.shape, lambda bi, qi: (0,) * a.ndim, pipeline_mode=pl.Buffered(1))
    cmp_spec = lambda sl: pl.BlockSpec((1, 1, KV_HEADS, 2 * ncp, HEAD_DIM), lambda bi, qi: (sl, bi, 0, 0, 0),
                                       pipeline_mode=pl.Buffered(1))
    return pl.pallas_call(
        functools.partial(_nsa_prompt_kernel, ncp=ncp, nsb=nsb), grid=(b, nq),
        in_specs=[pl.BlockSpec((1, Q_DIM, QT), lambda bi, qi: (bi, 0, qi)),
                  pl.BlockSpec((1, 3 * N_HEADS, QT), lambda bi, qi: (bi, 0, qi)),
                  cmp_spec(0), cmp_spec(1), seq4(ksel), seq4(vselT), seq4(kwin), seq4(vwinT),
                  const(tc), const(tiles), const(near), const(ov)],
        out_specs=pl.BlockSpec((1, Q_DIM, QT), lambda bi, qi: (bi, 0, qi)),
        out_shape=jax.ShapeDtypeStruct((b, Q_DIM, s), F32),
        scratch_shapes=[pltpu.VMEM((KV_HEADS, LANES, GQ), BF16), pltpu.VMEM((KV_HEADS, nsb, QT), F32),
                        pltpu.VMEM((KV_HEADS, HEAD_DIM, GQ), F32), pltpu.VMEM((2 * KV_HEADS, 1, GQ), F32),
                        pltpu.VMEM((2 * KV_HEADS, HEAD_DIM + ONES_ROWS, GQ), F32),
                        pltpu.VMEM((2, KV_HEADS, 2 * QT, GQ), F32), pltpu.VMEM((2, KV_HEADS, 2 * QT, GQ), BF16),
                        pltpu.VMEM((2, KV_HEADS, 1, GQ), F32)],
        compiler_params=_cparams(("parallel", "arbitrary")), name="nsa_prompt",
    )(qT, gT, kcvc, kcvc, ksel, vselT, kwin, vwinT, tc, tiles, near, ov)


def _nsa_sample_kernel(pt_ref, *refs, pps, nsteps, nsb):
    pages = refs[:pps]
    (qbd_ref, gate_ref, kcvc_ref, win_ref, new_ref, bc_ref, bw_ref, bstep_ref, bnew_ref, ovt_ref, gg_ref, exp_ref,
     o_ref, madd_s, oc_s, ow_s, m_s, l_s, acc_s) = refs[pps:]
    st = pl.program_id(1)
    qbd = qbd_ref[0]
    ncol = qbd.shape[0]
    wb = win_ref.shape[2]
    bps = pps * (PAGE_SIZE // L_SEL)
    nlane = ovt_ref.shape[1]

    def online(s, pv_of):
        m_old = m_s[...]
        m_new = jnp.maximum(m_old, jnp.max(s, axis=1, keepdims=True))
        alpha = jnp.exp2(m_old - m_new)
        p = jnp.exp2(s - m_new)
        l_s[...] = alpha * l_s[...] + jnp.sum(p, axis=1, keepdims=True)
        acc_s[...] = alpha * acc_s[...] + pv_of(p.astype(BF16))
        m_s[...] = m_new

    @pl.when(st == 0)
    def _():
        m_s[...] = jnp.full(m_s.shape, NEG, F32)
        l_s[...] = jnp.zeros_like(l_s)
        acc_s[...] = jnp.zeros_like(acc_s)
        pn = _softmax2(_dot_nt(qbd, kcvc_ref[0, 0].astype(BF16)) + bc_ref[...], 1)
        oc_s[...] = _dot(pn.astype(BF16), kcvc_ref[0, 1].astype(BF16))
        imp = sum(_dot_tn(part, gg_ref[...]) for part in _split3(_dot_exact_lhs(pn, ovt_ref[...])))
        nrow = -(-nsb // 8) * 8
        blk = lax.broadcasted_iota(jnp.int32, (nrow, ncol), 0)
        qb = nsb - 1
        forced = (blk == 0) | (blk > qb - N_LOCAL)
        score = jnp.where(blk > qb, -FORCED, jnp.where(forced, FORCED, imp[0:nrow]))
        score = jnp.where(blk < nsb, score, ABSENT)
        madd = jnp.concatenate([jnp.where(_topk_masks([score], 0)[0] > 0.0, 0.0, NEG),
                                jnp.full((nlane - nrow, ncol), NEG, F32)], axis=0).astype(BF16)
        for tl in range(nlane // LANES):
            madd_s[tl] = madd[tl * LANES:(tl + 1) * LANES, :]
        new = new_ref[0]
        s_w = jnp.concatenate([_dot(qbd, win_ref[0, 0:KV_DIM, :].astype(BF16)),
                               _dot_nt(qbd, new[:, 2 * KV_DIM:3 * KV_DIM].astype(BF16))], axis=1) + bw_ref[...]
        pw = _softmax2(s_w, 1).astype(BF16)
        ow_s[...] = (_dot_nt(pw[:, 0:wb], win_ref[0, KV_DIM:2 * KV_DIM, :].astype(BF16))
                     + _dot(pw[:, wb:], new[:, 3 * KV_DIM:4 * KV_DIM].astype(BF16)))

    kt = jnp.concatenate([pg[0, 0:KV_DIM, :] for pg in pages], axis=1).astype(BF16)
    vt = jnp.concatenate([pg[0, KV_DIM:2 * KV_DIM, :] for pg in pages], axis=1).astype(BF16)
    b0 = st * bps
    mask = _dot_tn(madd_s[b0 // LANES], exp_ref[(b0 % LANES) // bps])
    bias = jnp.where(st == nsteps - 1, bstep_ref[1], bstep_ref[0])
    online(_dot(qbd, kt) + mask + bias, lambda p: _dot_nt(p, vt))

    @pl.when(st == nsteps - 1)
    def _():
        new = new_ref[0]
        lb = nsb - 1
        spread = (lax.broadcasted_iota(jnp.int32, (LANES, PAGE_SIZE), 0) == lb % LANES).astype(BF16)
        online(_dot_nt(qbd, new[:, 0:KV_DIM].astype(BF16)) + bnew_ref[...] + _dot_tn(madd_s[lb // LANES], spread),
               lambda p: _dot(p, new[:, KV_DIM:2 * KV_DIM].astype(BF16)))
        gate = jax.nn.sigmoid(gate_ref[0])
        o_ref[0] = (gate[:, 0:1] * oc_s[...] + gate[:, 1:2] * (acc_s[...] * (1.0 / l_s[...]))
                    + gate[:, 2:3] * ow_s[...])


def _nsa_sample(cache, page_table, qbd, gates, kcvc, win, new_rows, bc, bw, bstep, bnew, ovt, gg, expand, pps):
    db, npg = page_table.shape
    nsteps = npg // pps
    nsb = npg * (PAGE_SIZE // L_SEL) + 1
    ncol = qbd.shape[1]
    nlane = ovt.shape[1]

    def page_spec(j):
        return pl.BlockSpec((1, 2 * KV_DIM, PAGE_SIZE), lambda bi, st, pt: (pt[bi, st * pps + j], 1, 0))

    const = lambda a: pl.BlockSpec(a.shape, lambda bi, st, pt: (0,) * a.ndim)
    seq = lambda a: pl.BlockSpec((1,) + a.shape[1:], lambda bi, st, pt: (bi,) + (0,) * (a.ndim - 1))
    grid_spec = pltpu.PrefetchScalarGridSpec(
        num_scalar_prefetch=1, grid=(db, nsteps),
        in_specs=[page_spec(j) for j in range(pps)] + [seq(qbd), seq(gates), seq(kcvc), seq(win), seq(new_rows)]
        + [const(a) for a in (bc, bw, bstep, bnew, ovt, gg, expand)],
        out_specs=pl.BlockSpec((1, ncol, KV_DIM), lambda bi, st, pt: (bi, 0, 0)),
        scratch_shapes=[pltpu.VMEM((nlane // LANES, LANES, ncol), BF16), pltpu.VMEM((ncol, KV_DIM), F32),
                        pltpu.VMEM((ncol, KV_DIM), F32), pltpu.VMEM((ncol, 1), F32), pltpu.VMEM((ncol, 1), F32),
                        pltpu.VMEM((ncol, KV_DIM), F32)])
    return pl.pallas_call(
        functools.partial(_nsa_sample_kernel, pps=pps, nsteps=nsteps, nsb=nsb), grid_spec=grid_spec,
        out_shape=jax.ShapeDtypeStruct((db, ncol, KV_DIM), F32),
        compiler_params=_cparams(("parallel", "arbitrary")), name="nsa_sample",
    )(page_table, *([cache] * pps), qbd, gates, kcvc, win, new_rows, bc, bw, bstep, bnew, ovt, gg, expand)


def _merge_kernel(x_ref, ylru_ref, ynsa_ref, mg_ref, wl_ref, wn_ref, wo_ref, gf_ref, wr_ref, br_ref,
                  x2_ref, hn_ref, wt_ref):
    gate = jax.nn.sigmoid(mg_ref[...])
    mixed = (gate[:, :D_MODEL] * _dot(ylru_ref[...].astype(BF16), wl_ref[...])
             + gate[:, D_MODEL:] * _dot(ynsa_ref[...].astype(BF16), wn_ref[...]))
    x2 = x_ref[...] + _dot(mixed.astype(BF16), wo_ref[...])
    x2_ref[...] = x2
    hn = x2 * lax.rsqrt(jnp.mean(x2 * x2, axis=-1, keepdims=True) + EPS) * gf_ref[...]
    hn_ref[...] = hn.astype(BF16)
    logits = jnp.dot(hn, wr_ref[...], precision=lax.Precision.HIGHEST, preferred_element_type=F32) + br_ref[...]
    lane = lax.broadcasted_iota(jnp.int32, logits.shape, 1)
    lanef = lane.astype(F32)
    big = float(LANES)
    gl = jnp.where(lane < N_GROUPS, logits, NEG)
    gmax = jnp.max(gl, axis=-1, keepdims=True)
    grp = jnp.min(jnp.where(gl == gmax, lanef, big), axis=-1, keepdims=True)
    p_grp = 1.0 / jnp.sum(jnp.where(lane < N_GROUPS, jnp.exp(gl - gmax), 0.0), axis=-1, keepdims=True)
    lo = N_GROUPS + grp * EXPERTS_PER_GROUP
    el = jnp.where((lanef >= lo) & (lanef < lo + EXPERTS_PER_GROUP), logits, NEG)
    v1 = jnp.max(el, axis=-1, keepdims=True)
    i1 = jnp.min(jnp.where(el == v1, lanef, big), axis=-1, keepdims=True)
    el2 = jnp.where(lanef == i1, NEG, el)
    v2 = jnp.max(el2, axis=-1, keepdims=True)
    i2 = jnp.min(jnp.where(el2 == v2, lanef, big), axis=-1, keepdims=True)
    e2 = jnp.exp(v2 - v1)
    den = 1.0 / (1.0 + e2)
    wt_ref[...] = jnp.where(lanef == i1, den * p_grp, jnp.where(lanef == i2, e2 * den * p_grp, 0.0))


def _merge(x2d, ylru, ynsa, mg, wl, wn, wo, gf, wr, br, tm=512):
    t = x2d.shape[0]
    tm = min(tm, t)
    tile = lambda a: pl.BlockSpec((tm, a.shape[1]), lambda i: (i, 0))
    const = lambda a: pl.BlockSpec(a.shape, lambda i: (0, 0), pipeline_mode=pl.Buffered(1))
    return pl.pallas_call(
        _merge_kernel, grid=(t // tm,),
        in_specs=[tile(x2d), tile(ylru), tile(ynsa), tile(mg)] + [const(a) for a in (wl, wn, wo, gf, wr, br)],
        out_specs=[pl.BlockSpec((tm, D_MODEL), lambda i: (i, 0)), pl.BlockSpec((tm, D_MODEL), lambda i: (i, 0)),
                   pl.BlockSpec((tm, LANES), lambda i: (i, 0))],
        out_shape=[jax.ShapeDtypeStruct((t, D_MODEL), F32), jax.ShapeDtypeStruct((t, D_MODEL), BF16),
                   jax.ShapeDtypeStruct((t, LANES), F32)],
        compiler_params=_cparams(("parallel",)), name="merge")(x2d, ylru, ynsa, mg, wl, wn, wo, gf, wr, br)


def _moe_kernel(hn_ref, wt_ref, x2_ref, wg_ref, wu_ref, wd_ref, gfin_ref, y_ref, acc):
    c = pl.program_id(1)

    @pl.when(c == 0)
    def _():
        acc[...] = jnp.zeros_like(acc)

    h = hn_ref[...]
    wt = wt_ref[...]
    lane = lax.broadcasted_iota(jnp.int32, wt.shape, 1)
    total = acc[...]
    for e in range(EXPERTS_PER_GROUP):
        act = jax.nn.silu(_dot(h, wg_ref[e])) * _dot(h, wu_ref[e])
        w_e = jnp.sum(jnp.where(lane == N_GROUPS + c * EXPERTS_PER_GROUP + e, wt, 0.0), axis=-1, keepdims=True)
        act = jnp.where(w_e != 0.0, act * w_e, 0.0)
        total = total + _dot(act.astype(BF16), wd_ref[e])
    acc[...] = total

    @pl.when(c == N_GROUPS - 1)
    def _():
        x = x2_ref[...] + total
        y_ref[...] = x * lax.rsqrt(jnp.mean(x * x, axis=-1, keepdims=True) + EPS) * gfin_ref[...]


def _moe(hn, wt, x2, wg, wu, wd, gfin, tm=512):
    t = hn.shape[0]
    tm = min(tm, t)
    return pl.pallas_call(
        _moe_kernel, grid=(t // tm, N_GROUPS),
        in_specs=[pl.BlockSpec((tm, D_MODEL), lambda i, c: (i, 0)), pl.BlockSpec((tm, LANES), lambda i, c: (i, 0)),
                  pl.BlockSpec((tm, D_MODEL), lambda i, c: (i, 0)),
                  pl.BlockSpec((EXPERTS_PER_GROUP, D_MODEL, D_EXPERT), lambda i, c: (c, 0, 0)),
                  pl.BlockSpec((EXPERTS_PER_GROUP, D_MODEL, D_EXPERT), lambda i, c: (c, 0, 0)),
                  pl.BlockSpec((EXPERTS_PER_GROUP, D_EXPERT, D_MODEL), lambda i, c: (c, 0, 0)),
                  pl.BlockSpec((1, D_MODEL), lambda i, c: (0, 0))],
        out_specs=pl.BlockSpec((tm, D_MODEL), lambda i, c: (i, 0)),
        out_shape=jax.ShapeDtypeStruct((t, D_MODEL), F32),
        scratch_shapes=[pltpu.VMEM((tm, D_MODEL), F32)],
        compiler_params=_cparams(("parallel", "arbitrary")), name="moe")(hn, wt, x2, wg, wu, wd, gfin)


def _bucket_table():
    n = np.arange(MAX_DISTANCE + 1)
    exact = N_BUCKETS // 2
    nf = np.maximum(n, exact).astype(np.float64)
    large = exact + (np.log(nf / exact) / math.log(MAX_DISTANCE / exact) * (N_BUCKETS - exact)).astype(np.int32)
    return np.where(n < exact, n, np.minimum(large, N_BUCKETS - 1))


def _bias_of(rel_b, rel):
    idx = jnp.asarray(np.clip(rel, 0, MAX_DISTANCE).reshape(-1, 1), jnp.int32)
    onehot = (idx == jnp.arange(MAX_DISTANCE + 1, dtype=jnp.int32)[None, :]).astype(F32)
    t = jnp.dot(onehot, rel_b * LOG2E, precision=lax.Precision.HIGHEST, preferred_element_type=F32)
    return t.reshape(rel.shape + (N_HEADS,))


def _prompt_tile(rel_b, rel, mask, minus=None):
    r = rel.shape[0]
    t = _bias_of(rel_b, rel).reshape(r, QT, KV_HEADS, GROUP).transpose(2, 0, 3, 1).reshape(KV_HEADS, r, GQ)
    if minus is not None:
        t = t - minus
    m = np.broadcast_to(np.asarray(mask)[None, :, None, :], (KV_HEADS, r, GROUP, QT)).reshape(KV_HEADS, r, GQ)
    return jnp.where(jnp.asarray(m), t, NEG)


def _sample_tile(rel_b, rel, mask):
    r, ds = rel.shape
    t = _bias_of(rel_b, rel)
    t = jnp.where(jnp.asarray(mask)[..., None], t, NEG)
    return t.transpose(2, 1, 0).reshape(N_HEADS * ds, r)


def _block_diag(w):
    eye = jnp.eye(LRU_HEADS, dtype=w.dtype)
    return jnp.einsum('hij,hk->hikj', w, eye).reshape(D_LRU, D_LRU)


def kernel(x_prompt, x_sample, cache_kv, state_kv_win, state_conv, state_h, page_table, g_mix, w_in, conv_w, conv_b,
           w_gate_a, b_gate_a, w_gate_x, b_gate_x, lru_lambda, cmp_w1, cmp_b1, cmp_w2, w_lru_out, w_nsa_out, w_out,
           g_ffn, w_router_group, b_router_group, w_router_expert, b_router_expert, w_exp_gate, w_exp_up,
           w_exp_down, rel_bias, g_final):
    assert w_in.shape[0] == 1, "single layer"
    b, s, _ = x_prompt.shape
    db, ds, _ = x_sample.shape
    npg = page_table.shape[1]
    past = npg * PAGE_SIZE
    wb = state_kv_win.shape[2]
    assert s % 256 == 0 and s // L_SEL >= N_SEL and CONV_W - 1 <= ds <= STRIDE and wb == WINDOW and past >= WINDOW

    w = w_in[0].astype(BF16)
    o = 0
    ws = []
    for width in (D_LRU, D_LRU, Q_DIM, 4 * KV_DIM, 2 * KV_DIM, 3 * N_HEADS, 2 * D_MODEL):
        ws.append(w[:, o:o + width])
        o += width
    ws[5] = jnp.pad(ws[5], ((0, 0), (0, LANES - 3 * N_HEADS)))
    g_mix2 = g_mix[0][None]
    wg = jnp.concatenate([_block_diag(w_gate_x[0]), _block_diag(w_gate_a[0])], axis=1).astype(BF16)
    lru_args = (conv_w[0], conv_b[0][None], wg, b_gate_x[0][None], b_gate_a[0][None], lru_lambda[0][None])
    w1 = cmp_w1[0].reshape(2, 2, STRIDE, HEAD_DIM, CMP_HIDDEN)
    w1f = jnp.einsum('armdn,hk->amhdkrn', w1, jnp.eye(HEADS_PER_TILE, dtype=F32)).reshape(
        2, STRIDE * LANES, HEADS_PER_TILE * 2 * CMP_HIDDEN).astype(BF16)
    b1 = cmp_b1[0][:, None, :]
    w2 = cmp_w2[0].astype(BF16)
    wl, wn, wo = w_lru_out[0].astype(BF16), w_nsa_out[0].astype(BF16), w_out[0].astype(BF16)
    n_r = N_GROUPS + N_EXPERTS
    wr = jnp.pad(jnp.concatenate([w_router_group[0], w_router_expert[0]], axis=1), ((0, 0), (0, LANES - n_r)))
    br = jnp.pad(jnp.concatenate([b_router_group[0], b_router_expert[0]]), (0, LANES - n_r))[None]
    weg, weu, wed = w_exp_gate[0].astype(BF16), w_exp_up[0].astype(BF16), w_exp_down[0].astype(BF16)
    gf, gfin = g_ffn[0][None], g_final[None]
    rel_b = rel_bias.astype(F32)[_bucket_table()]

    def mix_ffn(x2d, ylru, ynsa, mg):
        x2, hn, wt = _merge(x2d, ylru, ynsa, mg, wl, wn, wo, gf, wr, br)
        return _moe(hn, wt, x2, weg, weu, wed, gfin)

    xp2 = x_prompt.reshape(b * s, D_MODEL)
    lx, lg, q, kv, kvw, ng, mg, kv_t = _proj(xp2, g_mix2, ws, s, feature_major=(3,))
    y_lru, conv_p, h_p = _rglru_prompt(lx.reshape(b, s, D_LRU), lg.reshape(b, s, D_LRU), *lru_args)
    kv3 = kv.reshape(b, s, 4 * KV_DIM)
    kvw3 = kvw.reshape(b, s, 2 * KV_DIM)
    kcvc = _compress_prompt(kv3, w1f, b1, w2)
    nt = s // QT
    ncp = s // STRIDE
    nsb = s // L_SEL

    def key_tiles(x):
        return x.astype(BF16).reshape(b, nt, QT, KV_DIM)

    def val_tiles(x):
        return x.astype(BF16).reshape(b, nt, QT, KV_DIM).swapaxes(2, 3)

    qi = np.arange(QT)[None, :]
    lrow = np.arange(ncp)[:, None]
    rel_c = qi - STRIDE * lrow + STRIDE * ncp - (STRIDE * 8 + L_CMP - 1)
    n_far = int(np.sum(np.all(rel_c >= MAX_DISTANCE, axis=1)))
    crow = _prompt_tile(rel_b, np.full((1, QT), MAX_DISTANCE), np.ones((1, QT), bool))
    tc = jnp.concatenate([jnp.broadcast_to(crow, (KV_HEADS, n_far, GQ)),
                          _prompt_tile(rel_b, rel_c[n_far:], rel_c[n_far:] >= 0)], axis=1)
    kj = np.arange(QT)[:, None]
    diag = _prompt_tile(rel_b, qi - kj, qi - kj >= 0, crow)
    tiles = jnp.stack([diag, _prompt_tile(rel_b, WINDOW + qi - kj, qi - kj < 0, crow)])
    near = jnp.concatenate([_prompt_tile(rel_b, QT + qi - kj, np.ones((QT, QT), bool), crow), diag], axis=1)
    rr = np.arange(nsb)[:, None]
    ll = np.arange(ncp)[None, :]
    ov = jnp.asarray((ll >= 4 * rr - 1) & (ll <= 4 * rr + 3), BF16)
    qT = q.reshape(b, s, Q_DIM).swapaxes(1, 2)
    gT = ng[:, :3 * N_HEADS].reshape(b, s, 3 * N_HEADS).swapaxes(1, 2)
    bpt = QT // L_SEL
    code = (np.arange(nt)[:, None] % 2) * bpt + np.arange(QT)[None, :] // L_SEL
    mark = jnp.asarray(code[..., None] == np.arange(LANES - HEAD_DIM)[None, None, :], BF16)
    ks = kv3[..., 2 * KV_DIM:3 * KV_DIM].astype(BF16).reshape(b, nt, QT, KV_HEADS, HEAD_DIM)
    ksel_aug = jnp.concatenate([ks, jnp.broadcast_to(mark[None, :, :, None, :], ks.shape[:4] + (LANES - HEAD_DIM,))],
                               axis=-1).reshape(b, nt, QT, KV_HEADS * LANES)
    y_nsaT = _nsa_prompt(qT, gT, kcvc, ksel_aug, val_tiles(kv3[..., 3 * KV_DIM:]), key_tiles(kvw3[..., :KV_DIM]),
                         val_tiles(kvw3[..., KV_DIM:]), tc, tiles, near, ov)
    y_nsa = y_nsaT.swapaxes(1, 2).reshape(b * s, Q_DIM)
    y_prompt = mix_ffn(xp2, y_lru.reshape(b * s, D_LRU), y_nsa, mg).reshape(b, s, D_MODEL)
    kv_rows_prompt = kv_t.reshape(1, b, 4, KV_HEADS, HEAD_DIM, s).transpose(0, 1, 5, 2, 3, 4)
    win_prompt = kvw3[:, s - min(WINDOW, s):].reshape(1, b, min(WINDOW, s), 2, KV_HEADS, HEAD_DIM)

    xs2 = x_sample.reshape(db * ds, D_MODEL)
    lx, lg, q, kv, kvw, ng, mg = _proj(xs2, g_mix2, ws, db * ds)
    tmaj =lambda a: a.reshape(db, ds, D_LRU).swapaxes(0, 1)
    y_lru_t, conv_t, h_s = _rglru_sample(tmaj(lx), tmaj(lg), state_conv[0].swapaxes(0, 1), state_h[0], *lru_args)
    y_lru = y_lru_t.swapaxes(0, 1).reshape(db * ds, D_LRU)
    kv_s3 = kv.reshape(db, ds, 4 * KV_DIM)
    kvw_s3 = kvw.reshape(db, ds, 2 * KV_DIM)
    cache_t = cache_kv[0].transpose(0, 2, 3, 4, 1).reshape(cache_kv.shape[1], 4 * KV_DIM, PAGE_SIZE)
    win_t = state_kv_win[0].transpose(0, 2, 3, 4, 1).reshape(db, 2 * KV_DIM, wb)
    new_c = jnp.pad(kv_s3[..., :2 * KV_DIM], ((0, 0), (0, PAGE_SIZE - ds), (0, 0)))
    pps = 32
    assert npg % pps == 0 and LANES % (pps * PAGE_SIZE // L_SEL) == 0
    kcvc_s = _compress_sample(cache_t, page_table, new_c, w1f, b1, w2, pps)

    nsb_s = past // L_SEL + 1
    ntok = past // STRIDE
    jq = np.arange(ds)[None, :]
    nrow = np.arange(ntok)[:, None]
    rel = past + jq - STRIDE * nrow - (L_CMP - 1)
    bc = _sample_tile(rel_b, rel, rel >= 0)
    rn = np.arange(PAGE_SIZE)[:, None]
    new_rel, new_ok = jq - rn, (jq - rn >= 0) & (rn < ds)
    rw = np.arange(wb)[:, None]
    bw = _sample_tile(rel_b, np.concatenate([wb + jq - rw, new_rel], axis=0),
                      np.concatenate([(wb + jq - rw < WINDOW) & (past - wb + rw >= 0), new_ok], axis=0))
    bnew = _sample_tile(rel_b, new_rel, new_ok)
    step_keys = pps * PAGE_SIZE
    far_rel = np.full((step_keys, ds), MAX_DISTANCE)
    last_rel = far_rel.copy()
    last_rel[step_keys - PAGE_SIZE:] = PAGE_SIZE + jq - rn
    all_ok = np.ones((step_keys, ds), bool)
    bstep = jnp.stack([_sample_tile(rel_b, far_rel, all_ok), _sample_tile(rel_b, last_rel, all_ok)])
    nlane = -(-nsb_s // LANES) * LANES
    bb = np.arange(nlane)[None, :]
    tt = np.arange(ntok)[:, None]
    ovt = jnp.asarray((tt >= 4 * bb - 1) & (tt <= 4 * bb + 3) & (bb < nsb_s), BF16)
    ncol = N_HEADS * ds
    col = np.arange(ncol)
    kvh_c, j_c = col // (GROUP * ds), col % ds
    gg = jnp.asarray((kvh_c[:, None] == kvh_c[None, :]) & (j_c[:, None] == j_c[None, :]), BF16)
    bps = pps * PAGE_SIZE // L_SEL
    off = np.arange(LANES // bps)[:, None, None]
    expand = jnp.asarray(np.arange(LANES)[None, :, None] == bps * off + np.arange(step_keys)[None, None, :] // L_SEL,
                         BF16)
    q5 = q.reshape(db, ds, KV_HEADS, GROUP, HEAD_DIM) * QK_SCALE
    qbd = jnp.einsum('bjkgd,kc->bkgjcd', q5, jnp.eye(KV_HEADS, dtype=F32)).reshape(db, ncol, KV_DIM).astype(BF16)
    gates_s = ng[:, :3 * N_HEADS].reshape(db, ds, N_HEADS, 3).transpose(0, 2, 1, 3).reshape(db, ncol, 3)
    new_all = jnp.pad(jnp.concatenate([kv_s3[..., 2 * KV_DIM:], kvw_s3], axis=-1),
                      ((0, 0), (0, PAGE_SIZE - ds), (0, 0)))
    o_bd = _nsa_sample(cache_t, page_table, qbd, gates_s, kcvc_s, win_t, new_all, bc, bw, bstep, bnew, ovt, gg,
                       expand, pps)
    o6 = o_bd.reshape(db, KV_HEADS, GROUP, ds, KV_HEADS, HEAD_DIM)
    y_nsa = jnp.einsum('bkgjkd->bjkgd', o6).reshape(db * ds, Q_DIM)
    y_sample = mix_ffn(xs2, y_lru, y_nsa, mg).reshape(db, ds, D_MODEL)
    kv_rows_sample = kv_s3.reshape(1, db, ds, 4, KV_HEADS, HEAD_DIM)
    win_all = jnp.concatenate([state_kv_win[0].reshape(db, wb, 2 * KV_DIM), kvw_s3], axis=1)
    win_sample = win_all[:, win_all.shape[1] - WINDOW:].reshape(1, db, WINDOW, 2, KV_HEADS, HEAD_DIM)
    conv_s = conv_t.swapaxes(0, 1)

    return (y_prompt, y_sample, kv_rows_prompt, win_prompt, conv_p[None], h_p.reshape(1, b, D_LRU),
            kv_rows_sample, win_sample, conv_s[None], h_s[None])
```

```python
import functools
import math

import numpy as np
import jax
import jax.numpy as jnp
from jax import lax
from jax.experimental import pallas as pl
from jax.experimental.pallas import tpu as pltpu

F32 = jnp.float32
BF16 = jnp.bfloat16

D_MODEL = 1024
D_LRU = 1280
LRU_HEADS = 16
LRU_BLOCK = D_LRU // LRU_HEADS
CONV_W = 4
LRU_C = 8.0
N_HEADS = 16
HEAD_DIM = 64
KV_HEADS = 4
GROUP = N_HEADS // KV_HEADS
L_CMP = 32
STRIDE = 16
CMP_HIDDEN = 128
L_SEL = 64
N_SEL = 16
N_LOCAL = 2
N_FREE = N_SEL - (1 + N_LOCAL)
WINDOW = 512
PAGE_SIZE = 128
N_BUCKETS = 32
MAX_DISTANCE = 128
N_GROUPS = 4
EXPERTS_PER_GROUP = 8
N_EXPERTS = N_GROUPS * EXPERTS_PER_GROUP
D_EXPERT = 256
EPS = 1e-6
NEG = -1e30
FORCED = 1e9
Q_DIM = N_HEADS * HEAD_DIM
KV_DIM = KV_HEADS * HEAD_DIM
QT = 128
GQ = GROUP * QT
LANES = 128
SUB = 8
VMEM_LIMIT = 56 * 1024 * 1024
KNOCKED = -3e38
ABSENT = -2e38
LOG2E = math.log2(math.e)
QK_SCALE = HEAD_DIM ** -0.5 * LOG2E
MASK_ROWS = 16
ONES_ROWS = 16


def _cparams(sem):
    return pltpu.CompilerParams(dimension_semantics=sem, vmem_limit_bytes=VMEM_LIMIT)


def _dot(a, b):
    return jnp.dot(a, b, preferred_element_type=F32)


def _dot_tn(a, b):
    return lax.dot_general(a, b, (((0,), (0,)), ((), ())), preferred_element_type=F32)


def _dot_nt(a, b):
    return lax.dot_general(a, b, (((1,), (1,)), ((), ())), preferred_element_type=F32)


def _split3(x):
    hi = x.astype(BF16)
    r1 = x - hi.astype(F32)
    mid = r1.astype(BF16)
    lo = (r1 - mid.astype(F32)).astype(BF16)
    return hi, mid, lo


def _dot_exact_rhs(a_bf16, x):
    hi, mid, lo = _split3(x)
    return _dot(a_bf16, hi) + _dot(a_bf16, mid) + _dot(a_bf16, lo)


def _dot_exact_lhs(x, b_bf16):
    hi, mid, lo = _split3(x)
    return _dot(hi, b_bf16) + _dot(mid, b_bf16) + _dot(lo, b_bf16)


def _proj_kernel(x_ref, g_ref, *refs, n, feature_major):
    x = x_ref[...]
    xn = x * lax.rsqrt(jnp.mean(x * x, axis=-1, keepdims=True) + EPS) * g_ref[...]
    xb = xn.astype(BF16)
    outs = {}
    for j, (w_ref, o_ref) in enumerate(zip(refs[:n], refs[n:2 * n])):
        outs[j] = _dot(xb, w_ref[...])
        o_ref[...] = outs[j]
    for j, o_ref in zip(feature_major, refs[2 * n:]):
        o_ref[0] = outs[j].T


def _proj(x2d, g, ws, rows_per_seq, feature_major=(), tm=256):
    t = x2d.shape[0]
    tm = min(tm, t, rows_per_seq)
    tps = rows_per_seq // tm
    in_specs = [pl.BlockSpec((tm, D_MODEL), lambda i: (i, 0)), pl.BlockSpec((1, D_MODEL), lambda i: (0, 0))]
    in_specs += [pl.BlockSpec(w.shape, lambda i: (0, 0), pipeline_mode=pl.Buffered(1)) for w in ws]
    out_specs = [pl.BlockSpec((tm, w.shape[1]), lambda i: (i, 0)) for w in ws]
    out_shape = [jax.ShapeDtypeStruct((t, w.shape[1]), F32) for w in ws]
    for j in feature_major:
        out_specs.append(pl.BlockSpec((1, ws[j].shape[1], tm), lambda i: (i // tps, 0, i % tps)))
        out_shape.append(jax.ShapeDtypeStruct((t // rows_per_seq, ws[j].shape[1], rows_per_seq), F32))
    return pl.pallas_call(functools.partial(_proj_kernel, n=len(ws), feature_major=tuple(feature_major)),
                          grid=(t // tm,), in_specs=in_specs, out_specs=out_specs, out_shape=out_shape,
                          compiler_params=_cparams(("parallel",)), name="proj")(x2d, g, *ws)


def _softplus(x):
    return jnp.maximum(x, 0.0) + jnp.log1p(jnp.exp(-jnp.abs(x)))


def _lru_gates(xc, wg_ref, bgx_ref, bga_ref, lam_ref):
    gates = _dot(xc.astype(BF16), wg_ref[...])
    gx = jax.nn.sigmoid(gates[:, :D_LRU] + bgx_ref[...])
    ga = jax.nn.sigmoid(gates[:, D_LRU:] + bga_ref[...])
    log_a = -LRU_C * ga * _softplus(-lam_ref[...])
    a = jnp.exp(log_a)
    th = jnp.tanh(log_a)
    u = jnp.sqrt(-2.0 * th / (1.0 - th)) * (gx * xc)
    return a, u


def _rglru_prompt_kernel(lx_ref, lg_ref, cw_ref, cb_ref, wg_ref, bgx_ref, bga_ref, lam_ref,
                         y_ref, conv_ref, h_ref, xext, hc, *, tc):
    t = pl.program_id(1)

    @pl.when(t == 0)
    def _():
        xext[0:8, :] = jnp.zeros((8, D_LRU), F32)
        hc[...] = jnp.zeros_like(hc)

    x = lx_ref[0]
    xext[8:8 + tc, :] = x
    cw = cw_ref[...]
    xc = cb_ref[...] + cw[3:4] * x
    for j in range(CONV_W - 1):
        xc = xc + cw[j:j + 1] * xext[5 + j:5 + j + tc, :]
    a, u = _lru_gates(xc, wg_ref, bgx_ref, bga_ref, lam_ref)
    sub = lax.broadcasted_iota(jnp.int32, (tc, D_LRU), 0) % SUB
    s = 1
    while s < SUB:
        a_sh = pltpu.roll(a, s, 0)
        u_sh = pltpu.roll(u, s, 0)
        m = sub >= s
        u = jnp.where(m, a * u_sh + u, u)
        a = jnp.where(m, a * a_sh, a)
        s *= 2
    carry = hc[0:1, :]
    lg = lg_ref[0]
    for g in range(tc // SUB):
        rows = slice(g * SUB, (g + 1) * SUB)
        h = a[rows] * carry + u[rows]
        carry = h[SUB - 1:SUB, :]
        y_ref[0, rows, :] = h * jax.nn.gelu(lg[rows])
    hc[0:1, :] = carry
    xext[0:8, :] = x[tc - 8:tc, :]
    conv_ref[0] = x[tc - (CONV_W - 1):tc, :]
    h_ref[0] = carry


def _rglru_prompt(lx, lg, cw, cb, wg, bgx, bga, lam, tc=256):
    b, s, _ = lx.shape
    row = lambda shape: pl.BlockSpec(shape, lambda bi, ti: (0, 0))
    return pl.pallas_call(
        functools.partial(_rglru_prompt_kernel, tc=tc), grid=(b, s // tc),
        in_specs=[pl.BlockSpec((1, tc, D_LRU), lambda bi, ti: (bi, ti, 0)),
                  pl.BlockSpec((1, tc, D_LRU), lambda bi, ti: (bi, ti, 0)),
                  row((CONV_W, D_LRU)), row((1, D_LRU)), row((D_LRU, 2 * D_LRU)),
                  row((1, D_LRU)), row((1, D_LRU)), row((1, D_LRU))],
        out_specs=[pl.BlockSpec((1, tc, D_LRU), lambda bi, ti: (bi, ti, 0)),
                   pl.BlockSpec((1, CONV_W - 1, D_LRU), lambda bi, ti: (bi, 0, 0)),
                   pl.BlockSpec((1, 1, D_LRU), lambda bi, ti: (bi, 0, 0))],
        out_shape=[jax.ShapeDtypeStruct((b, s, D_LRU), F32),
                   jax.ShapeDtypeStruct((b, CONV_W - 1, D_LRU), F32),
                   jax.ShapeDtypeStruct((b, 1, D_LRU), F32)],
        scratch_shapes=[pltpu.VMEM((tc + 8, D_LRU), F32), pltpu.VMEM((8, D_LRU), F32)],
        compiler_params=_cparams(("parallel", "arbitrary")), name="rglru_prompt",
    )(lx, lg, cw, cb, wg, bgx, bga, lam)


def _rglru_sample_kernel(lx_ref, lg_ref, cbuf_ref, h0_ref, cw_ref, cb_ref, wg_ref, bgx_ref, bga_ref, lam_ref,
                         y_ref, conv_ref, h_ref, *, ds):
    cw = cw_ref[...]
    xp = [cbuf_ref[j] for j in range(CONV_W - 1)] + [lx_ref[j] for j in range(ds)]
    h = h0_ref[...]
    for t in range(ds):
        xc = cb_ref[...]
        for j in range(CONV_W):
            xc = xc + cw[j:j + 1] * xp[t + j]
        a, u = _lru_gates(xc, wg_ref, bgx_ref, bga_ref, lam_ref)
        h = a * h + u
        y_ref[t] = h * jax.nn.gelu(lg_ref[t])
    for j in range(CONV_W - 1):
        conv_ref[j] = xp[ds + j]
    h_ref[...] = h


def _rglru_sample(lx_t, lg_t, cbuf_t, h0, cw, cb, wg, bgx, bga, lam):
    ds, n, _ = lx_t.shape
    full = lambda a: pl.BlockSpec(a.shape, lambda i: (0,) * a.ndim)
    args = (lx_t, lg_t, cbuf_t, h0, cw, cb, wg, bgx, bga, lam)
    out_shape = [jax.ShapeDtypeStruct((ds, n, D_LRU), F32), jax.ShapeDtypeStruct((CONV_W - 1, n, D_LRU), F32),
                 jax.ShapeDtypeStruct((n, D_LRU), F32)]
    return pl.pallas_call(
        functools.partial(_rglru_sample_kernel, ds=ds), grid=(1,),
        in_specs=[full(a) for a in args], out_specs=[full(o) for o in out_shape], out_shape=out_shape,
        compiler_params=_cparams(("arbitrary",)), name="rglru_sample")(*args)


HEADS_PER_TILE = LANES // HEAD_DIM


def _chunk_project(load, w):
    feat = jnp.concatenate([load(s) for s in range(STRIDE)], axis=1).astype(BF16)
    return _dot(feat, w)


def _compress_prompt_kernel(xa_ref, xb_ref, w1_ref, b1_ref, w2_ref, o_ref, *, ncp):
    rows = lax.broadcasted_iota(jnp.int32, (ncp, HEAD_DIM), 0)
    for c, x_ref in enumerate((xa_ref, xb_ref)):
        p2 = _chunk_project(lambda s: x_ref[0, pl.ds(s, ncp, stride=STRIDE), :], w1_ref[0])
        for hh in range(HEADS_PER_TILE):
            k = c * HEADS_PER_TILE + hh
            p = p2[:, hh * 2 * CMP_HIDDEN:(hh + 1) * 2 * CMP_HIDDEN]
            pre = b1_ref[0] + p[:, :CMP_HIDDEN] + pltpu.roll(p[:, CMP_HIDDEN:], ncp - 1, 0)
            phi = _dot(jax.nn.gelu(pre).astype(BF16), w2_ref[0])
            o_ref[0, 0, k, 0:ncp, :] = jnp.zeros((ncp, HEAD_DIM), F32)
            o_ref[0, 0, k, ncp:2 * ncp, :] = jnp.where(rows < ncp - 1, phi, 0.0)


def _compress_prompt(kv, w1f, b1, w2):
    b, s, _ = kv.shape
    ncp = s // STRIDE
    return pl.pallas_call(
        functools.partial(_compress_prompt_kernel, ncp=ncp), grid=(b, 2),
        in_specs=[pl.BlockSpec((1, s, LANES), lambda bi, sl: (bi, 0, 2 * sl)),
                  pl.BlockSpec((1, s, LANES), lambda bi, sl: (bi, 0, 2 * sl + 1)),
                  pl.BlockSpec((1,) + w1f.shape[1:], lambda bi, sl: (sl, 0, 0)),
                  pl.BlockSpec((1, 1, CMP_HIDDEN), lambda bi, sl: (sl, 0, 0)),
                  pl.BlockSpec((1, CMP_HIDDEN, HEAD_DIM), lambda bi, sl: (sl, 0, 0))],
        out_specs=pl.BlockSpec((1, 1, KV_HEADS, 2 * ncp, HEAD_DIM), lambda bi, sl: (sl, bi, 0, 0, 0)),
        out_shape=jax.ShapeDtypeStruct((2, b, KV_HEADS, 2 * ncp, HEAD_DIM), F32),
        compiler_params=_cparams(("parallel", "parallel")), name="compress_prompt")(kv, kv, w1f, b1, w2)


def _compress_sample_kernel(pt_ref, *refs, pps, nsteps):
    pages = refs[:pps]
    new_ref, w1_ref, b1_ref, w2_ref, o_ref, xs, pscr = refs[pps:]
    st = pl.program_id(1)
    cps = PAGE_SIZE // STRIDE
    m = pps * cps
    ntok = nsteps * m
    tiles_per_slot = KV_DIM // LANES

    def stage(ref, j, transposed):
        for c in range(2 * tiles_per_slot):
            if transposed:
                blk = ref[0, c * LANES:(c + 1) * LANES, :].T
            else:
                blk = ref[0, :, c * LANES:(c + 1) * LANES]
            xs[c, j * PAGE_SIZE:(j + 1) * PAGE_SIZE, :] = blk

    def project(nrows, row0):
        for sl in range(2):
            for c in range(tiles_per_slot):
                p2 = _chunk_project(lambda s: xs[sl * tiles_per_slot + c, pl.ds(s, nrows, stride=STRIDE), :],
                                    w1_ref[sl])
                for hh in range(HEADS_PER_TILE):
                    pscr[sl, c * HEADS_PER_TILE + hh, pl.ds(row0, nrows), :] = (
                        p2[:, hh * 2 * CMP_HIDDEN:(hh + 1) * 2 * CMP_HIDDEN])

    for j, pg in enumerate(pages):
        stage(pg, j, True)
    project(m, pl.multiple_of(st * m, m))

    @pl.when(st == nsteps - 1)
    def _():
        stage(new_ref, 0, False)
        project(cps, ntok)
        for sl in range(2):
            toks = []
            for k in range(KV_HEADS):
                pre = (b1_ref[sl] + pscr[sl, k, 0:ntok, 0:CMP_HIDDEN]
                       + pscr[sl, k, 1:ntok + 1, CMP_HIDDEN:2 * CMP_HIDDEN])
                toks.append(_dot(jax.nn.gelu(pre).astype(BF16), w2_ref[sl]))
            o_ref[0, sl] = jnp.concatenate(toks, axis=1)


def _compress_sample(cache, page_table, new_rows, w1f, b1, w2, pps=16):
    db, npg = page_table.shape
    pps = min(pps, npg)
    nsteps = npg // pps
    cps = PAGE_SIZE // STRIDE
    ntok = npg * cps

    def page_spec(j):
        return pl.BlockSpec((1, 2 * KV_DIM, PAGE_SIZE), lambda bi, st, pt: (pt[bi, st * pps + j], 0, 0))

    const = lambda shape: pl.BlockSpec(shape, lambda bi, st, pt: (0,) * len(shape))
    grid_spec = pltpu.PrefetchScalarGridSpec(
        num_scalar_prefetch=1, grid=(db, nsteps),
        in_specs=[page_spec(j) for j in range(pps)] + [
            pl.BlockSpec((1, PAGE_SIZE, 2 * KV_DIM), lambda bi, st, pt: (bi, 0, 0)),
            const(w1f.shape), const((2, 1, CMP_HIDDEN)),
            const((2, CMP_HIDDEN, HEAD_DIM))],
        out_specs=pl.BlockSpec((1, 2, ntok, KV_DIM), lambda bi, st, pt: (bi, 0, 0, 0)),
        scratch_shapes=[pltpu.VMEM((2 * KV_DIM // LANES, pps * PAGE_SIZE, LANES), F32),
                        pltpu.VMEM((2, KV_HEADS, ntok + cps, 2 * CMP_HIDDEN), F32)])
    return pl.pallas_call(
        functools.partial(_compress_sample_kernel, pps=pps, nsteps=nsteps), grid_spec=grid_spec,
        out_shape=jax.ShapeDtypeStruct((db, 2, ntok, KV_DIM), F32),
        compiler_params=_cparams(("parallel", "arbitrary")), name="compress_sample",
    )(page_table, *([cache] * pps), new_rows, w1f, b1, w2)


def _topk_masks(scores, axis):
    n = scores[0].shape[axis]
    idx = lax.broadcasted_iota(jnp.int32, scores[0].shape, axis).astype(F32)

    def body(_, scs):
        out = []
        for sc in scs:
            mx = jnp.max(sc, axis=axis, keepdims=True)
            first = jnp.min(jnp.where(sc == mx, idx, float(n)), axis=axis, keepdims=True)
            out.append(jnp.where(idx == first, KNOCKED, sc))
        return tuple(out)

    return [jnp.where(sc == KNOCKED, 1.0, 0.0) for sc in lax.fori_loop(0, N_FREE, body, tuple(scores))]


def _softmax2(s, axis):
    m = jnp.max(s, axis=axis, keepdims=True)
    e = jnp.where(s > 0.5 * NEG, jnp.exp2(s - m), 0.0)
    l = jnp.sum(e, axis=axis, keepdims=True)
    return e * (1.0 / jnp.maximum(l, 1e-30))


def _nsa_prompt_kernel(qT_ref, gT_ref, kc_ref, vc_ref, ksel_ref, vselT_ref, kwin_ref, vwinT_ref,
                       tc_ref, tiles_ref, near_ref, ov_ref, y_ref,
                       qk_s, madd_s, oc_s, m_s, acc_s, s_scr, p_scr, alpha_s, *, ncp, nsb):
    i = pl.program_id(1)
    qT = qT_ref[0]
    for k in range(KV_HEADS):
        qk_s[k, 0:HEAD_DIM, :] = (jnp.concatenate([qT[(GROUP * k + g) * HEAD_DIM:(GROUP * k + g + 1) * HEAD_DIM, :]
                                                   for g in range(GROUP)], axis=1) * QK_SCALE).astype(BF16)
        qk_s[k, HEAD_DIM:LANES, :] = jnp.zeros((LANES - HEAD_DIM, GQ), BF16)
    m_s[...] = jnp.full(m_s.shape, NEG, F32)
    acc_s[...] = jnp.zeros_like(acc_s)

    start = pl.multiple_of(8 * i + 8, 8)
    tok_ok = lax.broadcasted_iota(jnp.int32, (ncp, 1), 0) >= ncp - 8 - 8 * i
    rblk = lax.broadcasted_iota(jnp.int32, (nsb, QT), 0)
    qhalf = jnp.where(lax.broadcasted_iota(jnp.int32, (nsb, QT), 1) >= L_SEL, 1, 0)
    r_qb = nsb - 2 + qhalf
    exists = rblk >= nsb - 2 - 2 * i
    forced = (rblk == nsb - 2 - 2 * i) | (rblk > r_qb - N_LOCAL)
    visible = exists & (rblk <= r_qb)
    forced_ok = visible & forced
    free = visible & jnp.logical_not(forced)
    def block_scores(k):
        kc = kc_ref[0, 0, k, pl.ds(start, ncp), :].astype(BF16)
        vc = vc_ref[0, 0, k, pl.ds(start, ncp), :].astype(BF16)
        s = jnp.where(tok_ok, _dot(kc, qk_s[k, 0:HEAD_DIM, :]) + tc_ref[k], NEG)
        pn = _softmax2(s, 0)
        oc_s[k] = _dot_tn(vc, pn.astype(BF16))
        psum = pn[:, 0:QT]
        for g in range(1, GROUP):
            psum = psum + pn[:, g * QT:(g + 1) * QT]
        imp = _dot_exact_rhs(ov_ref[...], psum)
        return jnp.where(free, imp, ABSENT)

    for k0 in range(0, KV_HEADS, 2):
        for j, sel in enumerate(_topk_masks([block_scores(k0), block_scores(k0 + 1)], 0)):
            madd_s[k0 + j] = jnp.where(forced_ok | (sel > 0.0), 0.0, NEG)

    bpt = QT // L_SEL

    def logits(t, n, k_ref, slot, masked):
        kt = k_ref[0, t] if n == 1 else jnp.concatenate([k_ref[0, t + j] for j in range(n)], axis=0)
        for k in range(KV_HEADS):
            if masked:
                r0 = bpt * (t - i) + nsb - bpt
                par = t % 2
                first = [r0 + (bpt * par if n == 2 else 0) + h for h in range(bpt)]
                second = [r0 + (bpt * (1 - par) if n == 2 else 0) + h for h in range(bpt)]
                rows = [jnp.concatenate([madd_s[k, pl.ds(r, 1), :]] * GROUP, axis=1) for r in first + second]
                blk = jnp.concatenate(rows + [jnp.zeros((MASK_ROWS - 2 * bpt, GQ), F32)], axis=0)
                qk_s[k, HEAD_DIM:HEAD_DIM + MASK_ROWS, :] = blk.astype(BF16)
                s_scr[slot, k, 0:n * QT, :] = _dot(kt[:, k * LANES:(k + 1) * LANES], qk_s[k])
            else:
                s_scr[slot, k, 0:n * QT, :] = _dot(kt[:, k * HEAD_DIM:(k + 1) * HEAD_DIM], qk_s[k, 0:HEAD_DIM, :])

    def softmax(n, add_of, branch, slot):
        nk = n * QT
        for k in range(KV_HEADS):
            idx = branch * KV_HEADS + k
            s = s_scr[slot, k, 0:nk, :]
            if add_of is not None:
                s = s + add_of(k)
            m_old = m_s[idx]
            m_new = jnp.maximum(m_old, jnp.max(s, axis=0, keepdims=True))
            alpha_s[slot, k] = jnp.exp2(m_old - m_new)
            m_s[idx] = m_new
            p_scr[slot, k, 0:nk, :] = jnp.exp2((s - m_new).astype(BF16))

    def values(t, n, vT_ref, branch, slot):
        nk = n * QT
        vt = vT_ref[0, t] if n == 1 else jnp.concatenate([vT_ref[0, t + j] for j in range(n)], axis=1)
        ones = jnp.ones((ONES_ROWS, nk), BF16)
        for k in range(KV_HEADS):
            idx = branch * KV_HEADS + k
            vt_aug = jnp.concatenate([vt[k * HEAD_DIM:(k + 1) * HEAD_DIM, :], ones], axis=0)
            acc_s[idx] = alpha_s[slot, k] * acc_s[idx] + _dot(vt_aug, p_scr[slot, k, 0:nk, :])

    def attend(t, n, k_ref, vT_ref, add_of, branch):
        logits(t, n, k_ref, 0, branch == 0)
        softmax(n, add_of, branch, 0)
        values(t, n, vT_ref, branch, 0)

    nfar = jnp.maximum(i - 1, 0)
    npair = nfar // 2

    last_pair = 2 * (npair - 1)

    @pl.when(npair >= 2)
    def _():
        logits(0, 2, ksel_ref, 0, True)
        p_scr[1] = jnp.zeros(p_scr.shape[1:], BF16)
        alpha_s[1] = jnp.ones(alpha_s.shape[1:], F32)

    def far_body(u, c):
        ta = 4 * u
        logits(ta + 2, 2, ksel_ref, 1, True)
        values(jnp.maximum(ta - 2, 0), 2, vselT_ref, 0, 1)
        softmax(2, None, 0, 0)
        logits(jnp.minimum(ta + 4, last_pair), 2, ksel_ref, 0, True)
        values(ta, 2, vselT_ref, 0, 0)
        softmax(2, None, 0, 1)
        return c

    ntrip = npair // 2
    lax.fori_loop(0, ntrip, far_body, 0)

    @pl.when(ntrip > 0)
    def _():
        values(4 * ntrip - 2, 2, vselT_ref, 0, 1)

    @pl.when(npair % 2 == 1)
    def _():
        attend(last_pair, 2, ksel_ref, vselT_ref, None, 0)

    @pl.when(nfar % 2 == 1)
    def _():
        attend(nfar - 1, 1, ksel_ref, vselT_ref, None, 0)

    near = lambda k: near_ref[k]
    win_edge = (i - 4, 1, kwin_ref, vwinT_ref, lambda k: tiles_ref[1, k], 1)
    win_far = (i - 3, 2, kwin_ref, vwinT_ref, None, 1)
    sel_near = (i - 1, 2, ksel_ref, vselT_ref, near, 0)
    win_near = (i - 1, 2, kwin_ref, vwinT_ref, near, 1)

    @pl.when(i >= 4)
    def _():
        items = (win_edge, win_far, sel_near, win_near)
        logits(items[0][0], items[0][1], items[0][2], 0, items[0][5] == 0)
        for j, (t, n, k_ref, vT_ref, add_of, branch) in enumerate(items):
            slot = j % 2
            if j + 1 < len(items):
                tn, nn, kn = items[j + 1][:3]
                logits(tn, nn, kn, 1 - slot, items[j + 1][5] == 0)
            if j >= 1:
                tp, np_, _, vp, _, bp = items[j - 1]
                values(tp, np_, vp, bp, 1 - slot)
            softmax(n, add_of, branch, slot)
        tp, np_, _, vp, _, bp = items[-1]
        values(tp, np_, vp, bp, (len(items) - 1) % 2)

    @pl.when(i == 3)
    def _():
        attend(*win_far)

    @pl.when(i == 2)
    def _():
        attend(0, 1, kwin_ref, vwinT_ref, None, 1)

    @pl.when((i >= 1) & (i < 4))
    def _():
        attend(*sel_near)
        attend(*win_near)

    @pl.when(i == 0)
    def _():
        attend(0, 1, ksel_ref, vselT_ref, lambda k: tiles_ref[0, k], 0)
        attend(0, 1, kwin_ref, vwinT_ref, lambda k: tiles_ref[0, k], 1)

    gate = jax.nn.sigmoid(gT_ref[0])
    for k in range(KV_HEADS):
        o_s = acc_s[k, 0:HEAD_DIM] * (1.0 / acc_s[k, HEAD_DIM:HEAD_DIM + 1])
        o_w = acc_s[KV_HEADS + k, 0:HEAD_DIM] * (1.0 / acc_s[KV_HEADS + k, HEAD_DIM:HEAD_DIM + 1])
        o_c = oc_s[k]
        for g in range(GROUP):
            h = GROUP * k + g
            cols = slice(g * QT, (g + 1) * QT)
            y_ref[0, h * HEAD_DIM:(h + 1) * HEAD_DIM, :] = (
                gate[3 * h:3 * h + 1] * o_c[:, cols] + gate[3 * h + 1:3 * h + 2] * o_s[:, cols]
                + gate[3 * h + 2:3 * h + 3] * o_w[:, cols])


def _nsa_prompt(qT, gT, kcvc, ksel, vselT, kwin, vwinT, tc, tiles, near, ov):
    b, _, s = qT.shape
    nq = s // QT
    ncp = s // STRIDE
    nsb = s // L_SEL
    seq4 = lambda a: pl.BlockSpec((1,) + a.shape[1:], lambda bi, qi: (bi, 0, 0, 0), pipeline_mode=pl.Buffered(1))
    const = lambda a: pl.BlockSpec(a.shape, lambda bi, qi: (0,) * a.ndim, pipeline_mode=pl.Buffered(1))
    cmp_spec = lambda sl: pl.BlockSpec((1, 1, KV_HEADS, 2 * ncp, HEAD_DIM), lambda bi, qi: (sl, bi, 0, 0, 0),
                                       pipeline_mode=pl.Buffered(1))
    return pl.pallas_call(
        functools.partial(_nsa_prompt_kernel, ncp=ncp, nsb=nsb), grid=(b, nq),
        in_specs=[pl.BlockSpec((1, Q_DIM, QT), lambda bi, qi: (bi, 0, qi)),
                  pl.BlockSpec((1, 3 * N_HEADS, QT), lambda bi, qi: (bi, 0, qi)),
                  cmp_spec(0), cmp_spec(1), seq4(ksel), seq4(vselT), seq4(kwin), seq4(vwinT),
                  const(tc), const(tiles), const(near), const(ov)],
        out_specs=pl.BlockSpec((1, Q_DIM, QT), lambda bi, qi: (bi, 0, qi)),
        out_shape=jax.ShapeDtypeStruct((b, Q_DIM, s), F32),
        scratch_shapes=[pltpu.VMEM((KV_HEADS, LANES, GQ), BF16), pltpu.VMEM((KV_HEADS, nsb, QT), F32),
                        pltpu.VMEM((KV_HEADS, HEAD_DIM, GQ), F32), pltpu.VMEM((2 * KV_HEADS, 1, GQ), F32),
                        pltpu.VMEM((2 * KV_HEADS, HEAD_DIM + ONES_ROWS, GQ), F32),
                        pltpu.VMEM((2, KV_HEADS, 2 * QT, GQ), F32), pltpu.VMEM((2, KV_HEADS, 2 * QT, GQ), BF16),
                        pltpu.VMEM((2, KV_HEADS, 1, GQ), F32)],
        compiler_params=_cparams(("parallel", "arbitrary")), name="nsa_prompt",
    )(qT, gT, kcvc, kcvc, ksel, vselT, kwin, vwinT, tc, tiles, near, ov)


def _nsa_sample_kernel(pt_ref, *refs, pps, nsteps, nsb):
    pages = refs[:pps]
    (qbd_ref, gate_ref, kcvc_ref, win_ref, new_ref, bc_ref, bw_ref, bstep_ref, bnew_ref, ovt_ref, gg_ref, exp_ref,
     o_ref, madd_s, oc_s, ow_s, m_s, l_s, acc_s) = refs[pps:]
    st = pl.program_id(1)
    qbd = qbd_ref[0]
    ncol = qbd.shape[0]
    wb = win_ref.shape[2]
    bps = pps * (PAGE_SIZE // L_SEL)
    nlane = ovt_ref.shape[1]

    def online(s, pv_of):
        m_old = m_s[...]
        m_new = jnp.maximum(m_old, jnp.max(s, axis=1, keepdims=True))
        alpha = jnp.exp2(m_old - m_new)
        p = jnp.exp2(s - m_new)
        l_s[...] = alpha * l_s[...] + jnp.sum(p, axis=1, keepdims=True)
        acc_s[...] = alpha * acc_s[...] + pv_of(p.astype(BF16))
        m_s[...] = m_new

    @pl.when(st == 0)
    def _():
        m_s[...] = jnp.full(m_s.shape, NEG, F32)
        l_s[...] = jnp.zeros_like(l_s)
        acc_s[...] = jnp.zeros_like(acc_s)
        pn = _softmax2(_dot_nt(qbd, kcvc_ref[0, 0].astype(BF16)) + bc_ref[...], 1)
        oc_s[...] = _dot(pn.astype(BF16), kcvc_ref[0, 1].astype(BF16))
        imp = sum(_dot_tn(part, gg_ref[...]) for part in _split3(_dot_exact_lhs(pn, ovt_ref[...])))
        nrow = -(-nsb // 8) * 8
        blk = lax.broadcasted_iota(jnp.int32, (nrow, ncol), 0)
        qb = nsb - 1
        forced = (blk == 0) | (blk > qb - N_LOCAL)
        visible = blk <= qb
        score = jnp.where(visible & jnp.logical_not(forced), imp[0:nrow], ABSENT)
        picked = (visible & forced) | (_topk_masks([score], 0)[0] > 0.0)
        madd = jnp.concatenate([jnp.where(picked, 0.0, NEG),
                                jnp.full((nlane - nrow, ncol), NEG, F32)], axis=0).astype(BF16)
        for tl in range(nlane // LANES):
            madd_s[tl] = madd[tl * LANES:(tl + 1) * LANES, :]
        new = new_ref[0]
        s_w = jnp.concatenate([_dot(qbd, win_ref[0, 0:KV_DIM, :].astype(BF16)),
                               _dot_nt(qbd, new[:, 2 * KV_DIM:3 * KV_DIM].astype(BF16))], axis=1) + bw_ref[...]
        pw = _softmax2(s_w, 1).astype(BF16)
        ow_s[...] = (_dot_nt(pw[:, 0:wb], win_ref[0, KV_DIM:2 * KV_DIM, :].astype(BF16))
                     + _dot(pw[:, wb:], new[:, 3 * KV_DIM:4 * KV_DIM].astype(BF16)))

    kt = jnp.concatenate([pg[0, 0:KV_DIM, :] for pg in pages], axis=1).astype(BF16)
    vt = jnp.concatenate([pg[0, KV_DIM:2 * KV_DIM, :] for pg in pages], axis=1).astype(BF16)
    b0 = st * bps
    mask = _dot_tn(madd_s[b0 // LANES], exp_ref[(b0 % LANES) // bps])
    bias = jnp.where(st == nsteps - 1, bstep_ref[1], bstep_ref[0])
    online(_dot(qbd, kt) + mask + bias, lambda p: _dot_nt(p, vt))

    @pl.when(st == nsteps - 1)
    def _():
        new = new_ref[0]
        lb = nsb - 1
        spread = (lax.broadcasted_iota(jnp.int32, (LANES, PAGE_SIZE), 0) == lb % LANES).astype(BF16)
        online(_dot_nt(qbd, new[:, 0:KV_DIM].astype(BF16)) + bnew_ref[...] + _dot_tn(madd_s[lb // LANES], spread),
               lambda p: _dot(p, new[:, KV_DIM:2 * KV_DIM].astype(BF16)))
        gate = jax.nn.sigmoid(gate_ref[0])
        o_ref[0] = (gate[:, 0:1] * oc_s[...] + gate[:, 1:2] * (acc_s[...] * (1.0 / l_s[...]))
                    + gate[:, 2:3] * ow_s[...])


def _nsa_sample(cache, page_table, qbd, gates, kcvc, win, new_rows, bc, bw, bstep, bnew, ovt, gg, expand, pps):
    db, npg = page_table.shape
    nsteps = npg // pps
    nsb = npg * (PAGE_SIZE // L_SEL) + 1
    ncol = qbd.shape[1]
    nlane = ovt.shape[1]

    def page_spec(j):
        return pl.BlockSpec((1, 2 * KV_DIM, PAGE_SIZE), lambda bi, st, pt: (pt[bi, st * pps + j], 1, 0))

    const = lambda a: pl.BlockSpec(a.shape, lambda bi, st, pt: (0,) * a.ndim)
    seq = lambda a: pl.BlockSpec((1,) + a.shape[1:], lambda bi, st, pt: (bi,) + (0,) * (a.ndim - 1))
    grid_spec = pltpu.PrefetchScalarGridSpec(
        num_scalar_prefetch=1, grid=(db, nsteps),
        in_specs=[page_spec(j) for j in range(pps)] + [seq(qbd), seq(gates), seq(kcvc), seq(win), seq(new_rows)]
        + [const(a) for a in (bc, bw, bstep, bnew, ovt, gg, expand)],
        out_specs=pl.BlockSpec((1, ncol, KV_DIM), lambda bi, st, pt: (bi, 0, 0)),
        scratch_shapes=[pltpu.VMEM((nlane // LANES, LANES, ncol), BF16), pltpu.VMEM((ncol, KV_DIM), F32),
                        pltpu.VMEM((ncol, KV_DIM), F32), pltpu.VMEM((ncol, 1), F32), pltpu.VMEM((ncol, 1), F32),
                        pltpu.VMEM((ncol, KV_DIM), F32)])
    return pl.pallas_call(
        functools.partial(_nsa_sample_kernel, pps=pps, nsteps=nsteps, nsb=nsb), grid_spec=grid_spec,
        out_shape=jax.ShapeDtypeStruct((db, ncol, KV_DIM), F32),
        compiler_params=_cparams(("parallel", "arbitrary")), name="nsa_sample",
    )(page_table, *([cache] * pps), qbd, gates, kcvc, win, new_rows, bc, bw, bstep, bnew, ovt, gg, expand)


def _merge_kernel(x_ref, ylru_ref, ynsa_ref, mg_ref, wl_ref, wn_ref, wo_ref, gf_ref, wr_ref, br_ref,
                  x2_ref, hn_ref, wt_ref):
    gate = jax.nn.sigmoid(mg_ref[...])
    mixed = (gate[:, :D_MODEL] * _dot(ylru_ref[...].astype(BF16), wl_ref[...])
             + gate[:, D_MODEL:] * _dot(ynsa_ref[...].astype(BF16), wn_ref[...]))
    x2 = x_ref[...] + _dot(mixed.astype(BF16), wo_ref[...])
    x2_ref[...] = x2
    hn = x2 * lax.rsqrt(jnp.mean(x2 * x2, axis=-1, keepdims=True) + EPS) * gf_ref[...]
    hn_ref[...] = hn.astype(BF16)
    logits = jnp.dot(hn, wr_ref[...], precision=lax.Precision.HIGHEST, preferred_element_type=F32) + br_ref[...]
    lane = lax.broadcasted_iota(jnp.int32, logits.shape, 1)
    lanef = lane.astype(F32)
    big = float(LANES)
    gl = jnp.where(lane < N_GROUPS, logits, NEG)
    gmax = jnp.max(gl, axis=-1, keepdims=True)
    grp = jnp.min(jnp.where(gl == gmax, lanef, big), axis=-1, keepdims=True)
    p_grp = 1.0 / jnp.sum(jnp.where(lane < N_GROUPS, jnp.exp(gl - gmax), 0.0), axis=-1, keepdims=True)
    lo = N_GROUPS + grp * EXPERTS_PER_GROUP
    el = jnp.where((lanef >= lo) & (lanef < lo + EXPERTS_PER_GROUP), logits, NEG)
    v1 = jnp.max(el, axis=-1, keepdims=True)
    i1 = jnp.min(jnp.where(el == v1, lanef, big), axis=-1, keepdims=True)
    el2 = jnp.where(lanef == i1, NEG, el)
    v2 = jnp.max(el2, axis=-1, keepdims=True)
    i2 = jnp.min(jnp.where(el2 == v2, lanef, big), axis=-1, keepdims=True)
    e2 = jnp.exp(v2 - v1)
    den = 1.0 / (1.0 + e2)
    wt_ref[...] = jnp.where(lanef == i1, den * p_grp, jnp.where(lanef == i2, e2 * den * p_grp, 0.0))


def _merge(x2d, ylru, ynsa, mg, wl, wn, wo, gf, wr, br, tm=512):
    t = x2d.shape[0]
    tm = min(tm, t)
    tile = lambda a: pl.BlockSpec((tm, a.shape[1]), lambda i: (i, 0))
    const = lambda a: pl.BlockSpec(a.shape, lambda i: (0, 0), pipeline_mode=pl.Buffered(1))
    return pl.pallas_call(
        _merge_kernel, grid=(t // tm,),
        in_specs=[tile(x2d), tile(ylru), tile(ynsa), tile(mg)] + [const(a) for a in (wl, wn, wo, gf, wr, br)],
        out_specs=[pl.BlockSpec((tm, D_MODEL), lambda i: (i, 0)), pl.BlockSpec((tm, D_MODEL), lambda i: (i, 0)),
                   pl.BlockSpec((tm, LANES), lambda i: (i, 0))],
        out_shape=[jax.ShapeDtypeStruct((t, D_MODEL), F32), jax.ShapeDtypeStruct((t, D_MODEL), BF16),
                   jax.ShapeDtypeStruct((t, LANES), F32)],
        compiler_params=_cparams(("parallel",)), name="merge")(x2d, ylru, ynsa, mg, wl, wn, wo, gf, wr, br)


def _moe_kernel(hn_ref, wt_ref, x2_ref, wg_ref, wu_ref, wd_ref, gfin_ref, y_ref, acc):
    c = pl.program_id(1)

    @pl.when(c == 0)
    def _():
        acc[...] = jnp.zeros_like(acc)

    h = hn_ref[...]
    wt = wt_ref[...]
    lane = lax.broadcasted_iota(jnp.int32, wt.shape, 1)
    total = acc[...]
    for e in range(EXPERTS_PER_GROUP):
        act = jax.nn.silu(_dot(h, wg_ref[e])) * _dot(h, wu_ref[e])
        w_e = jnp.sum(jnp.where(lane == N_GROUPS + c * EXPERTS_PER_GROUP + e, wt, 0.0), axis=-1, keepdims=True)
        act = jnp.where(w_e != 0.0, act * w_e, 0.0)
        total = total + _dot(act.astype(BF16), wd_ref[e])
    acc[...] = total

    @pl.when(c == N_GROUPS - 1)
    def _():
        x = x2_ref[...] + total
        y_ref[...] = x * lax.rsqrt(jnp.mean(x * x, axis=-1, keepdims=True) + EPS) * gfin_ref[...]


def _moe(hn, wt, x2, wg, wu, wd, gfin, tm=512):
    t = hn.shape[0]
    tm = min(tm, t)
    return pl.pallas_call(
        _moe_kernel, grid=(t // tm, N_GROUPS),
        in_specs=[pl.BlockSpec((tm, D_MODEL), lambda i, c: (i, 0)), pl.BlockSpec((tm, LANES), lambda i, c: (i, 0)),
                  pl.BlockSpec((tm, D_MODEL), lambda i, c: (i, 0)),
                  pl.BlockSpec((EXPERTS_PER_GROUP, D_MODEL, D_EXPERT), lambda i, c: (c, 0, 0)),
                  pl.BlockSpec((EXPERTS_PER_GROUP, D_MODEL, D_EXPERT), lambda i, c: (c, 0, 0)),
                  pl.BlockSpec((EXPERTS_PER_GROUP, D_EXPERT, D_MODEL), lambda i, c: (c, 0, 0)),
                  pl.BlockSpec((1, D_MODEL), lambda i, c: (0, 0))],
        out_specs=pl.BlockSpec((tm, D_MODEL), lambda i, c: (i, 0)),
        out_shape=jax.ShapeDtypeStruct((t, D_MODEL), F32),
        scratch_shapes=[pltpu.VMEM((tm, D_MODEL), F32)],
        compiler_params=_cparams(("parallel", "arbitrary")), name="moe")(hn, wt, x2, wg, wu, wd, gfin)


def _bucket_table():
    n = np.arange(MAX_DISTANCE + 1)
    exact = N_BUCKETS // 2
    nf = np.maximum(n, exact).astype(np.float64)
    large = exact + (np.log(nf / exact) / math.log(MAX_DISTANCE / exact) * (N_BUCKETS - exact)).astype(np.int32)
    return np.where(n < exact, n, np.minimum(large, N_BUCKETS - 1))


def _bias_of(rel_b, rel):
    idx = jnp.asarray(np.clip(rel, 0, MAX_DISTANCE).reshape(-1, 1), jnp.int32)
    onehot = (idx == jnp.arange(MAX_DISTANCE + 1, dtype=jnp.int32)[None, :]).astype(F32)
    t = jnp.dot(onehot, rel_b * LOG2E, precision=lax.Precision.HIGHEST, preferred_element_type=F32)
    return t.reshape(rel.shape + (N_HEADS,))


def _prompt_tile(rel_b, rel, mask, minus=None):
    r = rel.shape[0]
    t = _bias_of(rel_b, rel).reshape(r, QT, KV_HEADS, GROUP).transpose(2, 0, 3, 1).reshape(KV_HEADS, r, GQ)
    if minus is not None:
        t = t - minus
    m = np.broadcast_to(np.asarray(mask)[None, :, None, :], (KV_HEADS, r, GROUP, QT)).reshape(KV_HEADS, r, GQ)
    return jnp.where(jnp.asarray(m), t, NEG)


def _sample_tile(rel_b, rel, mask):
    r, ds = rel.shape
    t = _bias_of(rel_b, rel)
    t = jnp.where(jnp.asarray(mask)[..., None], t, NEG)
    return t.transpose(2, 1, 0).reshape(N_HEADS * ds, r)


def _block_diag(w):
    eye = jnp.eye(LRU_HEADS, dtype=w.dtype)
    return jnp.einsum('hij,hk->hikj', w, eye).reshape(D_LRU, D_LRU)


def kernel(x_prompt, x_sample, cache_kv, state_kv_win, state_conv, state_h, page_table, g_mix, w_in, conv_w, conv_b,
           w_gate_a, b_gate_a, w_gate_x, b_gate_x, lru_lambda, cmp_w1, cmp_b1, cmp_w2, w_lru_out, w_nsa_out, w_out,
           g_ffn, w_router_group, b_router_group, w_router_expert, b_router_expert, w_exp_gate, w_exp_up,
           w_exp_down, rel_bias, g_final):
    assert w_in.shape[0] == 1, "single layer"
    b, s, _ = x_prompt.shape
    db, ds, _ = x_sample.shape
    npg = page_table.shape[1]
    past = npg * PAGE_SIZE
    wb = state_kv_win.shape[2]
    assert s % 256 == 0 and s // L_SEL >= N_SEL and CONV_W - 1 <= ds <= STRIDE and wb == WINDOW and past >= WINDOW

    w = w_in[0].astype(BF16)
    o = 0
    ws = []
    for width in (D_LRU, D_LRU, Q_DIM, 4 * KV_DIM, 2 * KV_DIM, 3 * N_HEADS, 2 * D_MODEL):
        ws.append(w[:, o:o + width])
        o += width
    ws[5] = jnp.pad(ws[5], ((0, 0), (0, LANES - 3 * N_HEADS)))
    g_mix2 = g_mix[0][None]
    wg = jnp.concatenate([_block_diag(w_gate_x[0]), _block_diag(w_gate_a[0])], axis=1).astype(BF16)
    lru_args = (conv_w[0], conv_b[0][None], wg, b_gate_x[0][None], b_gate_a[0][None], lru_lambda[0][None])
    w1 = cmp_w1[0].reshape(2, 2, STRIDE, HEAD_DIM, CMP_HIDDEN)
    w1f = jnp.einsum('armdn,hk->amhdkrn', w1, jnp.eye(HEADS_PER_TILE, dtype=F32)).reshape(
        2, STRIDE * LANES, HEADS_PER_TILE * 2 * CMP_HIDDEN).astype(BF16)
    b1 = cmp_b1[0][:, None, :]
    w2 = cmp_w2[0].astype(BF16)
    wl, wn, wo = w_lru_out[0].astype(BF16), w_nsa_out[0].astype(BF16), w_out[0].astype(BF16)
    n_r = N_GROUPS + N_EXPERTS
    wr = jnp.pad(jnp.concatenate([w_router_group[0], w_router_expert[0]], axis=1), ((0, 0), (0, LANES - n_r)))
    br = jnp.pad(jnp.concatenate([b_router_group[0], b_router_expert[0]]), (0, LANES - n_r))[None]
    weg, weu, wed = w_exp_gate[0].astype(BF16), w_exp_up[0].astype(BF16), w_exp_down[0].astype(BF16)
    gf, gfin = g_ffn[0][None], g_final[None]
    rel_b = rel_bias.astype(F32)[_bucket_table()]

    def mix_ffn(x2d, ylru, ynsa, mg):
        x2, hn, wt = _merge(x2d, ylru, ynsa, mg, wl, wn, wo, gf, wr, br)
        return _moe(hn, wt, x2, weg, weu, wed, gfin)

    xp2 = x_prompt.reshape(b * s, D_MODEL)
    lx, lg, _, kv, kvw, ng, mg, qT, kv_t = _proj(xp2, g_mix2, ws, s, feature_major=(2, 3))
    y_lru, conv_p, h_p = _rglru_prompt(lx.reshape(b, s, D_LRU), lg.reshape(b, s, D_LRU), *lru_args)
    kv3 = kv.reshape(b, s, 4 * KV_DIM)
    kvw3 = kvw.reshape(b, s, 2 * KV_DIM)
    kcvc = _compress_prompt(kv3, w1f, b1, w2)
    nt = s // QT
    ncp = s // STRIDE
    nsb = s // L_SEL

    def key_tiles(x):
        return x.astype(BF16).reshape(b, nt, QT, KV_DIM)

    def val_tiles(x):
        return x.astype(BF16).reshape(b, nt, QT, KV_DIM).swapaxes(2, 3)

    qi = np.arange(QT)[None, :]
    lrow = np.arange(ncp)[:, None]
    rel_c = qi - STRIDE * lrow + STRIDE * ncp - (STRIDE * 8 + L_CMP - 1)
    n_far = int(np.sum(np.all(rel_c >= MAX_DISTANCE, axis=1)))
    crow = _prompt_tile(rel_b, np.full((1, QT), MAX_DISTANCE), np.ones((1, QT), bool))
    tc = jnp.concatenate([jnp.broadcast_to(crow, (KV_HEADS, n_far, GQ)),
                          _prompt_tile(rel_b, rel_c[n_far:], rel_c[n_far:] >= 0)], axis=1)
    kj = np.arange(QT)[:, None]
    diag = _prompt_tile(rel_b, qi - kj, qi - kj >= 0, crow)
    tiles = jnp.stack([diag, _prompt_tile(rel_b, WINDOW + qi - kj, qi - kj < 0, crow)])
    near = jnp.concatenate([_prompt_tile(rel_b, QT + qi - kj, np.ones((QT, QT), bool), crow), diag], axis=1)
    rr = np.arange(nsb)[:, None]
    ll = np.arange(ncp)[None, :]
    ov = jnp.asarray((ll >= 4 * rr - 1) & (ll <= 4 * rr + 3), BF16)
    gT = ng[:, :3 * N_HEADS].reshape(b, s, 3 * N_HEADS).swapaxes(1, 2)
    bpt = QT // L_SEL
    code = (np.arange(nt)[:, None] % 2) * bpt + np.arange(QT)[None, :] // L_SEL
    mark = jnp.asarray(code[..., None] == np.arange(LANES - HEAD_DIM)[None, None, :], BF16)
    ks = kv3[..., 2 * KV_DIM:3 * KV_DIM].astype(BF16).reshape(b, nt, QT, KV_HEADS, HEAD_DIM)
    ksel_aug = jnp.concatenate([ks, jnp.broadcast_to(mark[None, :, :, None, :], ks.shape[:4] + (LANES - HEAD_DIM,))],
                               axis=-1).reshape(b, nt, QT, KV_HEADS * LANES)
    y_nsaT = _nsa_prompt(qT, gT, kcvc, ksel_aug, val_tiles(kv3[..., 3 * KV_DIM:]), key_tiles(kvw3[..., :KV_DIM]),
                         val_tiles(kvw3[..., KV_DIM:]), tc, tiles, near, ov)
    y_nsa = y_nsaT.swapaxes(1, 2).reshape(b * s, Q_DIM)
    y_prompt = mix_ffn(xp2, y_lru.reshape(b * s, D_LRU), y_nsa, mg).reshape(b, s, D_MODEL)
    kv_rows_prompt = kv_t.reshape(1, b, 4, KV_HEADS, HEAD_DIM, s).transpose(0, 1, 5, 2, 3, 4)
    win_prompt = kvw3[:, s - min(WINDOW, s):].reshape(1, b, min(WINDOW, s), 2, KV_HEADS, HEAD_DIM)

    xs2 = x_sample.reshape(db * ds, D_MODEL)
    lx, lg, q, kv, kvw, ng, mg = _proj(xs2, g_mix2, ws, db * ds)
    tmaj =lambda a: a.reshape(db, ds, D_LRU).swapaxes(0, 1)
    y_lru_t, conv_t, h_s = _rglru_sample(tmaj(lx), tmaj(lg), state_conv[0].swapaxes(0, 1), state_h[0], *lru_args)
    y_lru = y_lru_t.swapaxes(0, 1).reshape(db * ds, D_LRU)
    kv_s3 = kv.reshape(db, ds, 4 * KV_DIM)
    kvw_s3 = kvw.reshape(db, ds, 2 * KV_DIM)
    cache_t = cache_kv[0].transpose(0, 2, 3, 4, 1).reshape(cache_kv.shape[1], 4 * KV_DIM, PAGE_SIZE)
    win_t = state_kv_win[0].transpose(0, 2, 3, 4, 1).reshape(db, 2 * KV_DIM, wb)
    new_c = jnp.pad(kv_s3[..., :2 * KV_DIM], ((0, 0), (0, PAGE_SIZE - ds), (0, 0)))
    pps = 32
    assert npg % pps == 0 and LANES % (pps * PAGE_SIZE // L_SEL) == 0
    kcvc_s = _compress_sample(cache_t, page_table, new_c, w1f, b1, w2, pps)

    nsb_s = past // L_SEL + 1
    ntok = past // STRIDE
    jq = np.arange(ds)[None, :]
    nrow = np.arange(ntok)[:, None]
    rel = past + jq - STRIDE * nrow - (L_CMP - 1)
    bc = _sample_tile(rel_b, rel, rel >= 0)
    rn = np.arange(PAGE_SIZE)[:, None]
    new_rel, new_ok = jq - rn, (jq - rn >= 0) & (rn < ds)
    rw = np.arange(wb)[:, None]
    bw = _sample_tile(rel_b, np.concatenate([wb + jq - rw, new_rel], axis=0),
                      np.concatenate([(wb + jq - rw < WINDOW) & (past - wb + rw >= 0), new_ok], axis=0))
    bnew = _sample_tile(rel_b, new_rel, new_ok)
    step_keys = pps * PAGE_SIZE
    far_rel = np.full((step_keys, ds), MAX_DISTANCE)
    last_rel = far_rel.copy()
    last_rel[step_keys - PAGE_SIZE:] = PAGE_SIZE + jq - rn
    all_ok = np.ones((step_keys, ds), bool)
    bstep = jnp.stack([_sample_tile(rel_b, far_rel, all_ok), _sample_tile(rel_b, last_rel, all_ok)])
    nlane = -(-nsb_s // LANES) * LANES
    bb = np.arange(nlane)[None, :]
    tt = np.arange(ntok)[:, None]
    ovt = jnp.asarray((tt >= 4 * bb - 1) & (tt <= 4 * bb + 3) & (bb < nsb_s), BF16)
    ncol = N_HEADS * ds
    col = np.arange(ncol)
    kvh_c, j_c = col // (GROUP * ds), col % ds
    gg = jnp.asarray((kvh_c[:, None] == kvh_c[None, :]) & (j_c[:, None] == j_c[None, :]), BF16)
    bps = pps * PAGE_SIZE // L_SEL
    off = np.arange(LANES // bps)[:, None, None]
    expand = jnp.asarray(np.arange(LANES)[None, :, None] == bps * off + np.arange(step_keys)[None, None, :] // L_SEL,
                         BF16)
    q5 = q.reshape(db, ds, KV_HEADS, GROUP, HEAD_DIM) * QK_SCALE
    qbd = jnp.einsum('bjkgd,kc->bkgjcd', q5, jnp.eye(KV_HEADS, dtype=F32)).reshape(db, ncol, KV_DIM).astype(BF16)
    gates_s = ng[:, :3 * N_HEADS].reshape(db, ds, N_HEADS, 3).transpose(0, 2, 1, 3).reshape(db, ncol, 3)
    new_all = jnp.pad(jnp.concatenate([kv_s3[..., 2 * KV_DIM:], kvw_s3], axis=-1),
                      ((0, 0), (0, PAGE_SIZE - ds), (0, 0)))
    o_bd = _nsa_sample(cache_t, page_table, qbd, gates_s, kcvc_s, win_t, new_all, bc, bw, bstep, bnew, ovt, gg,
                       expand, pps)
    o6 = o_bd.reshape(db, KV_HEADS, GROUP, ds, KV_HEADS, HEAD_DIM)
    y_nsa = jnp.einsum('bkgjkd->bjkgd', o6).reshape(db * ds, Q_DIM)
    y_sample = mix_ffn(xs2, y_lru, y_nsa, mg).reshape(db, ds, D_MODEL)
    kv_rows_sample = kv_s3.reshape(1, db, ds, 4, KV_HEADS, HEAD_DIM)
    win_all = jnp.concatenate([state_kv_win[0].reshape(db, wb, 2 * KV_DIM), kvw_s3], axis=1)
    win_sample = win_all[:, win_all.shape[1] - WINDOW:].reshape(1, db, WINDOW, 2, KV_HEADS, HEAD_DIM)
    conv_s = conv_t.swapaxes(0, 1)

    return (y_prompt, y_sample, kv_rows_prompt, win_prompt, conv_p[None], h_p.reshape(1, b, D_LRU),
            kv_rows_sample, win_sample, conv_s[None], h_s[None])
```

```python
import functools
import math

import numpy as np
import jax
import jax.numpy as jnp
from jax import lax
from jax.experimental import pallas as pl
from jax.experimental.pallas import tpu as pltpu

F32 = jnp.float32
BF16 = jnp.bfloat16

D_MODEL = 1024
D_LRU = 1280
LRU_HEADS = 16
LRU_BLOCK = D_LRU // LRU_HEADS
CONV_W = 4
LRU_C = 8.0
N_HEADS = 16
HEAD_DIM = 64
KV_HEADS = 4
GROUP = N_HEADS // KV_HEADS
L_CMP = 32
STRIDE = 16
CMP_HIDDEN = 128
L_SEL = 64
N_SEL = 16
N_LOCAL = 2
N_FREE = N_SEL - (1 + N_LOCAL)
WINDOW = 512
PAGE_SIZE = 128
N_BUCKETS = 32
MAX_DISTANCE = 128
N_GROUPS = 4
EXPERTS_PER_GROUP = 8
N_EXPERTS = N_GROUPS * EXPERTS_PER_GROUP
D_EXPERT = 256
EPS = 1e-6
NEG = -1e30
FORCED = 1e9
Q_DIM = N_HEADS * HEAD_DIM
KV_DIM = KV_HEADS * HEAD_DIM
QT = 128
GQ = GROUP * QT
LANES = 128
SUB = 8
VMEM_LIMIT = 56 * 1024 * 1024
KNOCKED = -3e38
ABSENT = -2e38
LOG2E = math.log2(math.e)
QK_SCALE = HEAD_DIM ** -0.5 * LOG2E
MASK_ROWS = 16
ONES_ROWS = 16


def _cparams(sem):
    return pltpu.CompilerParams(dimension_semantics=sem, vmem_limit_bytes=VMEM_LIMIT)


def _dot(a, b):
    return jnp.dot(a, b, preferred_element_type=F32)


def _dot_tn(a, b):
    return lax.dot_general(a, b, (((0,), (0,)), ((), ())), preferred_element_type=F32)


def _dot_nt(a, b):
    return lax.dot_general(a, b, (((1,), (1,)), ((), ())), preferred_element_type=F32)


def _split3(x):
    hi = x.astype(BF16)
    r1 = x - hi.astype(F32)
    mid = r1.astype(BF16)
    lo = (r1 - mid.astype(F32)).astype(BF16)
    return hi, mid, lo


def _dot_exact_rhs(a_bf16, x):
    hi, mid, lo = _split3(x)
    return _dot(a_bf16, hi) + _dot(a_bf16, mid) + _dot(a_bf16, lo)


def _dot_exact_lhs(x, b_bf16):
    hi, mid, lo = _split3(x)
    return _dot(hi, b_bf16) + _dot(mid, b_bf16) + _dot(lo, b_bf16)


def _proj_kernel(x_ref, g_ref, *refs, n, feature_major):
    x = x_ref[...]
    xn = x * lax.rsqrt(jnp.mean(x * x, axis=-1, keepdims=True) + EPS) * g_ref[...]
    xb = xn.astype(BF16)
    outs = {}
    for j, (w_ref, o_ref) in enumerate(zip(refs[:n], refs[n:2 * n])):
        outs[j] = _dot(xb, w_ref[...])
        o_ref[...] = outs[j]
    for j, o_ref in zip(feature_major, refs[2 * n:]):
        o_ref[0] = outs[j].T


def _proj(x2d, g, ws, rows_per_seq, feature_major=(), tm=256):
    t = x2d.shape[0]
    tm = min(tm, t, rows_per_seq)
    tps = rows_per_seq // tm
    in_specs = [pl.BlockSpec((tm, D_MODEL), lambda i: (i, 0)), pl.BlockSpec((1, D_MODEL), lambda i: (0, 0))]
    in_specs += [pl.BlockSpec(w.shape, lambda i: (0, 0), pipeline_mode=pl.Buffered(1)) for w in ws]
    out_specs = [pl.BlockSpec((tm, w.shape[1]), lambda i: (i, 0)) for w in ws]
    out_shape = [jax.ShapeDtypeStruct((t, w.shape[1]), F32) for w in ws]
    for j in feature_major:
        out_specs.append(pl.BlockSpec((1, ws[j].shape[1], tm), lambda i: (i // tps, 0, i % tps)))
        out_shape.append(jax.ShapeDtypeStruct((t // rows_per_seq, ws[j].shape[1], rows_per_seq), F32))
    return pl.pallas_call(functools.partial(_proj_kernel, n=len(ws), feature_major=tuple(feature_major)),
                          grid=(t // tm,), in_specs=in_specs, out_specs=out_specs, out_shape=out_shape,
                          compiler_params=_cparams(("parallel",)), name="proj")(x2d, g, *ws)


def _softplus(x):
    return jnp.maximum(x, 0.0) + jnp.log1p(jnp.exp(-jnp.abs(x)))


def _lru_gates(xc, wg_ref, bgx_ref, bga_ref, lam_ref):
    gates = _dot(xc.astype(BF16), wg_ref[...])
    gx = jax.nn.sigmoid(gates[:, :D_LRU] + bgx_ref[...])
    ga = jax.nn.sigmoid(gates[:, D_LRU:] + bga_ref[...])
    log_a = -LRU_C * ga * _softplus(-lam_ref[...])
    a = jnp.exp(log_a)
    th = jnp.tanh(log_a)
    u = jnp.sqrt(-2.0 * th / (1.0 - th)) * (gx * xc)
    return a, u


def _rglru_prompt_kernel(lx_ref, lg_ref, cw_ref, cb_ref, wg_ref, bgx_ref, bga_ref, lam_ref,
                         y_ref, conv_ref, h_ref, xext, hc, *, tc):
    t = pl.program_id(1)

    @pl.when(t == 0)
    def _():
        xext[0:8, :] = jnp.zeros((8, D_LRU), F32)
        hc[...] = jnp.zeros_like(hc)

    x = lx_ref[0]
    xext[8:8 + tc, :] = x
    cw = cw_ref[...]
    xc = cb_ref[...] + cw[3:4] * x
    for j in range(CONV_W - 1):
        xc = xc + cw[j:j + 1] * xext[5 + j:5 + j + tc, :]
    a, u = _lru_gates(xc, wg_ref, bgx_ref, bga_ref, lam_ref)
    sub = lax.broadcasted_iota(jnp.int32, (tc, D_LRU), 0) % SUB
    s = 1
    while s < SUB:
        a_sh = pltpu.roll(a, s, 0)
        u_sh = pltpu.roll(u, s, 0)
        m = sub >= s
        u = jnp.where(m, a * u_sh + u, u)
        a = jnp.where(m, a * a_sh, a)
        s *= 2
    carry = hc[0:1, :]
    lg = lg_ref[0]
    for g in range(tc // SUB):
        rows = slice(g * SUB, (g + 1) * SUB)
        h = a[rows] * carry + u[rows]
        carry = h[SUB - 1:SUB, :]
        y_ref[0, rows, :] = h * jax.nn.gelu(lg[rows])
    hc[0:1, :] = carry
    xext[0:8, :] = x[tc - 8:tc, :]
    conv_ref[0] = x[tc - (CONV_W - 1):tc, :]
    h_ref[0] = carry


def _rglru_prompt(lx, lg, cw, cb, wg, bgx, bga, lam, tc=256):
    b, s, _ = lx.shape
    row = lambda shape: pl.BlockSpec(shape, lambda bi, ti: (0, 0))
    return pl.pallas_call(
        functools.partial(_rglru_prompt_kernel, tc=tc), grid=(b, s // tc),
        in_specs=[pl.BlockSpec((1, tc, D_LRU), lambda bi, ti: (bi, ti, 0)),
                  pl.BlockSpec((1, tc, D_LRU), lambda bi, ti: (bi, ti, 0)),
                  row((CONV_W, D_LRU)), row((1, D_LRU)), row((D_LRU, 2 * D_LRU)),
                  row((1, D_LRU)), row((1, D_LRU)), row((1, D_LRU))],
        out_specs=[pl.BlockSpec((1, tc, D_LRU), lambda bi, ti: (bi, ti, 0)),
                   pl.BlockSpec((1, CONV_W - 1, D_LRU), lambda bi, ti: (bi, 0, 0)),
                   pl.BlockSpec((1, 1, D_LRU), lambda bi, ti: (bi, 0, 0))],
        out_shape=[jax.ShapeDtypeStruct((b, s, D_LRU), F32),
                   jax.ShapeDtypeStruct((b, CONV_W - 1, D_LRU), F32),
                   jax.ShapeDtypeStruct((b, 1, D_LRU), F32)],
        scratch_shapes=[pltpu.VMEM((tc + 8, D_LRU), F32), pltpu.VMEM((8, D_LRU), F32)],
        compiler_params=_cparams(("parallel", "arbitrary")), name="rglru_prompt",
    )(lx, lg, cw, cb, wg, bgx, bga, lam)


def _rglru_sample_kernel(lx_ref, lg_ref, cbuf_ref, h0_ref, cw_ref, cb_ref, wg_ref, bgx_ref, bga_ref, lam_ref,
                         y_ref, conv_ref, h_ref, *, ds):
    cw = cw_ref[...]
    xp = [cbuf_ref[j] for j in range(CONV_W - 1)] + [lx_ref[j] for j in range(ds)]
    h = h0_ref[...]
    for t in range(ds):
        xc = cb_ref[...]
        for j in range(CONV_W):
            xc = xc + cw[j:j + 1] * xp[t + j]
        a, u = _lru_gates(xc, wg_ref, bgx_ref, bga_ref, lam_ref)
        h = a * h + u
        y_ref[t] = h * jax.nn.gelu(lg_ref[t])
    for j in range(CONV_W - 1):
        conv_ref[j] = xp[ds + j]
    h_ref[...] = h


def _rglru_sample(lx_t, lg_t, cbuf_t, h0, cw, cb, wg, bgx, bga, lam):
    ds, n, _ = lx_t.shape
    full = lambda a: pl.BlockSpec(a.shape, lambda i: (0,) * a.ndim)
    args = (lx_t, lg_t, cbuf_t, h0, cw, cb, wg, bgx, bga, lam)
    out_shape = [jax.ShapeDtypeStruct((ds, n, D_LRU), F32), jax.ShapeDtypeStruct((CONV_W - 1, n, D_LRU), F32),
                 jax.ShapeDtypeStruct((n, D_LRU), F32)]
    return pl.pallas_call(
        functools.partial(_rglru_sample_kernel, ds=ds), grid=(1,),
        in_specs=[full(a) for a in args], out_specs=[full(o) for o in out_shape], out_shape=out_shape,
        compiler_params=_cparams(("arbitrary",)), name="rglru_sample")(*args)


HEADS_PER_TILE = LANES // HEAD_DIM


def _chunk_project(load, w):
    feat = jnp.concatenate([load(s) for s in range(STRIDE)], axis=1).astype(BF16)
    return _dot(feat, w)


def _compress_prompt_kernel(xa_ref, xb_ref, w1_ref, b1_ref, w2_ref, o_ref, *, ncp):
    rows = lax.broadcasted_iota(jnp.int32, (ncp, HEAD_DIM), 0)
    for c, x_ref in enumerate((xa_ref, xb_ref)):
        p2 = _chunk_project(lambda s: x_ref[0, pl.ds(s, ncp, stride=STRIDE), :], w1_ref[0])
        for hh in range(HEADS_PER_TILE):
            k = c * HEADS_PER_TILE + hh
            p = p2[:, hh * 2 * CMP_HIDDEN:(hh + 1) * 2 * CMP_HIDDEN]
            pre = b1_ref[0] + p[:, :CMP_HIDDEN] + pltpu.roll(p[:, CMP_HIDDEN:], ncp - 1, 0)
            phi = _dot(jax.nn.gelu(pre).astype(BF16), w2_ref[0])
            o_ref[0, 0, k, 0:ncp, :] = jnp.zeros((ncp, HEAD_DIM), F32)
            o_ref[0, 0, k, ncp:2 * ncp, :] = jnp.where(rows < ncp - 1, phi, 0.0)


def _compress_prompt(kv, w1f, b1, w2):
    b, s, _ = kv.shape
    ncp = s // STRIDE
    return pl.pallas_call(
        functools.partial(_compress_prompt_kernel, ncp=ncp), grid=(b, 2),
        in_specs=[pl.BlockSpec((1, s, LANES), lambda bi, sl: (bi, 0, 2 * sl)),
                  pl.BlockSpec((1, s, LANES), lambda bi, sl: (bi, 0, 2 * sl + 1)),
                  pl.BlockSpec((1,) + w1f.shape[1:], lambda bi, sl: (sl, 0, 0)),
                  pl.BlockSpec((1, 1, CMP_HIDDEN), lambda bi, sl: (sl, 0, 0)),
                  pl.BlockSpec((1, CMP_HIDDEN, HEAD_DIM), lambda bi, sl: (sl, 0, 0))],
        out_specs=pl.BlockSpec((1, 1, KV_HEADS, 2 * ncp, HEAD_DIM), lambda bi, sl: (sl, bi, 0, 0, 0)),
        out_shape=jax.ShapeDtypeStruct((2, b, KV_HEADS, 2 * ncp, HEAD_DIM), F32),
        compiler_params=_cparams(("parallel", "parallel")), name="compress_prompt")(kv, kv, w1f, b1, w2)


def _compress_sample_kernel(pt_ref, *refs, pps, nsteps):
    pages = refs[:pps]
    new_ref, w1_ref, b1_ref, w2_ref, o_ref, xs, pscr = refs[pps:]
    st = pl.program_id(1)
    cps = PAGE_SIZE // STRIDE
    m = pps * cps
    ntok = nsteps * m
    tiles_per_slot = KV_DIM // LANES

    def stage(ref, j, transposed):
        for c in range(2 * tiles_per_slot):
            if transposed:
                blk = ref[0, c * LANES:(c + 1) * LANES, :].T
            else:
                blk = ref[0, :, c * LANES:(c + 1) * LANES]
            xs[c, j * PAGE_SIZE:(j + 1) * PAGE_SIZE, :] = blk

    def project(nrows, row0):
        for sl in range(2):
            for c in range(tiles_per_slot):
                p2 = _chunk_project(lambda s: xs[sl * tiles_per_slot + c, pl.ds(s, nrows, stride=STRIDE), :],
                                    w1_ref[sl])
                for hh in range(HEADS_PER_TILE):
                    pscr[sl, c * HEADS_PER_TILE + hh, pl.ds(row0, nrows), :] = (
                        p2[:, hh * 2 * CMP_HIDDEN:(hh + 1) * 2 * CMP_HIDDEN])

    for j, pg in enumerate(pages):
        stage(pg, j, True)
    project(m, pl.multiple_of(st * m, m))

    @pl.when(st == nsteps - 1)
    def _():
        stage(new_ref, 0, False)
        project(cps, ntok)
        for sl in range(2):
            toks = []
            for k in range(KV_HEADS):
                pre = (b1_ref[sl] + pscr[sl, k, 0:ntok, 0:CMP_HIDDEN]
                       + pscr[sl, k, 1:ntok + 1, CMP_HIDDEN:2 * CMP_HIDDEN])
                toks.append(_dot(jax.nn.gelu(pre).astype(BF16), w2_ref[sl]))
            o_ref[0, sl] = jnp.concatenate(toks, axis=1)


def _compress_sample(cache, page_table, new_rows, w1f, b1, w2, pps=16):
    db, npg = page_table.shape
    pps = min(pps, npg)
    nsteps = npg // pps
    cps = PAGE_SIZE // STRIDE
    ntok = npg * cps

    def page_spec(j):
        return pl.BlockSpec((1, 2 * KV_DIM, PAGE_SIZE), lambda bi, st, pt: (pt[bi, st * pps + j], 0, 0))

    const = lambda shape: pl.BlockSpec(shape, lambda bi, st, pt: (0,) * len(shape))
    grid_spec = pltpu.PrefetchScalarGridSpec(
        num_scalar_prefetch=1, grid=(db, nsteps),
        in_specs=[page_spec(j) for j in range(pps)] + [
            pl.BlockSpec((1, PAGE_SIZE, 2 * KV_DIM), lambda bi, st, pt: (bi, 0, 0)),
            const(w1f.shape), const((2, 1, CMP_HIDDEN)),
            const((2, CMP_HIDDEN, HEAD_DIM))],
        out_specs=pl.BlockSpec((1, 2, ntok, KV_DIM), lambda bi, st, pt: (bi, 0, 0, 0)),
        scratch_shapes=[pltpu.VMEM((2 * KV_DIM // LANES, pps * PAGE_SIZE, LANES), F32),
                        pltpu.VMEM((2, KV_HEADS, ntok + cps, 2 * CMP_HIDDEN), F32)])
    return pl.pallas_call(
        functools.partial(_compress_sample_kernel, pps=pps, nsteps=nsteps), grid_spec=grid_spec,
        out_shape=jax.ShapeDtypeStruct((db, 2, ntok, KV_DIM), F32),
        compiler_params=_cparams(("parallel", "arbitrary")), name="compress_sample",
    )(page_table, *([cache] * pps), new_rows, w1f, b1, w2)


def _topk_masks(scores, axis):
    n = scores[0].shape[axis]
    idx = lax.broadcasted_iota(jnp.int32, scores[0].shape, axis).astype(F32)

    def body(_, scs):
        out = []
        for sc in scs:
            mx = jnp.max(sc, axis=axis, keepdims=True)
            first = jnp.min(jnp.where(sc == mx, idx, float(n)), axis=axis, keepdims=True)
            out.append(jnp.where(idx == first, KNOCKED, sc))
        return tuple(out)

    return [jnp.where(sc == KNOCKED, 1.0, 0.0) for sc in lax.fori_loop(0, N_FREE, body, tuple(scores))]


def _softmax2(s, axis):
    m = jnp.max(s, axis=axis, keepdims=True)
    e = jnp.where(s > 0.5 * NEG, jnp.exp2(s - m), 0.0)
    l = jnp.sum(e, axis=axis, keepdims=True)
    return e * (1.0 / jnp.maximum(l, 1e-30))


def _nsa_prompt_kernel(qT_ref, gT_ref, kc_ref, vc_ref, ksel_ref, vselT_ref, kwin_ref, vwinT_ref,
                       tc_ref, tiles_ref, near_ref, ov_ref, y_ref,
                       qk_s, madd_s, oc_s, m_s, acc_s, s_scr, p_scr, alpha_s, *, ncp, nsb):
    i = pl.program_id(1)
    qT = qT_ref[0]
    for k in range(KV_HEADS):
        qk_s[k, 0:HEAD_DIM, :] = (jnp.concatenate([qT[(GROUP * k + g) * HEAD_DIM:(GROUP * k + g + 1) * HEAD_DIM, :]
                                                   for g in range(GROUP)], axis=1) * QK_SCALE).astype(BF16)
        qk_s[k, HEAD_DIM:LANES, :] = jnp.zeros((LANES - HEAD_DIM, GQ), BF16)
    m_s[...] = jnp.full(m_s.shape, NEG, F32)
    acc_s[...] = jnp.zeros_like(acc_s)

    start = pl.multiple_of(8 * i + 8, 8)
    tok_ok = lax.broadcasted_iota(jnp.int32, (ncp, 1), 0) >= ncp - 8 - 8 * i
    rblk = lax.broadcasted_iota(jnp.int32, (nsb, QT), 0)
    qhalf = jnp.where(lax.broadcasted_iota(jnp.int32, (nsb, QT), 1) >= L_SEL, 1, 0)
    r_qb = nsb - 2 + qhalf
    exists = rblk >= nsb - 2 - 2 * i
    forced = (rblk == nsb - 2 - 2 * i) | (rblk > r_qb - N_LOCAL)
    visible = exists & (rblk <= r_qb)
    forced_ok = visible & forced
    free = visible & jnp.logical_not(forced)
    def block_scores(k, r0):
        rs = pl.multiple_of(start + r0, 8)
        kc = kc_ref[0, 0, k, pl.ds(rs, ncp - r0), :].astype(BF16)
        vc = vc_ref[0, 0, k, pl.ds(rs, ncp - r0), :].astype(BF16)
        s = jnp.where(tok_ok[r0:], _dot(kc, qk_s[k, 0:HEAD_DIM, :]) + tc_ref[k, r0:ncp, :], NEG)
        pn = _softmax2(s, 0)
        oc_s[k] = _dot_tn(vc, pn.astype(BF16))
        psum = pn[:, 0:QT]
        for g in range(1, GROUP):
            psum = psum + pn[:, g * QT:(g + 1) * QT]
        imp = _dot_exact_rhs(ov_ref[:, r0:ncp], psum)
        madd_s[k] = jnp.where(free, imp, ABSENT)

    quarter = ncp // 4
    for v in range(4):
        lo, hi = v * quarter // 8, (v + 1) * quarter // 8
        cond = (i >= lo) & (i < hi) if v < 3 else i >= lo

        @pl.when(cond)
        def _(v=v):
            for k in range(KV_HEADS):
                block_scores(k, ncp - (v + 1) * quarter)

    for k0 in range(0, KV_HEADS, 2):
        for j, sel in enumerate(_topk_masks([madd_s[k0], madd_s[k0 + 1]], 0)):
            madd_s[k0 + j] = jnp.where(forced_ok | (sel > 0.0), 0.0, NEG)

    bpt = QT // L_SEL

    def logits(t, n, k_ref, slot, masked):
        kt = k_ref[0, t] if n == 1 else jnp.concatenate([k_ref[0, t + j] for j in range(n)], axis=0)
        for k in range(KV_HEADS):
            if masked:
                r0 = bpt * (t - i) + nsb - bpt
                par = t % 2
                first = [r0 + (bpt * par if n == 2 else 0) + h for h in range(bpt)]
                second = [r0 + (bpt * (1 - par) if n == 2 else 0) + h for h in range(bpt)]
                rows = [jnp.concatenate([madd_s[k, pl.ds(r, 1), :]] * GROUP, axis=1) for r in first + second]
                blk = jnp.concatenate(rows + [jnp.zeros((MASK_ROWS - 2 * bpt, GQ), F32)], axis=0)
                qk_s[k, HEAD_DIM:HEAD_DIM + MASK_ROWS, :] = blk.astype(BF16)
                s_scr[slot, k, 0:n * QT, :] = _dot(kt[:, k * LANES:(k + 1) * LANES], qk_s[k])
            else:
                s_scr[slot, k, 0:n * QT, :] = _dot(kt[:, k * HEAD_DIM:(k + 1) * HEAD_DIM], qk_s[k, 0:HEAD_DIM, :])

    def softmax(n, add_of, branch, slot):
        nk = n * QT
        for k in range(KV_HEADS):
            idx = branch * KV_HEADS + k
            s = s_scr[slot, k, 0:nk, :]
            if add_of is not None:
                s = s + add_of(k)
            m_old = m_s[idx]
            m_new = jnp.maximum(m_old, jnp.max(s, axis=0, keepdims=True))
            alpha_s[slot, k] = jnp.exp2(m_old - m_new)
            m_s[idx] = m_new
            p_scr[slot, k, 0:nk, :] = jnp.exp2((s - m_new).astype(BF16))

    def values(t, n, vT_ref, branch, slot):
        nk = n * QT
        vt = vT_ref[0, t] if n == 1 else jnp.concatenate([vT_ref[0, t + j] for j in range(n)], axis=1)
        ones = jnp.ones((ONES_ROWS, nk), BF16)
        for k in range(KV_HEADS):
            idx = branch * KV_HEADS + k
            vt_aug = jnp.concatenate([vt[k * HEAD_DIM:(k + 1) * HEAD_DIM, :], ones], axis=0)
            acc_s[idx] = alpha_s[slot, k] * acc_s[idx] + _dot(vt_aug, p_scr[slot, k, 0:nk, :])

    def attend(t, n, k_ref, vT_ref, add_of, branch):
        logits(t, n, k_ref, 0, branch == 0)
        softmax(n, add_of, branch, 0)
        values(t, n, vT_ref, branch, 0)

    nfar = jnp.maximum(i - 1, 0)
    npair = nfar // 2

    last_pair = 2 * (npair - 1)

    @pl.when(npair >= 2)
    def _():
        logits(0, 2, ksel_ref, 0, True)
        p_scr[1] = jnp.zeros(p_scr.shape[1:], BF16)
        alpha_s[1] = jnp.ones(alpha_s.shape[1:], F32)

    def far_body(u, c):
        ta = 4 * u
        logits(ta + 2, 2, ksel_ref, 1, True)
        values(jnp.maximum(ta - 2, 0), 2, vselT_ref, 0, 1)
        softmax(2, None, 0, 0)
        logits(jnp.minimum(ta + 4, last_pair), 2, ksel_ref, 0, True)
        values(ta, 2, vselT_ref, 0, 0)
        softmax(2, None, 0, 1)
        return c

    ntrip = npair // 2
    lax.fori_loop(0, ntrip, far_body, 0)

    @pl.when(ntrip > 0)
    def _():
        values(4 * ntrip - 2, 2, vselT_ref, 0, 1)

    @pl.when(npair % 2 == 1)
    def _():
        attend(last_pair, 2, ksel_ref, vselT_ref, None, 0)

    @pl.when(nfar % 2 == 1)
    def _():
        attend(nfar - 1, 1, ksel_ref, vselT_ref, None, 0)

    near = lambda k: near_ref[k]
    win_edge = (i - 4, 1, kwin_ref, vwinT_ref, lambda k: tiles_ref[1, k], 1)
    win_far = (i - 3, 2, kwin_ref, vwinT_ref, None, 1)
    sel_near = (i - 1, 2, ksel_ref, vselT_ref, near, 0)
    win_near = (i - 1, 2, kwin_ref, vwinT_ref, near, 1)

    @pl.when(i >= 4)
    def _():
        items = (win_edge, win_far, sel_near, win_near)
        logits(items[0][0], items[0][1], items[0][2], 0, items[0][5] == 0)
        for j, (t, n, k_ref, vT_ref, add_of, branch) in enumerate(items):
            slot = j % 2
            if j + 1 < len(items):
                tn, nn, kn = items[j + 1][:3]
                logits(tn, nn, kn, 1 - slot, items[j + 1][5] == 0)
            if j >= 1:
                tp, np_, _, vp, _, bp = items[j - 1]
                values(tp, np_, vp, bp, 1 - slot)
            softmax(n, add_of, branch, slot)
        tp, np_, _, vp, _, bp = items[-1]
        values(tp, np_, vp, bp, (len(items) - 1) % 2)

    @pl.when(i == 3)
    def _():
        attend(*win_far)

    @pl.when(i == 2)
    def _():
        attend(0, 1, kwin_ref, vwinT_ref, None, 1)

    @pl.when((i >= 1) & (i < 4))
    def _():
        attend(*sel_near)
        attend(*win_near)

    @pl.when(i == 0)
    def _():
        attend(0, 1, ksel_ref, vselT_ref, lambda k: tiles_ref[0, k], 0)
        attend(0, 1, kwin_ref, vwinT_ref, lambda k: tiles_ref[0, k], 1)

    gate = jax.nn.sigmoid(gT_ref[0])
    for k in range(KV_HEADS):
        o_s = acc_s[k, 0:HEAD_DIM] * (1.0 / acc_s[k, HEAD_DIM:HEAD_DIM + 1])
        o_w = acc_s[KV_HEADS + k, 0:HEAD_DIM] * (1.0 / acc_s[KV_HEADS + k, HEAD_DIM:HEAD_DIM + 1])
        o_c = oc_s[k]
        for g in range(GROUP):
            h = GROUP * k + g
            cols = slice(g * QT, (g + 1) * QT)
            y_ref[0, h * HEAD_DIM:(h + 1) * HEAD_DIM, :] = (
                gate[3 * h:3 * h + 1] * o_c[:, cols] + gate[3 * h + 1:3 * h + 2] * o_s[:, cols]
                + gate[3 * h + 2:3 * h + 3] * o_w[:, cols])


def _nsa_prompt(qT, gT, kcvc, ksel, vselT, kwin, vwinT, tc, tiles, near, ov):
    b, _, s = qT.shape
    nq = s // QT
    ncp = s // STRIDE
    nsb = s // L_SEL
    seq4 = lambda a: pl.BlockSpec((1,) + a.shape[1:], lambda bi, qi: (bi, 0, 0, 0), pipeline_mode=pl.Buffered(1))
    const = lambda a: pl.BlockSpec(a.shape, lambda bi, qi: (0,) * a.ndim, pipeline_mode=pl.Buffered(1))
    cmp_spec = lambda sl: pl.BlockSpec((1, 1, KV_HEADS, 2 * ncp, HEAD_DIM), lambda bi, qi: (sl, bi, 0, 0, 0),
                                       pipeline_mode=pl.Buffered(1))
    return pl.pallas_call(
        functools.partial(_nsa_prompt_kernel, ncp=ncp, nsb=nsb), grid=(b, nq),
        in_specs=[pl.BlockSpec((1, Q_DIM, QT), lambda bi, qi: (bi, 0, qi)),
                  pl.BlockSpec((1, 3 * N_HEADS, QT), lambda bi, qi: (bi, 0, qi)),
                  cmp_spec(0), cmp_spec(1), seq4(ksel), seq4(vselT), seq4(kwin), seq4(vwinT),
                  const(tc), const(tiles), const(near), const(ov)],
        out_specs=pl.BlockSpec((1, Q_DIM, QT), lambda bi, qi: (bi, 0, qi)),
        out_shape=jax.ShapeDtypeStruct((b, Q_DIM, s), F32),
        scratch_shapes=[pltpu.VMEM((KV_HEADS, LANES, GQ), BF16), pltpu.VMEM((KV_HEADS, nsb, QT), F32),
                        pltpu.VMEM((KV_HEADS, HEAD_DIM, GQ), F32), pltpu.VMEM((2 * KV_HEADS, 1, GQ), F32),
                        pltpu.VMEM((2 * KV_HEADS, HEAD_DIM + ONES_ROWS, GQ), F32),
                        pltpu.VMEM((2, KV_HEADS, 2 * QT, GQ), F32), pltpu.VMEM((2, KV_HEADS, 2 * QT, GQ), BF16),
                        pltpu.VMEM((2, KV_HEADS, 1, GQ), F32)],
        compiler_params=_cparams(("parallel", "arbitrary")), name="nsa_prompt",
    )(qT, gT, kcvc, kcvc, ksel, vselT, kwin, vwinT, tc, tiles, near, ov)


def _nsa_sample_kernel(pt_ref, *refs, pps, nsteps, nsb):
    pages = refs[:pps]
    (qbd_ref, gate_ref, kcvc_ref, win_ref, new_ref, bc_ref, bw_ref, bstep_ref, bnew_ref, ovt_ref, gg_ref, exp_ref,
     o_ref, madd_s, oc_s, ow_s, m_s, l_s, acc_s) = refs[pps:]
    st = pl.program_id(1)
    qbd = qbd_ref[0]
    ncol = qbd.shape[0]
    wb = win_ref.shape[2]
    bps = pps * (PAGE_SIZE // L_SEL)
    nlane = ovt_ref.shape[1]

    def online(s, pv_of):
        m_old = m_s[...]
        m_new = jnp.maximum(m_old, jnp.max(s, axis=1, keepdims=True))
        alpha = jnp.exp2(m_old - m_new)
        p = jnp.exp2(s - m_new)
        l_s[...] = alpha * l_s[...] + jnp.sum(p, axis=1, keepdims=True)
        acc_s[...] = alpha * acc_s[...] + pv_of(p.astype(BF16))
        m_s[...] = m_new

    @pl.when(st == 0)
    def _():
        m_s[...] = jnp.full(m_s.shape, NEG, F32)
        l_s[...] = jnp.zeros_like(l_s)
        acc_s[...] = jnp.zeros_like(acc_s)
        pn = _softmax2(_dot_nt(qbd, kcvc_ref[0, 0].astype(BF16)) + bc_ref[...], 1)
        oc_s[...] = _dot(pn.astype(BF16), kcvc_ref[0, 1].astype(BF16))
        imp = sum(_dot_tn(part, gg_ref[...]) for part in _split3(_dot_exact_lhs(pn, ovt_ref[...])))
        nrow = -(-nsb // 8) * 8
        blk = lax.broadcasted_iota(jnp.int32, (nrow, ncol), 0)
        qb = nsb - 1
        forced = (blk == 0) | (blk > qb - N_LOCAL)
        visible = blk <= qb
        score = jnp.where(visible & jnp.logical_not(forced), imp[0:nrow], ABSENT)
        picked = (visible & forced) | (_topk_masks([score], 0)[0] > 0.0)
        madd = jnp.concatenate([jnp.where(picked, 0.0, NEG),
                                jnp.full((nlane - nrow, ncol), NEG, F32)], axis=0).astype(BF16)
        for tl in range(nlane // LANES):
            madd_s[tl] = madd[tl * LANES:(tl + 1) * LANES, :]
        new = new_ref[0]
        s_w = jnp.concatenate([_dot(qbd, win_ref[0, 0:KV_DIM, :].astype(BF16)),
                               _dot_nt(qbd, new[:, 2 * KV_DIM:3 * KV_DIM].astype(BF16))], axis=1) + bw_ref[...]
        pw = _softmax2(s_w, 1).astype(BF16)
        ow_s[...] = (_dot_nt(pw[:, 0:wb], win_ref[0, KV_DIM:2 * KV_DIM, :].astype(BF16))
                     + _dot(pw[:, wb:], new[:, 3 * KV_DIM:4 * KV_DIM].astype(BF16)))

    kt = jnp.concatenate([pg[0, 0:KV_DIM, :] for pg in pages], axis=1).astype(BF16)
    vt = jnp.concatenate([pg[0, KV_DIM:2 * KV_DIM, :] for pg in pages], axis=1).astype(BF16)
    b0 = st * bps
    mask = _dot_tn(madd_s[b0 // LANES], exp_ref[(b0 % LANES) // bps])
    bias = jnp.where(st == nsteps - 1, bstep_ref[1], bstep_ref[0])
    online(_dot(qbd, kt) + mask + bias, lambda p: _dot_nt(p, vt))

    @pl.when(st == nsteps - 1)
    def _():
        new = new_ref[0]
        lb = nsb - 1
        spread = (lax.broadcasted_iota(jnp.int32, (LANES, PAGE_SIZE), 0) == lb % LANES).astype(BF16)
        online(_dot_nt(qbd, new[:, 0:KV_DIM].astype(BF16)) + bnew_ref[...] + _dot_tn(madd_s[lb // LANES], spread),
               lambda p: _dot(p, new[:, KV_DIM:2 * KV_DIM].astype(BF16)))
        gate = jax.nn.sigmoid(gate_ref[0])
        o_ref[0] = (gate[:, 0:1] * oc_s[...] + gate[:, 1:2] * (acc_s[...] * (1.0 / l_s[...]))
                    + gate[:, 2:3] * ow_s[...])


def _nsa_sample(cache, page_table, qbd, gates, kcvc, win, new_rows, bc, bw, bstep, bnew, ovt, gg, expand, pps):
    db, npg = page_table.shape
    nsteps = npg // pps
    nsb = npg * (PAGE_SIZE // L_SEL) + 1
    ncol = qbd.shape[1]
    nlane = ovt.shape[1]

    def page_spec(j):
        return pl.BlockSpec((1, 2 * KV_DIM, PAGE_SIZE), lambda bi, st, pt: (pt[bi, st * pps + j], 1, 0))

    const = lambda a: pl.BlockSpec(a.shape, lambda bi, st, pt: (0,) * a.ndim)
    seq = lambda a: pl.BlockSpec((1,) + a.shape[1:], lambda bi, st, pt: (bi,) + (0,) * (a.ndim - 1))
    grid_spec = pltpu.PrefetchScalarGridSpec(
        num_scalar_prefetch=1, grid=(db, nsteps),
        in_specs=[page_spec(j) for j in range(pps)] + [seq(qbd), seq(gates), seq(kcvc), seq(win), seq(new_rows)]
        + [const(a) for a in (bc, bw, bstep, bnew, ovt, gg, expand)],
        out_specs=pl.BlockSpec((1, ncol, KV_DIM), lambda bi, st, pt: (bi, 0, 0)),
        scratch_shapes=[pltpu.VMEM((nlane // LANES, LANES, ncol), BF16), pltpu.VMEM((ncol, KV_DIM), F32),
                        pltpu.VMEM((ncol, KV_DIM), F32), pltpu.VMEM((ncol, 1), F32), pltpu.VMEM((ncol, 1), F32),
                        pltpu.VMEM((ncol, KV_DIM), F32)])
    return pl.pallas_call(
        functools.partial(_nsa_sample_kernel, pps=pps, nsteps=nsteps, nsb=nsb), grid_spec=grid_spec,
        out_shape=jax.ShapeDtypeStruct((db, ncol, KV_DIM), F32),
        compiler_params=_cparams(("parallel", "arbitrary")), name="nsa_sample",
    )(page_table, *([cache] * pps), qbd, gates, kcvc, win, new_rows, bc, bw, bstep, bnew, ovt, gg, expand)


def _merge_kernel(x_ref, ylru_ref, ynsa_ref, mg_ref, wl_ref, wn_ref, wo_ref, gf_ref, wr_ref, br_ref,
                  x2_ref, hn_ref, wt_ref):
    gate = jax.nn.sigmoid(mg_ref[...])
    mixed = (gate[:, :D_MODEL] * _dot(ylru_ref[...].astype(BF16), wl_ref[...])
             + gate[:, D_MODEL:] * _dot(ynsa_ref[...].astype(BF16), wn_ref[...]))
    x2 = x_ref[...] + _dot(mixed.astype(BF16), wo_ref[...])
    x2_ref[...] = x2
    hn = x2 * lax.rsqrt(jnp.mean(x2 * x2, axis=-1, keepdims=True) + EPS) * gf_ref[...]
    hn_ref[...] = hn.astype(BF16)
    logits = jnp.dot(hn, wr_ref[...], precision=lax.Precision.HIGHEST, preferred_element_type=F32) + br_ref[...]
    lane = lax.broadcasted_iota(jnp.int32, logits.shape, 1)
    lanef = lane.astype(F32)
    big = float(LANES)
    gl = jnp.where(lane < N_GROUPS, logits, NEG)
    gmax = jnp.max(gl, axis=-1, keepdims=True)
    grp = jnp.min(jnp.where(gl == gmax, lanef, big), axis=-1, keepdims=True)
    p_grp = 1.0 / jnp.sum(jnp.where(lane < N_GROUPS, jnp.exp(gl - gmax), 0.0), axis=-1, keepdims=True)
    lo = N_GROUPS + grp * EXPERTS_PER_GROUP
    el = jnp.where((lanef >= lo) & (lanef < lo + EXPERTS_PER_GROUP), logits, NEG)
    v1 = jnp.max(el, axis=-1, keepdims=True)
    i1 = jnp.min(jnp.where(el == v1, lanef, big), axis=-1, keepdims=True)
    el2 = jnp.where(lanef == i1, NEG, el)
    v2 = jnp.max(el2, axis=-1, keepdims=True)
    i2 = jnp.min(jnp.where(el2 == v2, lanef, big), axis=-1, keepdims=True)
    e2 = jnp.exp(v2 - v1)
    den = 1.0 / (1.0 + e2)
    wt_ref[...] = jnp.where(lanef == i1, den * p_grp, jnp.where(lanef == i2, e2 * den * p_grp, 0.0))


def _merge(x2d, ylru, ynsa, mg, wl, wn, wo, gf, wr, br, tm=512):
    t = x2d.shape[0]
    tm = min(tm, t)
    tile = lambda a: pl.BlockSpec((tm, a.shape[1]), lambda i: (i, 0))
    const = lambda a: pl.BlockSpec(a.shape, lambda i: (0, 0), pipeline_mode=pl.Buffered(1))
    return pl.pallas_call(
        _merge_kernel, grid=(t // tm,),
        in_specs=[tile(x2d), tile(ylru), tile(ynsa), tile(mg)] + [const(a) for a in (wl, wn, wo, gf, wr, br)],
        out_specs=[pl.BlockSpec((tm, D_MODEL), lambda i: (i, 0)), pl.BlockSpec((tm, D_MODEL), lambda i: (i, 0)),
                   pl.BlockSpec((tm, LANES), lambda i: (i, 0))],
        out_shape=[jax.ShapeDtypeStruct((t, D_MODEL), F32), jax.ShapeDtypeStruct((t, D_MODEL), BF16),
                   jax.ShapeDtypeStruct((t, LANES), F32)],
        compiler_params=_cparams(("parallel",)), name="merge")(x2d, ylru, ynsa, mg, wl, wn, wo, gf, wr, br)


def _moe_kernel(hn_ref, wt_ref, x2_ref, wg_ref, wu_ref, wd_ref, gfin_ref, y_ref, acc):
    c = pl.program_id(1)

    @pl.when(c == 0)
    def _():
        acc[...] = jnp.zeros_like(acc)

    h = hn_ref[...]
    wt = wt_ref[...]
    lane = lax.broadcasted_iota(jnp.int32, wt.shape, 1)
    total = acc[...]
    for e in range(EXPERTS_PER_GROUP):
        act = jax.nn.silu(_dot(h, wg_ref[e])) * _dot(h, wu_ref[e])
        w_e = jnp.sum(jnp.where(lane == N_GROUPS + c * EXPERTS_PER_GROUP + e, wt, 0.0), axis=-1, keepdims=True)
        act = jnp.where(w_e != 0.0, act * w_e, 0.0)
        total = total + _dot(act.astype(BF16), wd_ref[e])
    acc[...] = total

    @pl.when(c == N_GROUPS - 1)
    def _():
        x = x2_ref[...] + total
        y_ref[...] = x * lax.rsqrt(jnp.mean(x * x, axis=-1, keepdims=True) + EPS) * gfin_ref[...]


def _moe(hn, wt, x2, wg, wu, wd, gfin, tm=512):
    t = hn.shape[0]
    tm = min(tm, t)
    return pl.pallas_call(
        _moe_kernel, grid=(t // tm, N_GROUPS),
        in_specs=[pl.BlockSpec((tm, D_MODEL), lambda i, c: (i, 0)), pl.BlockSpec((tm, LANES), lambda i, c: (i, 0)),
                  pl.BlockSpec((tm, D_MODEL), lambda i, c: (i, 0)),
                  pl.BlockSpec((EXPERTS_PER_GROUP, D_MODEL, D_EXPERT), lambda i, c: (c, 0, 0)),
                  pl.BlockSpec((EXPERTS_PER_GROUP, D_MODEL, D_EXPERT), lambda i, c: (c, 0, 0)),
                  pl.BlockSpec((EXPERTS_PER_GROUP, D_EXPERT, D_MODEL), lambda i, c: (c, 0, 0)),
                  pl.BlockSpec((1, D_MODEL), lambda i, c: (0, 0))],
        out_specs=pl.BlockSpec((tm, D_MODEL), lambda i, c: (i, 0)),
        out_shape=jax.ShapeDtypeStruct((t, D_MODEL), F32),
        scratch_shapes=[pltpu.VMEM((tm, D_MODEL), F32)],
        compiler_params=_cparams(("parallel", "arbitrary")), name="moe")(hn, wt, x2, wg, wu, wd, gfin)


def _bucket_table():
    n = np.arange(MAX_DISTANCE + 1)
    exact = N_BUCKETS // 2
    nf = np.maximum(n, exact).astype(np.float64)
    large = exact + (np.log(nf / exact) / math.log(MAX_DISTANCE / exact) * (N_BUCKETS - exact)).astype(np.int32)
    return np.where(n < exact, n, np.minimum(large, N_BUCKETS - 1))


def _bias_of(rel_b, rel):
    idx = jnp.asarray(np.clip(rel, 0, MAX_DISTANCE).reshape(-1, 1), jnp.int32)
    onehot = (idx == jnp.arange(MAX_DISTANCE + 1, dtype=jnp.int32)[None, :]).astype(F32)
    t = jnp.dot(onehot, rel_b * LOG2E, precision=lax.Precision.HIGHEST, preferred_element_type=F32)
    return t.reshape(rel.shape + (N_HEADS,))


def _prompt_tile(rel_b, rel, mask, minus=None):
    r = rel.shape[0]
    t = _bias_of(rel_b, rel).reshape(r, QT, KV_HEADS, GROUP).transpose(2, 0, 3, 1).reshape(KV_HEADS, r, GQ)
    if minus is not None:
        t = t - minus
    m = np.broadcast_to(np.asarray(mask)[None, :, None, :], (KV_HEADS, r, GROUP, QT)).reshape(KV_HEADS, r, GQ)
    return jnp.where(jnp.asarray(m), t, NEG)


def _sample_tile(rel_b, rel, mask):
    r, ds = rel.shape
    t = _bias_of(rel_b, rel)
    t = jnp.where(jnp.asarray(mask)[..., None], t, NEG)
    return t.transpose(2, 1, 0).reshape(N_HEADS * ds, r)


def _block_diag(w):
    eye = jnp.eye(LRU_HEADS, dtype=w.dtype)
    return jnp.einsum('hij,hk->hikj', w, eye).reshape(D_LRU, D_LRU)


def kernel(x_prompt, x_sample, cache_kv, state_kv_win, state_conv, state_h, page_table, g_mix, w_in, conv_w, conv_b,
           w_gate_a, b_gate_a, w_gate_x, b_gate_x, lru_lambda, cmp_w1, cmp_b1, cmp_w2, w_lru_out, w_nsa_out, w_out,
           g_ffn, w_router_group, b_router_group, w_router_expert, b_router_expert, w_exp_gate, w_exp_up,
           w_exp_down, rel_bias, g_final):
    assert w_in.shape[0] == 1, "single layer"
    b, s, _ = x_prompt.shape
    db, ds, _ = x_sample.shape
    npg = page_table.shape[1]
    past = npg * PAGE_SIZE
    wb = state_kv_win.shape[2]
    assert s % 256 == 0 and s // L_SEL >= N_SEL and CONV_W - 1 <= ds <= STRIDE and wb == WINDOW and past >= WINDOW

    w = w_in[0].astype(BF16)
    o = 0
    ws = []
    for width in (D_LRU, D_LRU, Q_DIM, 4 * KV_DIM, 2 * KV_DIM, 3 * N_HEADS, 2 * D_MODEL):
        ws.append(w[:, o:o + width])
        o += width
    ws[5] = jnp.pad(ws[5], ((0, 0), (0, LANES - 3 * N_HEADS)))
    g_mix2 = g_mix[0][None]
    wg = jnp.concatenate([_block_diag(w_gate_x[0]), _block_diag(w_gate_a[0])], axis=1).astype(BF16)
    lru_args = (conv_w[0], conv_b[0][None], wg, b_gate_x[0][None], b_gate_a[0][None], lru_lambda[0][None])
    w1 = cmp_w1[0].reshape(2, 2, STRIDE, HEAD_DIM, CMP_HIDDEN)
    w1f = jnp.einsum('armdn,hk->amhdkrn', w1, jnp.eye(HEADS_PER_TILE, dtype=F32)).reshape(
        2, STRIDE * LANES, HEADS_PER_TILE * 2 * CMP_HIDDEN).astype(BF16)
    b1 = cmp_b1[0][:, None, :]
    w2 = cmp_w2[0].astype(BF16)
    wl, wn, wo = w_lru_out[0].astype(BF16), w_nsa_out[0].astype(BF16), w_out[0].astype(BF16)
    n_r = N_GROUPS + N_EXPERTS
    wr = jnp.pad(jnp.concatenate([w_router_group[0], w_router_expert[0]], axis=1), ((0, 0), (0, LANES - n_r)))
    br = jnp.pad(jnp.concatenate([b_router_group[0], b_router_expert[0]]), (0, LANES - n_r))[None]
    weg, weu, wed = w_exp_gate[0].astype(BF16), w_exp_up[0].astype(BF16), w_exp_down[0].astype(BF16)
    gf, gfin = g_ffn[0][None], g_final[None]
    rel_b = rel_bias.astype(F32)[_bucket_table()]

    def mix_ffn(x2d, ylru, ynsa, mg):
        x2, hn, wt = _merge(x2d, ylru, ynsa, mg, wl, wn, wo, gf, wr, br)
        return _moe(hn, wt, x2, weg, weu, wed, gfin)

    xp2 = x_prompt.reshape(b * s, D_MODEL)
    lx, lg, _, kv, kvw, ng, mg, qT, kv_t = _proj(xp2, g_mix2, ws, s, feature_major=(2, 3))
    y_lru, conv_p, h_p = _rglru_prompt(lx.reshape(b, s, D_LRU), lg.reshape(b, s, D_LRU), *lru_args)
    kv3 = kv.reshape(b, s, 4 * KV_DIM)
    kvw3 = kvw.reshape(b, s, 2 * KV_DIM)
    kcvc = _compress_prompt(kv3, w1f, b1, w2)
    nt = s // QT
    ncp = s // STRIDE
    nsb = s // L_SEL

    def key_tiles(x):
        return x.astype(BF16).reshape(b, nt, QT, KV_DIM)

    def val_tiles(x):
        return x.astype(BF16).reshape(b, nt, QT, KV_DIM).swapaxes(2, 3)

    qi = np.arange(QT)[None, :]
    lrow = np.arange(ncp)[:, None]
    rel_c = qi - STRIDE * lrow + STRIDE * ncp - (STRIDE * 8 + L_CMP - 1)
    n_far = int(np.sum(np.all(rel_c >= MAX_DISTANCE, axis=1)))
    crow = _prompt_tile(rel_b, np.full((1, QT), MAX_DISTANCE), np.ones((1, QT), bool))
    tc = jnp.concatenate([jnp.broadcast_to(crow, (KV_HEADS, n_far, GQ)),
                          _prompt_tile(rel_b, rel_c[n_far:], rel_c[n_far:] >= 0)], axis=1)
    kj = np.arange(QT)[:, None]
    diag = _prompt_tile(rel_b, qi - kj, qi - kj >= 0, crow)
    tiles = jnp.stack([diag, _prompt_tile(rel_b, WINDOW + qi - kj, qi - kj < 0, crow)])
    near = jnp.concatenate([_prompt_tile(rel_b, QT + qi - kj, np.ones((QT, QT), bool), crow), diag], axis=1)
    rr = np.arange(nsb)[:, None]
    ll = np.arange(ncp)[None, :]
    ov = jnp.asarray((ll >= 4 * rr - 1) & (ll <= 4 * rr + 3), BF16)
    gT = ng[:, :3 * N_HEADS].reshape(b, s, 3 * N_HEADS).swapaxes(1, 2)
    bpt = QT // L_SEL
    code = (np.arange(nt)[:, None] % 2) * bpt + np.arange(QT)[None, :] // L_SEL
    mark = jnp.asarray(code[..., None] == np.arange(LANES - HEAD_DIM)[None, None, :], BF16)
    ks = kv3[..., 2 * KV_DIM:3 * KV_DIM].astype(BF16).reshape(b, nt, QT, KV_HEADS, HEAD_DIM)
    ksel_aug = jnp.concatenate([ks, jnp.broadcast_to(mark[None, :, :, None, :], ks.shape[:4] + (LANES - HEAD_DIM,))],
                               axis=-1).reshape(b, nt, QT, KV_HEADS * LANES)
    y_nsaT = _nsa_prompt(qT, gT, kcvc, ksel_aug, val_tiles(kv3[..., 3 * KV_DIM:]), key_tiles(kvw3[..., :KV_DIM]),
                         val_tiles(kvw3[..., KV_DIM:]), tc, tiles, near, ov)
    y_nsa = y_nsaT.swapaxes(1, 2).reshape(b * s, Q_DIM)
    y_prompt = mix_ffn(xp2, y_lru.reshape(b * s, D_LRU), y_nsa, mg).reshape(b, s, D_MODEL)
    kv_rows_prompt = kv_t.reshape(1, b, 4, KV_HEADS, HEAD_DIM, s).transpose(0, 1, 5, 2, 3, 4)
    win_prompt = kvw3[:, s - min(WINDOW, s):].reshape(1, b, min(WINDOW, s), 2, KV_HEADS, HEAD_DIM)

    xs2 = x_sample.reshape(db * ds, D_MODEL)
    lx, lg, q, kv, kvw, ng, mg = _proj(xs2, g_mix2, ws, db * ds)
    tmaj =lambda a: a.reshape(db, ds, D_LRU).swapaxes(0, 1)
    y_lru_t, conv_t, h_s = _rglru_sample(tmaj(lx), tmaj(lg), state_conv[0].swapaxes(0, 1), state_h[0], *lru_args)
    y_lru = y_lru_t.swapaxes(0, 1).reshape(db * ds, D_LRU)
    kv_s3 = kv.reshape(db, ds, 4 * KV_DIM)
    kvw_s3 = kvw.reshape(db, ds, 2 * KV_DIM)
    cache_t = cache_kv[0].transpose(0, 2, 3, 4, 1).reshape(cache_kv.shape[1], 4 * KV_DIM, PAGE_SIZE)
    win_t = state_kv_win[0].transpose(0, 2, 3, 4, 1).reshape(db, 2 * KV_DIM, wb)
    new_c = jnp.pad(kv_s3[..., :2 * KV_DIM], ((0, 0), (0, PAGE_SIZE - ds), (0, 0)))
    pps = 32
    assert npg % pps == 0 and LANES % (pps * PAGE_SIZE // L_SEL) == 0
    kcvc_s = _compress_sample(cache_t, page_table, new_c, w1f, b1, w2, pps)

    nsb_s = past // L_SEL + 1
    ntok = past // STRIDE
    jq = np.arange(ds)[None, :]
    nrow = np.arange(ntok)[:, None]
    rel = past + jq - STRIDE * nrow - (L_CMP - 1)
    bc = _sample_tile(rel_b, rel, rel >= 0)
    rn = np.arange(PAGE_SIZE)[:, None]
    new_rel, new_ok = jq - rn, (jq - rn >= 0) & (rn < ds)
    rw = np.arange(wb)[:, None]
    bw = _sample_tile(rel_b, np.concatenate([wb + jq - rw, new_rel], axis=0),
                      np.concatenate([(wb + jq - rw < WINDOW) & (past - wb + rw >= 0), new_ok], axis=0))
    bnew = _sample_tile(rel_b, new_rel, new_ok)
    step_keys = pps * PAGE_SIZE
    far_rel = np.full((step_keys, ds), MAX_DISTANCE)
    last_rel = far_rel.copy()
    last_rel[step_keys - PAGE_SIZE:] = PAGE_SIZE + jq - rn
    all_ok = np.ones((step_keys, ds), bool)
    bstep = jnp.stack([_sample_tile(rel_b, far_rel, all_ok), _sample_tile(rel_b, last_rel, all_ok)])
    nlane = -(-nsb_s // LANES) * LANES
    bb = np.arange(nlane)[None, :]
    tt = np.arange(ntok)[:, None]
    ovt = jnp.asarray((tt >= 4 * bb - 1) & (tt <= 4 * bb + 3) & (bb < nsb_s), BF16)
    ncol = N_HEADS * ds
    col = np.arange(ncol)
    kvh_c, j_c = col // (GROUP * ds), col % ds
    gg = jnp.asarray((kvh_c[:, None] == kvh_c[None, :]) & (j_c[:, None] == j_c[None, :]), BF16)
    bps = pps * PAGE_SIZE // L_SEL
    off = np.arange(LANES // bps)[:, None, None]
    expand = jnp.asarray(np.arange(LANES)[None, :, None] == bps * off + np.arange(step_keys)[None, None, :] // L_SEL,
                         BF16)
    q5 = q.reshape(db, ds, KV_HEADS, GROUP, HEAD_DIM) * QK_SCALE
    qbd = jnp.einsum('bjkgd,kc->bkgjcd', q5, jnp.eye(KV_HEADS, dtype=F32)).reshape(db, ncol, KV_DIM).astype(BF16)
    gates_s = ng[:, :3 * N_HEADS].reshape(db, ds, N_HEADS, 3).transpose(0, 2, 1, 3).reshape(db, ncol, 3)
    new_all = jnp.pad(jnp.concatenate([kv_s3[..., 2 * KV_DIM:], kvw_s3], axis=-1),
                      ((0, 0), (0, PAGE_SIZE - ds), (0, 0)))
    o_bd = _nsa_sample(cache_t, page_table, qbd, gates_s, kcvc_s, win_t, new_all, bc, bw, bstep, bnew, ovt, gg,
                       expand, pps)
    o6 = o_bd.reshape(db, KV_HEADS, GROUP, ds, KV_HEADS, HEAD_DIM)
    y_nsa = jnp.einsum('bkgjkd->bjkgd', o6).reshape(db * ds, Q_DIM)
    y_sample = mix_ffn(xs2, y_lru, y_nsa, mg).reshape(db, ds, D_MODEL)
    kv_rows_sample = kv_s3.reshape(1, db, ds, 4, KV_HEADS, HEAD_DIM)
    win_all = jnp.concatenate([state_kv_win[0].reshape(db, wb, 2 * KV_DIM), kvw_s3], axis=1)
    win_sample = win_all[:, win_all.shape[1] - WINDOW:].reshape(1, db, WINDOW, 2, KV_HEADS, HEAD_DIM)
    conv_s = conv_t.swapaxes(0, 1)

    return (y_prompt, y_sample, kv_rows_prompt, win_prompt, conv_p[None], h_p.reshape(1, b, D_LRU),
            kv_rows_sample, win_sample, conv_s[None], h_s[None])
```
